```python
import math
import jax, jax.numpy as jnp
from jax import lax
import numpy as np

D_MODEL = 2048
BATCH = 1
SEQ = 8192
DEPTH = 1

BRANCH_WIDTH = 1024
N_BRANCH = 3
A_HEADS = 8
A_KV_HEADS = 2
A_HEAD_DIM = 128
A_WIDTH = A_HEADS * A_HEAD_DIM
IDX_HEADS = 16
IDX_DIM = 64
TOPK_MAX = 256
IDX_SCALE = (IDX_DIM ** -0.5) * (IDX_HEADS ** -0.5)
A_SCALE = A_HEAD_DIM ** -0.5
B_HEADS = 8
B_Q_RANK = 512
B_KV_RANK = 256
B_NOPE = 128
B_ROPE = 64
B_QK_DIM = B_NOPE + B_ROPE
B_V = 128
B_WIDTH = B_HEADS * B_V
B_SCALE = B_QK_DIM ** -0.5
ROPE_THETA = 10000.0
N_MEM = 256
C_HEADS = 4
C_HEAD_DIM = 256
C_WIDTH = C_HEADS * C_HEAD_DIM
C_SCALE = C_HEAD_DIM ** -0.5
REL_BUCKETS = 32
REL_MAX_DIST = 128
Q_BLOCK = 128
EPS = 1e-6

IN_SIZES = (
    A_WIDTH,
    A_KV_HEADS * A_HEAD_DIM,
    A_KV_HEADS * A_HEAD_DIM,
    A_WIDTH,
    IDX_HEADS * IDX_DIM,
    IDX_DIM,
    IDX_HEADS,
    B_Q_RANK,
    B_KV_RANK,
    B_ROPE,
    B_WIDTH,
    C_WIDTH,
    C_WIDTH,
    N_BRANCH * D_MODEL,
)
IN_WIDTH = int(sum(IN_SIZES))
IN_OFFSETS = [int(o) for o in np.cumsum(IN_SIZES)[:-1]]

kernel_name = "hybrid_dsa_mla_memory_gated_block"


def rms_norm(x, g):
    x32 = x.astype(jnp.float32)
    y = x32 * lax.rsqrt(jnp.mean(x32 * x32, axis=-1, keepdims=True) + EPS)
    return (y * g.astype(jnp.float32)).astype(x.dtype)


def apply_rope(x, cos, sin):
    x32 = x.astype(jnp.float32)
    half = x.shape[-1] // 2
    x1, x2 = x32[..., :half], x32[..., half:]
    return jnp.concatenate([x1 * cos - x2 * sin, x2 * cos + x1 * sin], axis=-1).astype(x.dtype)


def t5_bucket(dist):
    n = jnp.maximum(dist, 0)
    max_exact = REL_BUCKETS // 2
    nf = jnp.maximum(n, 1).astype(jnp.float32)
    large = max_exact + (jnp.log(nf / max_exact) / math.log(REL_MAX_DIST / max_exact)
                         * (REL_BUCKETS - max_exact)).astype(jnp.int32)
    large = jnp.minimum(large, REL_BUCKETS - 1)
    return jnp.where(n < max_exact, n, large)


_gather_rows = jax.vmap(lambda a, i: a[i])


def dsa_attention(q, k, v, q_idx, k_idx, w_idx, pos, rel_table):
    bsz, seq = q.shape[0], q.shape[1]
    topk = min(TOPK_MAX, seq // 4)
    n_blocks = seq // Q_BLOCK
    rep = A_HEADS // A_KV_HEADS
    key_ids = jnp.arange(seq)
    k_idx32 = k_idx.astype(jnp.float32)

    def block(i):
        start = i * Q_BLOCK
        qi = lax.dynamic_slice_in_dim(q_idx, start, Q_BLOCK, axis=1).astype(jnp.float32)
        wi = lax.dynamic_slice_in_dim(w_idx, start, Q_BLOCK, axis=1).astype(jnp.float32)
        qb = lax.dynamic_slice_in_dim(q, start, Q_BLOCK, axis=1)
        pq = lax.dynamic_slice_in_dim(pos, start, Q_BLOCK, axis=1)
        t_ids = start + jnp.arange(Q_BLOCK)
        causal = key_ids[None, :] <= t_ids[:, None]
        dots = jnp.einsum('bthd,bsd->bths', qi, k_idx32)
        score = jnp.einsum('bths,bth->bts', jax.nn.relu(dots), wi) * IDX_SCALE
        score = jnp.where(causal[None], score, -jnp.inf)
        _, sel = lax.top_k(score, topk)
        kg = _gather_rows(k, sel)
        vg = _gather_rows(v, sel)
        pk = _gather_rows(pos, sel)
        qg = qb.reshape(bsz, Q_BLOCK, A_KV_HEADS, rep, A_HEAD_DIM)
        logits = jnp.einsum('btgrd,btkgd->btgrk', qg, kg).astype(jnp.float32) * A_SCALE
        bias = rel_table[t5_bucket(pq[:, :, None] - pk)].astype(jnp.float32)
        bias = bias.reshape(bsz, Q_BLOCK, topk, A_KV_HEADS, rep).transpose(0, 1, 3, 4, 2)
        valid = sel <= t_ids[None, :, None]
        logits = jnp.where(valid[:, :, None, None, :], logits + bias, -jnp.inf)
        p = jax.nn.softmax(logits, axis=-1).astype(v.dtype)
        o = jnp.einsum('btgrk,btkgd->btgrd', p, vg)
        return o.reshape(bsz, Q_BLOCK, A_HEADS, A_HEAD_DIM)

    out = lax.map(block, jnp.arange(n_blocks))
    return jnp.moveaxis(out, 0, 1).reshape(bsz, seq, A_HEADS, A_HEAD_DIM)


def mla_attention(q, k, v):
    bsz, seq = q.shape[0], q.shape[1]
    n_blocks = seq // Q_BLOCK
    key_ids = jnp.arange(seq)

    def block(i):
        start = i * Q_BLOCK
        qb = lax.dynamic_slice_in_dim(q, start, Q_BLOCK, axis=1)
        logits = jnp.einsum('bthd,bshd->bhts', qb, k).astype(jnp.float32) * B_SCALE
        causal = key_ids[None, :] <= (start + jnp.arange(Q_BLOCK))[:, None]
        logits = jnp.where(causal, logits, -jnp.inf)
        p = jax.nn.softmax(logits, axis=-1).astype(v.dtype)
        return jnp.einsum('bhts,bshd->bthd', p, v)

    out = lax.map(block, jnp.arange(n_blocks))
    return jnp.moveaxis(out, 0, 1).reshape(bsz, seq, q.shape[2], v.shape[-1])


def memory_attention(q, k, v):
    logits = jnp.einsum('bshd,bmhd->bhsm', q, k).astype(jnp.float32) * C_SCALE
    p = jax.nn.softmax(logits, axis=-1).astype(v.dtype)
    return jnp.einsum('bhsm,bmhd->bshd', p, v)


def setup_inputs(seed: int = 0) -> dict:
    key = jax.random.key(seed)
    ks = jax.random.split(key, 24)
    f32 = jnp.float32

    def nrm(k, shape, fan_in):
        return jax.random.normal(k, shape, f32) * (fan_in ** -0.5)

    def gain(k, shape):
        return 1.0 + 0.02 * jax.random.normal(k, shape, f32)

    x = jax.random.normal(ks[0], (BATCH, SEQ, D_MODEL), f32)
    mem = jax.random.normal(ks[1], (BATCH, N_MEM, D_MODEL), f32)
    positions = jnp.broadcast_to(jnp.arange(SEQ, dtype=jnp.int32)[None, :], (BATCH, SEQ))
    rel_bias = 0.5 * jax.random.normal(ks[2], (REL_BUCKETS, A_HEADS), f32)
    return {
        "x": x,
        "mem": mem,
        "positions": positions,
        "rel_bias": rel_bias,
        "norm_g": gain(ks[3], (DEPTH, D_MODEL)),
        "mem_norm_g": gain(ks[4], (DEPTH, D_MODEL)),
        "w_in": nrm(ks[5], (DEPTH, D_MODEL, IN_WIDTH), D_MODEL),
        "a_q_norm_g": gain(ks[6], (DEPTH, A_HEAD_DIM)),
        "a_k_norm_g": gain(ks[7], (DEPTH, A_HEAD_DIM)),
        "b_q_lat_norm_g": gain(ks[8], (DEPTH, B_Q_RANK)),
        "b_kv_lat_norm_g": gain(ks[9], (DEPTH, B_KV_RANK)),
        "w_b_uq": nrm(ks[10], (DEPTH, B_Q_RANK, B_HEADS * B_QK_DIM), B_Q_RANK),
        "w_b_ukv": nrm(ks[11], (DEPTH, B_KV_RANK, B_HEADS * (B_NOPE + B_V)), B_KV_RANK),
        "b_q_norm_g": gain(ks[12], (DEPTH, B_QK_DIM)),
        "b_k_norm_g": gain(ks[13], (DEPTH, B_QK_DIM)),
        "w_mem_kv": nrm(ks[14], (DEPTH, D_MODEL, 2 * C_WIDTH), D_MODEL),
        "c_q_norm_g": gain(ks[15], (DEPTH, C_HEAD_DIM)),
        "c_k_norm_g": gain(ks[16], (DEPTH, C_HEAD_DIM)),
        "w_branch": nrm(ks[17], (DEPTH, N_BRANCH, BRANCH_WIDTH, D_MODEL), BRANCH_WIDTH),
        "w_out": nrm(ks[18], (DEPTH, D_MODEL, D_MODEL), D_MODEL),
    }


def reference(x, mem, positions, rel_bias, norm_g, mem_norm_g, w_in, a_q_norm_g, a_k_norm_g,
              b_q_lat_norm_g, b_kv_lat_norm_g, w_b_uq, w_b_ukv, b_q_norm_g, b_k_norm_g,
              w_mem_kv, c_q_norm_g, c_k_norm_g, w_branch, w_out):
    bsz, seq, d = x.shape
    n_mem = mem.shape[1]
    inv_freq = 1.0 / (ROPE_THETA ** (jnp.arange(0, B_ROPE, 2, dtype=jnp.float32) / B_ROPE))
    ang = positions.astype(jnp.float32)[..., None] * inv_freq
    cos = jnp.cos(ang)[:, :, None, :]
    sin = jnp.sin(ang)[:, :, None, :]

    for l in range(DEPTH):
        h = rms_norm(x, norm_g[l])
        proj = h @ w_in[l]
        (aq, ak, av, az, iq, ik, iw, bcq, bckv, bkpe, bz, cq, cz, gates) = jnp.split(
            proj, IN_OFFSETS, axis=-1)

        aq = rms_norm(aq.reshape(bsz, seq, A_HEADS, A_HEAD_DIM), a_q_norm_g[l])
        ak = rms_norm(ak.reshape(bsz, seq, A_KV_HEADS, A_HEAD_DIM), a_k_norm_g[l])
        av = av.reshape(bsz, seq, A_KV_HEADS, A_HEAD_DIM)
        iq = iq.reshape(bsz, seq, IDX_HEADS, IDX_DIM)
        o_a = dsa_attention(aq, ak, av, iq, ik, iw, positions, rel_bias).reshape(bsz, seq, A_WIDTH)

        qb = (rms_norm(bcq, b_q_lat_norm_g[l]) @ w_b_uq[l]).reshape(bsz, seq, B_HEADS, B_QK_DIM)
        qb = rms_norm(qb, b_q_norm_g[l])
        qb = jnp.concatenate([qb[..., :B_NOPE], apply_rope(qb[..., B_NOPE:], cos, sin)], axis=-1)
        kv = (rms_norm(bckv, b_kv_lat_norm_g[l]) @ w_b_ukv[l]).reshape(bsz, seq, B_HEADS, B_NOPE + B_V)
        k_nope, vb = kv[..., :B_NOPE], kv[..., B_NOPE:]
        k_pe = jnp.broadcast_to(bkpe[:, :, None, :], (bsz, seq, B_HEADS, B_ROPE))
        kb = rms_norm(jnp.concatenate([k_nope, k_pe], axis=-1), b_k_norm_g[l])
        kb = jnp.concatenate([kb[..., :B_NOPE], apply_rope(kb[..., B_NOPE:], cos, sin)], axis=-1)
        o_b = mla_attention(qb, kb, vb).reshape(bsz, seq, B_WIDTH)

        mkv = rms_norm(mem, mem_norm_g[l]) @ w_mem_kv[l]
        ck = rms_norm(mkv[..., :C_WIDTH].reshape(bsz, n_mem, C_HEADS, C_HEAD_DIM), c_k_norm_g[l])
        cv = mkv[..., C_WIDTH:].reshape(bsz, n_mem, C_HEADS, C_HEAD_DIM)
        cq = rms_norm(cq.reshape(bsz, seq, C_HEADS, C_HEAD_DIM), c_q_norm_g[l])
        o_c = memory_attention(cq, ck, cv).reshape(bsz, seq, C_WIDTH)

        u = jnp.stack([o_a * jax.nn.silu(az), o_b * jax.nn.silu(bz), o_c * jax.nn.silu(cz)],
                      axis=2)
        y_br = jnp.einsum('bsnw,nwd->bsnd', u, w_branch[l])
        g = jax.nn.sigmoid(gates.reshape(bsz, seq, N_BRANCH, d))
        merged = jnp.einsum('bsnd,bsnd->bsd', g, y_br)
        x = x + merged @ w_out[l]
    return x
```

```python
import functools
import math

import numpy as np
import jax
import jax.numpy as jnp
from jax import lax
from jax.experimental import pallas as pl
from jax.experimental.pallas import tpu as pltpu

F32 = jnp.float32
BF16 = jnp.bfloat16
I32 = jnp.int32

D_MODEL = 2048
BRANCH_WIDTH = 1024
N_BRANCH = 3
A_HEADS = 8
A_KV_HEADS = 2
A_HEAD_DIM = 128
A_WIDTH = A_HEADS * A_HEAD_DIM
A_KV_WIDTH = A_KV_HEADS * A_HEAD_DIM
A_REP = A_HEADS // A_KV_HEADS
IDX_HEADS = 16
IDX_DIM = 64
TOPK_MAX = 256
IDX_SCALE = (IDX_DIM ** -0.5) * (IDX_HEADS ** -0.5)
A_SCALE = A_HEAD_DIM ** -0.5
B_HEADS = 8
B_Q_RANK = 512
B_KV_RANK = 256
B_NOPE = 128
B_ROPE = 64
B_QK_DIM = B_NOPE + B_ROPE
B_V = 128
B_WIDTH = B_HEADS * B_V
B_SCALE = B_QK_DIM ** -0.5
ROPE_THETA = 10000.0
C_HEADS = 4
C_HEAD_DIM = 256
C_WIDTH = C_HEADS * C_HEAD_DIM
C_SCALE = C_HEAD_DIM ** -0.5
REL_BUCKETS = 32
REL_MAX_DIST = 128
EPS = 1e-6

IN_SIZES = (A_WIDTH, A_KV_WIDTH, A_KV_WIDTH, A_WIDTH, IDX_HEADS * IDX_DIM, IDX_DIM, IDX_HEADS,
            B_Q_RANK, B_KV_RANK, B_ROPE, B_WIDTH, C_WIDTH, C_WIDTH, N_BRANCH * D_MODEL)
IN_OFFSETS = [int(o) for o in np.cumsum(IN_SIZES)[:-1]]

LANES = 128
B_QK_PAD = 2 * LANES
VMEM_LIMIT_BYTES = 56 * 1024 * 1024
MASK_VALUE = -1e30

SEG_UNITS = dict(gates=48, aq=8, az=8, iq=8, bz=8, cq=8, cz=8, bcq=4, ak=2, av=2, bckv=2, ikw=1, bkpe=1)
SEG_START = {}
_u = 0
for _name, _w in SEG_UNITS.items():
    SEG_START[_name] = _u
    _u += _w
PROJ_UNITS = _u
PROJ_WIDTH = PROJ_UNITS * LANES


def _t5_bucket_table():
    d = np.arange(LANES)
    max_exact = REL_BUCKETS // 2
    nf = np.maximum(d, 1).astype(np.float64)
    large = max_exact + (np.log(nf / max_exact) / math.log(REL_MAX_DIST / max_exact)
                         * (REL_BUCKETS - max_exact)).astype(np.int64)
    large = np.minimum(large, REL_BUCKETS - 1)
    table = np.where(d < max_exact, d, large).astype(np.int32)
    far = int(np.min(np.nonzero(table == REL_BUCKETS - 1)[0]))
    assert np.all(table[far:] == REL_BUCKETS - 1)
    return table, far


T5_TABLE, T5_FAR = _t5_bucket_table()


def _cparams(*sem):
    return pltpu.CompilerParams(dimension_semantics=sem, vmem_limit_bytes=VMEM_LIMIT_BYTES)


def _dot_t(a, b):
    return lax.dot_general(a, b, (((1,), (1,)), ((), ())), preferred_element_type=F32)


def _rms_scale(ss, n):
    return lax.rsqrt(ss * (1.0 / n) + EPS)


def _mem_kv_kernel(mem_ref, g_ref, w_ref, ckn_ref, o_ref, h_ref):
    j = pl.program_id(0)

    @pl.when(j == 0)
    def _():
        m = mem_ref[...]
        r = _rms_scale(jnp.sum(m * m, axis=-1, keepdims=True), D_MODEL)
        h_ref[...] = (m * r * g_ref[...]).astype(BF16)

    y = jnp.dot(h_ref[...], w_ref[...].astype(BF16), preferred_element_type=F32)

    @pl.when(j < C_HEADS)
    def _():
        r = _rms_scale(jnp.sum(y * y, axis=-1, keepdims=True), C_HEAD_DIM)
        o_ref[...] = (y * r * ckn_ref[...]).astype(BF16)

    @pl.when(j >= C_HEADS)
    def _():
        o_ref[...] = y.astype(BF16)


def _mem_kv(mem, mem_g, w_mem_kv, ck_g):
    n_mem = mem.shape[0]
    return pl.pallas_call(
        _mem_kv_kernel,
        grid=(2 * C_HEADS,),
        in_specs=[
            pl.BlockSpec((n_mem, D_MODEL), lambda j: (0, 0)),
            pl.BlockSpec((1, D_MODEL), lambda j: (0, 0)),
            pl.BlockSpec((D_MODEL, C_HEAD_DIM), lambda j: (0, j)),
            pl.BlockSpec((1, C_HEAD_DIM), lambda j: (0, 0)),
        ],
        out_specs=pl.BlockSpec((n_mem, C_HEAD_DIM), lambda j: (0, j)),
        out_shape=jax.ShapeDtypeStruct((n_mem, 2 * C_WIDTH), BF16),
        scratch_shapes=[pltpu.VMEM((n_mem, D_MODEL), BF16)],
        compiler_params=_cparams("arbitrary"),
        name="mem_kv",
    )(mem, mem_g, w_mem_kv, ck_g)


def _in_proj_kernel(x_ref, g_ref, w_ref, o_ref, h_ref, *, row_chunk):
    @pl.when(pl.program_id(1) == 0)
    def _():
        g = g_ref[...]
        for c in range(x_ref.shape[0] // row_chunk):
            rows = pl.ds(c * row_chunk, row_chunk)
            x = x_ref[rows, :]
            r = _rms_scale(jnp.sum(x * x, axis=-1, keepdims=True), D_MODEL)
            h_ref[rows, :] = (x * r * g).astype(BF16)

    o_ref[...] = jnp.dot(h_ref[...], w_ref[...], preferred_element_type=F32).astype(o_ref.dtype)


def _in_proj(x, norm_g, w_cat, *, tm, tn):
    s = x.shape[0]
    return pl.pallas_call(
        functools.partial(_in_proj_kernel, row_chunk=min(tm, 256)),
        grid=(s // tm, PROJ_WIDTH // tn),
        in_specs=[
            pl.BlockSpec((tm, D_MODEL), lambda i, j: (i, 0)),
            pl.BlockSpec((1, D_MODEL), lambda i, j: (0, 0)),
            pl.BlockSpec((D_MODEL, tn), lambda i, j: (0, j)),
        ],
        out_specs=pl.BlockSpec((tm, tn), lambda i, j: (i, j)),
        out_shape=jax.ShapeDtypeStruct((s, PROJ_WIDTH), BF16),
        scratch_shapes=[pltpu.VMEM((tm, D_MODEL), BF16)],
        compiler_params=_cparams("parallel", "arbitrary"),
        name="in_proj",
    )(x, norm_g, w_cat)


def _rope128(x, cos, sin):
    half = B_ROPE // 2
    lane = lax.broadcasted_iota(I32, x.shape, 1)
    partner = jnp.where(lane < half, pltpu.roll(x, LANES - half, 1), pltpu.roll(x, half, 1))
    return x * cos + partner * sin


def _prep_kernel(aq_ref, ak_ref, ikw_ref, bcq_ref, bckv_ref, bkpe_ref, cq_ref,
                 cos_ref, sin_ref, aqg_ref, akg_ref, bqlg_ref, bkvlg_ref, bqg_ref, bkg_ref, cqg_ref,
                 wuq_ref, wukv_ref,
                 aqn_ref, akn_ref, kab_ref, wsc_ref, qb_ref, kb_ref, vb_ref, cqn_ref):
    cos = cos_ref[...]
    sin = sin_ref[...]

    aqg = aqg_ref[...] * A_SCALE
    for h in range(A_HEADS):
        cols = slice(h * A_HEAD_DIM, (h + 1) * A_HEAD_DIM)
        v = aq_ref[:, cols].astype(F32)
        r = _rms_scale(jnp.sum(v * v, axis=-1, keepdims=True), A_HEAD_DIM)
        aqn_ref[:, cols] = (v * r * aqg).astype(BF16)
    akg = akg_ref[...]
    for h in range(A_KV_HEADS):
        cols = slice(h * A_HEAD_DIM, (h + 1) * A_HEAD_DIM)
        v = ak_ref[:, cols].astype(F32)
        r = _rms_scale(jnp.sum(v * v, axis=-1, keepdims=True), A_HEAD_DIM)
        akn_ref[:, cols] = (v * r * akg).astype(BF16)

    ikw = ikw_ref[...]
    lane = lax.broadcasted_iota(I32, ikw.shape, 1)
    zero = jnp.zeros_like(ikw)
    kab_ref[:, 0:LANES] = jnp.where(lane < IDX_DIM, ikw, zero)
    kab_ref[:, LANES:2 * LANES] = jnp.where(lane >= IDX_DIM, pltpu.roll(ikw.astype(F32), IDX_DIM, 1).astype(BF16), zero)
    w = pltpu.roll(ikw.astype(F32), LANES - IDX_DIM, 1)
    wsc_ref[...] = jnp.where(lane < IDX_HEADS, w * IDX_SCALE, 0.0)

    cq_lat = bcq_ref[...].astype(F32)
    r = _rms_scale(jnp.sum(cq_lat * cq_lat, axis=-1, keepdims=True), B_Q_RANK)
    qlat = (cq_lat * r * bqlg_ref[...]).astype(BF16)
    bqg = bqg_ref[...] * B_SCALE
    for h in range(B_HEADS):
        cols = slice(h * B_QK_PAD, (h + 1) * B_QK_PAD)
        qh = jnp.dot(qlat, wuq_ref[:, cols], preferred_element_type=F32)
        r = _rms_scale(jnp.sum(qh * qh, axis=-1, keepdims=True), B_QK_DIM)
        qn = qh * r * bqg
        qb_ref[:, h * B_QK_PAD:h * B_QK_PAD + LANES] = qn[:, :LANES].astype(BF16)
        qb_ref[:, h * B_QK_PAD + LANES:(h + 1) * B_QK_PAD] = _rope128(qn[:, LANES:], cos, sin).astype(BF16)

    ckv = bckv_ref[...].astype(F32)
    r = _rms_scale(jnp.sum(ckv * ckv, axis=-1, keepdims=True), B_KV_RANK)
    kvlat = (ckv * r * bkvlg_ref[...]).astype(BF16)
    kpe = bkpe_ref[...].astype(F32)
    ss_pe = jnp.sum(kpe * kpe, axis=-1, keepdims=True)
    bkg = bkg_ref[...]
    kpe_rot = _rope128(kpe * bkg[:, LANES:], cos, sin)
    for h in range(B_HEADS):
        cols = slice(h * (B_NOPE + B_V), (h + 1) * (B_NOPE + B_V))
        kvh = jnp.dot(kvlat, wukv_ref[:, cols], preferred_element_type=F32)
        kn = kvh[:, :B_NOPE]
        r = _rms_scale(jnp.sum(kn * kn, axis=-1, keepdims=True) + ss_pe, B_QK_DIM)
        kb_ref[:, h * B_QK_PAD:h * B_QK_PAD + LANES] = (kn * r * bkg[:, :LANES]).astype(BF16)
        kb_ref[:, h * B_QK_PAD + LANES:(h + 1) * B_QK_PAD] = (kpe_rot * r).astype(BF16)
        vb_ref[:, h * B_V:(h + 1) * B_V] = kvh[:, B_NOPE:].astype(BF16)

    cqg = cqg_ref[...] * C_SCALE
    for h in range(C_HEADS):
        cols = slice(h * C_HEAD_DIM, (h + 1) * C_HEAD_DIM)
        v = cq_ref[:, cols].astype(F32)
        r = _rms_scale(jnp.sum(v * v, axis=-1, keepdims=True), C_HEAD_DIM)
        cqn_ref[:, cols] = (v * r * cqg).astype(BF16)


def _seg_spec(tm, name):
    units = SEG_UNITS[name]
    blk = SEG_START[name] // units
    return pl.BlockSpec((tm, units * LANES), lambda i: (i, blk))


def _full_spec(shape):
    return pl.BlockSpec(shape, lambda i: (0,) * len(shape))


def _prep(proj, cos128, sin128, aqg, akg, bqlg, bkvlg, bqg, bkg, cqg, wuq, wukv, *, tm):
    s = proj.shape[0]
    row = lambda w: pl.BlockSpec((tm, w), lambda i: (i, 0))
    outs = [
        (A_WIDTH, BF16),
        (A_KV_WIDTH, BF16),
        (2 * LANES, BF16),
        (LANES, F32),
        (B_HEADS * B_QK_PAD, BF16),
        (B_HEADS * B_QK_PAD, BF16),
        (B_WIDTH, BF16),
        (C_WIDTH, BF16),
    ]
    small = [aqg, akg, bqlg, bkvlg, bqg, bkg, cqg, wuq, wukv]
    return pl.pallas_call(
        _prep_kernel,
        grid=(s // tm,),
        in_specs=[_seg_spec(tm, n) for n in ("aq", "ak", "ikw", "bcq", "bckv", "bkpe", "cq")]
        + [row(LANES), row(LANES)] + [_full_spec(a.shape) for a in small],
        out_specs=[row(w) for w, _ in outs],
        out_shape=[jax.ShapeDtypeStruct((s, w), dt) for w, dt in outs],
        compiler_params=_cparams("parallel"),
        name="prep",
    )(*([proj] * 7), cos128, sin128, *small)


def _softmax_step(s, v, m_ref, l_ref, acc_ref):
    m_prev = m_ref[...]
    m_new = jnp.maximum(m_prev, jnp.max(s, axis=-1, keepdims=True))
    alpha = jnp.exp(m_prev - m_new)
    p = jnp.exp(s - m_new)
    l_ref[...] = alpha * l_ref[...] + jnp.sum(p, axis=-1, keepdims=True)
    acc_ref[...] = alpha * acc_ref[...] + jnp.dot(p.astype(BF16), v, preferred_element_type=F32)
    m_ref[...] = m_new


def _causal_ok(i, j, tq, tk):
    row = i * tq + lax.broadcasted_iota(I32, (tq, tk), 0)
    col = j * tk + lax.broadcasted_iota(I32, (tq, tk), 1)
    return col <= row


def _attn_b_kernel(q_ref, k_ref, v_ref, o_ref, m_ref, l_ref, acc_ref, *, tq, tk):
    i = pl.program_id(1)
    m_ref[...] = jnp.full(m_ref.shape, MASK_VALUE, F32)
    l_ref[...] = jnp.zeros(l_ref.shape, F32)
    acc_ref[...] = jnp.zeros(acc_ref.shape, F32)
    q = q_ref[...]
    j_diag = (i * tq) // tk

    def tile(j, masked):
        rows = pl.ds(pl.multiple_of(j * tk, tk), tk)
        s = _dot_t(q, k_ref[rows, :])
        if masked:
            s = jnp.where(_causal_ok(i, j, tq, tk), s, MASK_VALUE)
        _softmax_step(s, v_ref[rows, :], m_ref, l_ref, acc_ref)

    def body(j, c):
        tile(j, False)
        return c

    lax.fori_loop(0, j_diag, body, 0)
    tile(j_diag, True)
    o_ref[...] = (acc_ref[...] / l_ref[...]).astype(o_ref.dtype)


def _attn_b(qb, kb, vb, *, tq, tk):
    s = qb.shape[0]
    assert tk % tq == 0
    return pl.pallas_call(
        functools.partial(_attn_b_kernel, tq=tq, tk=tk),
        grid=(B_HEADS, s // tq),
        in_specs=[
            pl.BlockSpec((tq, B_QK_PAD), lambda h, i: (i, h)),
            pl.BlockSpec((s, B_QK_PAD), lambda h, i: (0, h)),
            pl.BlockSpec((s, B_V), lambda h, i: (0, h)),
        ],
        out_specs=pl.BlockSpec((tq, B_V), lambda h, i: (i, h)),
        out_shape=jax.ShapeDtypeStruct((s, B_WIDTH), BF16),
        scratch_shapes=[pltpu.VMEM((tq, 1), F32), pltpu.VMEM((tq, 1), F32), pltpu.VMEM((tq, B_V), F32)],
        compiler_params=_cparams("parallel", "arbitrary"),
        name="attn_b",
    )(qb, kb, vb)


def _order_key(x):
    b = lax.bitcast_convert_type(x + 0.0, I32)
    return b ^ (lax.shift_right_arithmetic(b, 31) & 0x7FFFFFFF)


def _lane_chunk_count(pred_fn, tile, cnt):
    for c in range(tile.shape[1] // LANES):
        cnt = cnt + jnp.where(pred_fn(tile[:, c * LANES:(c + 1) * LANES], c), 1, 0)
    return cnt


def _attn_a_kernel(qmin_ref, kmax_ref,
                   iq_ref, wsc_ref, kab_ref, aqn_ref, akn_ref, av_ref, pq_ref, pk_ref, lut_ref, cfar_ref,
                   o_ref,
                   key_ref, m_ref, l_ref, acc_ref, *, tq, tk, topk):
    i = pl.program_id(0)
    j_diag = (i * tq) // tk
    n_tiles = j_diag + 1
    int_min = jnp.int32(-2 ** 31)

    wsc = wsc_ref[...]

    def score_tile(j, masked):
        rows = pl.ds(pl.multiple_of(j * tk, tk), tk)
        k_lo = kab_ref[rows, 0:LANES]
        k_hi = kab_ref[rows, LANES:2 * LANES]
        acc = jnp.zeros((tq, tk), F32)
        for p in range(IDX_HEADS // 2):
            lhs = iq_ref[:, p * LANES:(p + 1) * LANES]
            acc = acc + jnp.maximum(_dot_t(lhs, k_lo), 0.0) * wsc[:, 2 * p:2 * p + 1]
            acc = acc + jnp.maximum(_dot_t(lhs, k_hi), 0.0) * wsc[:, 2 * p + 1:2 * p + 2]
        if masked:
            acc = jnp.where(_causal_ok(i, j, tq, tk), acc, -jnp.inf)
        key_ref[j] = _order_key(acc)

    def score_body(j, c):
        score_tile(j, False)
        return c

    lax.fori_loop(0, j_diag, score_body, 0)
    score_tile(j_diag, True)

    def count_rows(pred_fn):
        def body(j, cnt):
            return _lane_chunk_count(lambda x, c: pred_fn(x, j, c), key_ref[j], cnt)
        cnt = lax.fori_loop(0, n_tiles, body, jnp.zeros((tq, LANES), I32))
        return jnp.sum(cnt, axis=-1, keepdims=True)

    def bit_body(it, t):
        cand = t + lax.shift_left(jnp.int32(1), 31 - it)
        cand_b = jnp.broadcast_to(cand, (tq, LANES))
        cnt = count_rows(lambda x, j, c: x >= cand_b)
        return jnp.where(cnt >= topk, cand, t)

    thr = lax.fori_loop(0, 32, bit_body, jnp.full((tq, 1), int_min, I32))
    thr_b = jnp.broadcast_to(thr, (tq, LANES))
    cnt_ge = count_rows(lambda x, j, c: x >= thr_b)

    @pl.when(jnp.max(cnt_ge) > topk)
    def _():
        cnt_gt = count_rows(lambda x, j, c: x > thr_b)
        need = topk - cnt_gt
        lane = lax.broadcasted_iota(I32, (tq, LANES), 1)

        def col_body(it, cut):
            cand = cut + lax.shift_left(jnp.int32(1), 30 - it)
            cand_b = jnp.broadcast_to(cand, (tq, LANES))
            cnt = count_rows(lambda x, j, c: (x == thr_b) & (j * tk + c * LANES + lane < cand_b))
            return jnp.where(cnt < need, cand, cut)

        cut = lax.fori_loop(0, 31, col_body, jnp.zeros((tq, 1), I32))
        cut_b = jnp.broadcast_to(cut, (tq, LANES))

        def demote_body(j, c):
            for ch in range(tk // LANES):
                cols = slice(ch * LANES, (ch + 1) * LANES)
                x = key_ref[j, :, cols]
                drop = (x == thr_b) & (j * tk + ch * LANES + lane > cut_b) & (thr_b > int_min)
                key_ref[j, :, cols] = jnp.where(drop, x - 1, x)
            return c

        lax.fori_loop(0, n_tiles, demote_body, 0)

    m_ref[...] = jnp.full(m_ref.shape, MASK_VALUE, F32)
    l_ref[...] = jnp.zeros(l_ref.shape, F32)
    acc_ref[...] = jnp.zeros(acc_ref.shape, F32)
    thr_t = jnp.broadcast_to(thr, (tq, tk))
    pq = pq_ref[...]

    def attend(j, masked, near):
        rows = pl.ds(pl.multiple_of(j * tk, tk), tk)
        sel = key_ref[j] >= thr_t
        if masked:
            sel = sel & _causal_ok(i, j, tq, tk)
        mask_bias = jnp.where(sel, 0.0, MASK_VALUE)
        if near:
            dist = jnp.clip(pq - pk_ref[j], 0, LANES - 1)
        for h in range(A_HEADS):
            g = h // A_REP
            kv_cols = slice(g * A_HEAD_DIM, (g + 1) * A_HEAD_DIM)
            q_cols = slice(h * A_HEAD_DIM, (h + 1) * A_HEAD_DIM)
            s = _dot_t(aqn_ref[:, q_cols], akn_ref[rows, kv_cols]) + mask_bias
            if near:
                table = jnp.broadcast_to(lut_ref[h:h + 1, :], (tq, LANES))
                bias = jnp.concatenate(
                    [jnp.take_along_axis(table, dist[:, c * LANES:(c + 1) * LANES], axis=1)
                     for c in range(tk // LANES)], axis=1)
                s = s + bias
            else:
                s = s + cfar_ref[h]
            _softmax_step(s, av_ref[rows, kv_cols], m_ref.at[h], l_ref.at[h], acc_ref.at[:, q_cols])

    def attend_dyn(j, masked):
        far = qmin_ref[i] - kmax_ref[j] >= T5_FAR

        @pl.when(far)
        def _():
            attend(j, masked, False)

        @pl.when(jnp.logical_not(far))
        def _():
            attend(j, masked, True)

    def attend_body(j, c):
        attend_dyn(j, False)
        return c

    lax.fori_loop(0, j_diag, attend_body, 0)
    attend_dyn(j_diag, True)

    for h in range(A_HEADS):
        q_cols = slice(h * A_HEAD_DIM, (h + 1) * A_HEAD_DIM)
        o_ref[:, q_cols] = (acc_ref[:, q_cols] / l_ref[h]).astype(o_ref.dtype)


def _attn_a(proj, wsc, kab, aqn, akn, pos_col, pos_tiles, lut_t, c_far, qmin, kmax, *, tq, tk, topk):
    s = proj.shape[0]
    assert tk % tq == 0
    iq_blk = SEG_START["iq"] // SEG_UNITS["iq"]
    av_blk = SEG_START["av"] // SEG_UNITS["av"]
    grid_spec = pltpu.PrefetchScalarGridSpec(
        num_scalar_prefetch=2,
        grid=(s // tq,),
        in_specs=[
            pl.BlockSpec((tq, IDX_HEADS * IDX_DIM), lambda i, *_: (i, iq_blk)),
            pl.BlockSpec((tq, LANES), lambda i, *_: (i, 0)),
            pl.BlockSpec((s, 2 * LANES), lambda i, *_: (0, 0)),
            pl.BlockSpec((tq, A_WIDTH), lambda i, *_: (i, 0)),
            pl.BlockSpec((s, A_KV_WIDTH), lambda i, *_: (0, 0)),
            pl.BlockSpec((s, A_KV_WIDTH), lambda i, *_: (0, av_blk)),
            pl.BlockSpec((tq, 1), lambda i, *_: (i, 0)),
            pl.BlockSpec((s // tk, 1, tk), lambda i, *_: (0, 0, 0)),
            pl.BlockSpec((A_HEADS, LANES), lambda i, *_: (0, 0)),
            pl.BlockSpec(memory_space=pltpu.SMEM),
        ],
        out_specs=pl.BlockSpec((tq, A_WIDTH), lambda i, *_: (i, 0)),
        scratch_shapes=[
            pltpu.VMEM((s // tk, tq, tk), I32),
            pltpu.VMEM((A_HEADS, tq, 1), F32),
            pltpu.VMEM((A_HEADS, tq, 1), F32),
            pltpu.VMEM((tq, A_WIDTH), F32),
        ],
    )
    return pl.pallas_call(
        functools.partial(_attn_a_kernel, tq=tq, tk=tk, topk=topk),
        grid_spec=grid_spec,
        out_shape=jax.ShapeDtypeStruct((s, A_WIDTH), BF16),
        compiler_params=_cparams("arbitrary"),
        name="attn_a",
    )(qmin, kmax, proj, wsc, kab, aqn, akn, proj, pos_col, pos_tiles, lut_t, c_far)


def _attn_c_kernel(q_ref, kv_ref, o_ref):
    for h in range(C_HEADS):
        cols = slice(h * C_HEAD_DIM, (h + 1) * C_HEAD_DIM)
        s = _dot_t(q_ref[:, cols], kv_ref[:, cols])
        p = jnp.exp(s - jnp.max(s, axis=-1, keepdims=True))
        o = jnp.dot(p.astype(BF16), kv_ref[:, C_WIDTH + h * C_HEAD_DIM:C_WIDTH + (h + 1) * C_HEAD_DIM],
                    preferred_element_type=F32)
        o_ref[:, cols] = (o / jnp.sum(p, axis=-1, keepdims=True)).astype(o_ref.dtype)


def _attn_c(cqn, mem_kv, *, tm):
    s = cqn.shape[0]
    return pl.pallas_call(
        _attn_c_kernel,
        grid=(s // tm,),
        in_specs=[pl.BlockSpec((tm, C_WIDTH), lambda i: (i, 0)), _full_spec(mem_kv.shape)],
        out_specs=pl.BlockSpec((tm, C_WIDTH), lambda i: (i, 0)),
        out_shape=jax.ShapeDtypeStruct((s, C_WIDTH), BF16),
        compiler_params=_cparams("parallel"),
        name="attn_c",
    )(cqn, mem_kv)


def _merge_out_kernel(x_ref, oa_ref, ob_ref, oc_ref, az_ref, bz_ref, cz_ref, ga_ref, gb_ref, gc_ref,
                      wbr_ref, wout_ref, o_ref):
    merged = None
    for n, (o_r, z_r, g_r) in enumerate(((oa_ref, az_ref, ga_ref), (ob_ref, bz_ref, gb_ref),
                                         (oc_ref, cz_ref, gc_ref))):
        z = z_r[...].astype(F32)
        u = (o_r[...].astype(F32) * (z * jax.nn.sigmoid(z))).astype(BF16)
        y = jnp.dot(u, wbr_ref[n], preferred_element_type=F32)
        t = jax.nn.sigmoid(g_r[...].astype(F32)) * y
        merged = t if merged is None else merged + t
    o_ref[...] = x_ref[...] + jnp.dot(merged.astype(BF16), wout_ref[...], preferred_element_type=F32)


def _merge_out(x, o_a, o_b, o_c, proj, w_branch, w_out, *, tm):
    s = x.shape[0]
    row = lambda w: pl.BlockSpec((tm, w), lambda i: (i, 0))
    gate_blk = SEG_START["gates"] * LANES // D_MODEL
    gate = lambda n: pl.BlockSpec((tm, D_MODEL), lambda i: (i, gate_blk + n))
    single = pl.Buffered(1)
    return pl.pallas_call(
        _merge_out_kernel,
        grid=(s // tm,),
        in_specs=[row(D_MODEL), row(BRANCH_WIDTH), row(BRANCH_WIDTH), row(BRANCH_WIDTH),
                  _seg_spec(tm, "az"), _seg_spec(tm, "bz"), _seg_spec(tm, "cz"),
                  gate(0), gate(1), gate(2),
                  pl.BlockSpec(w_branch.shape, lambda i: (0, 0, 0), pipeline_mode=single),
                  pl.BlockSpec(w_out.shape, lambda i: (0, 0), pipeline_mode=single)],
        out_specs=row(D_MODEL),
        out_shape=jax.ShapeDtypeStruct((s, D_MODEL), F32),
        compiler_params=_cparams("parallel"),
        name="merge_out",
    )(x, o_a, o_b, o_c, proj, proj, proj, proj, proj, proj, w_branch, w_out)


def _regroup_w_in(w_in):
    (aq, ak, av, az, iq, ik, iw, bcq, bckv, bkpe, bz, cq, cz, gates) = jnp.split(w_in, IN_OFFSETS, axis=1)
    d = w_in.shape[0]
    ikw = jnp.concatenate([ik, iw, jnp.zeros((d, LANES - IDX_DIM - IDX_HEADS), w_in.dtype)], axis=1)
    bkpe_p = jnp.concatenate([bkpe, jnp.zeros((d, LANES - B_ROPE), w_in.dtype)], axis=1)
    segs = dict(gates=gates, aq=aq, az=az, iq=iq, bz=bz, cq=cq, cz=cz, bcq=bcq, ak=ak, av=av,
                bckv=bckv, ikw=ikw, bkpe=bkpe_p)
    return jnp.concatenate([segs[n] for n in SEG_UNITS], axis=1).astype(BF16)


def _pad_heads(w, n_heads, width, pad_to):
    r = w.shape[0]
    w = w.reshape(r, n_heads, width)
    w = jnp.pad(w, ((0, 0), (0, 0), (0, pad_to - width)))
    return w.reshape(r, n_heads * pad_to)


def kernel(x, mem, positions, rel_bias, norm_g, mem_norm_g, w_in, a_q_norm_g, a_k_norm_g, b_q_lat_norm_g,
           b_kv_lat_norm_g, w_b_uq, w_b_ukv, b_q_norm_g, b_k_norm_g, w_mem_kv, c_q_norm_g, c_k_norm_g,
           w_branch, w_out):
    bsz, seq, d = x.shape
    assert d == D_MODEL and norm_g.shape[0] == 1
    topk = min(TOPK_MAX, seq // 4)
    tq_a, tk_a = 256, 512
    tq_b, tk_b = 512, 512
    tm_in = min(1024, seq)

    inv_freq = 1.0 / (ROPE_THETA ** (jnp.arange(0, B_ROPE, 2, dtype=F32) / B_ROPE))
    zeros_half = jnp.zeros((B_ROPE,), F32)
    gq_pad = jnp.concatenate([b_q_norm_g[0], zeros_half])[None, :]
    gk_pad = jnp.concatenate([b_k_norm_g[0], zeros_half])[None, :]
    lut_t = rel_bias[jnp.asarray(T5_TABLE)].T.astype(F32)
    c_far = rel_bias[REL_BUCKETS - 1].astype(F32)

    w_cat = _regroup_w_in(w_in[0])
    wuq = _pad_heads(w_b_uq[0], B_HEADS, B_QK_DIM, B_QK_PAD).astype(BF16)
    wukv = w_b_ukv[0].astype(BF16)
    wbr = w_branch[0].astype(BF16)
    wout = w_out[0].astype(BF16)

    outs = []
    for b in range(bsz):
        pos = positions[b]
        ang = pos.astype(F32)[:, None] * inv_freq
        cos, sin = jnp.cos(ang), jnp.sin(ang)
        zpad = jnp.zeros((seq, LANES - B_ROPE), F32)
        cos128 = jnp.concatenate([cos, cos, zpad], axis=1)
        sin128 = jnp.concatenate([-sin, sin, zpad], axis=1)

        mem_kv = _mem_kv(mem[b], mem_norm_g[0][None, :], w_mem_kv[0], c_k_norm_g[0][None, :])
        proj = _in_proj(x[b], norm_g[0][None, :], w_cat, tm=tm_in, tn=512)
        aqn, akn, kab, wsc, qb, kb, vb, cqn = _prep(
            proj, cos128, sin128, a_q_norm_g[0][None, :], a_k_norm_g[0][None, :],
            b_q_lat_norm_g[0][None, :], b_kv_lat_norm_g[0][None, :], gq_pad, gk_pad,
            c_q_norm_g[0][None, :], wuq, wukv, tm=256)
        o_b = _attn_b(qb, kb, vb, tq=tq_b, tk=tk_b)
        qmin = jnp.min(pos.reshape(seq // tq_a, tq_a), axis=1)
        kmax = jnp.max(pos.reshape(seq // tk_a, tk_a), axis=1)
        o_a = _attn_a(proj, wsc, kab, aqn, akn, pos[:, None], pos.reshape(seq // tk_a, 1, tk_a),
                      lut_t, c_far, qmin, kmax, tq=tq_a, tk=tk_a, topk=topk)
        o_c = _attn_c(cqn, mem_kv, tm=512)
        outs.append(_merge_out(x[b], o_a, o_b, o_c, proj, wbr, wout, tm=256))
    return jnp.stack(outs, axis=0)
```

```python
import functools
import math

import numpy as np
import jax
import jax.numpy as jnp
from jax import lax
from jax.experimental import pallas as pl
from jax.experimental.pallas import tpu as pltpu

F32 = jnp.float32
BF16 = jnp.bfloat16
I32 = jnp.int32

D_MODEL = 2048
BRANCH_WIDTH = 1024
N_BRANCH = 3
A_HEADS = 8
A_KV_HEADS = 2
A_HEAD_DIM = 128
A_WIDTH = A_HEADS * A_HEAD_DIM
A_KV_WIDTH = A_KV_HEADS * A_HEAD_DIM
A_REP = A_HEADS // A_KV_HEADS
IDX_HEADS = 16
IDX_DIM = 64
TOPK_MAX = 256
IDX_SCALE = (IDX_DIM ** -0.5) * (IDX_HEADS ** -0.5)
A_SCALE = A_HEAD_DIM ** -0.5
B_HEADS = 8
B_Q_RANK = 512
B_KV_RANK = 256
B_NOPE = 128
B_ROPE = 64
B_QK_DIM = B_NOPE + B_ROPE
B_V = 128
B_WIDTH = B_HEADS * B_V
B_SCALE = B_QK_DIM ** -0.5
ROPE_THETA = 10000.0
C_HEADS = 4
C_HEAD_DIM = 256
C_WIDTH = C_HEADS * C_HEAD_DIM
C_SCALE = C_HEAD_DIM ** -0.5
REL_BUCKETS = 32
REL_MAX_DIST = 128
EPS = 1e-6

IN_SIZES = (A_WIDTH, A_KV_WIDTH, A_KV_WIDTH, A_WIDTH, IDX_HEADS * IDX_DIM, IDX_DIM, IDX_HEADS,
            B_Q_RANK, B_KV_RANK, B_ROPE, B_WIDTH, C_WIDTH, C_WIDTH, N_BRANCH * D_MODEL)
IN_OFFSETS = [int(o) for o in np.cumsum(IN_SIZES)[:-1]]

LANES = 128
SUBLANES = 8
B_QK_PAD = 2 * LANES
VMEM_LIMIT_BYTES = 56 * 1024 * 1024
MASK_VALUE = -1e30
LOG2E = math.log2(math.e)
KV_TILE = 512
B_HEADS_PER_STEP = 2

SEG_UNITS = dict(gates=48, aq=8, az=8, iq=8, bz=8, cq=8, cz=8, bcq=4, ak=2, av=2, bckv=2, ikw=1, bkpe=1)
SEG_START = {}
_u = 0
for _name, _w in SEG_UNITS.items():
    SEG_START[_name] = _u
    _u += _w
PROJ_UNITS = _u
PROJ_WIDTH = PROJ_UNITS * LANES


def _t5_bucket_table():
    d = np.arange(LANES)
    max_exact = REL_BUCKETS // 2
    nf = np.maximum(d, 1).astype(np.float64)
    large = max_exact + (np.log(nf / max_exact) / math.log(REL_MAX_DIST / max_exact)
                         * (REL_BUCKETS - max_exact)).astype(np.int64)
    large = np.minimum(large, REL_BUCKETS - 1)
    table = np.where(d < max_exact, d, large).astype(np.int32)
    far = int(np.min(np.nonzero(table == REL_BUCKETS - 1)[0]))
    assert np.all(table[far:] == REL_BUCKETS - 1)
    return table, far


T5_TABLE, T5_FAR = _t5_bucket_table()


def _cparams(*sem):
    return pltpu.CompilerParams(dimension_semantics=sem, vmem_limit_bytes=VMEM_LIMIT_BYTES)


def _dot_t(a, b):
    return lax.dot_general(a, b, (((1,), (1,)), ((), ())), preferred_element_type=F32)


def _rms_scale(ss, n):
    return lax.rsqrt(ss * (1.0 / n) + EPS)


def _mem_kv_kernel(mem_ref, g_ref, w_ref, ckn_ref, o_ref, h_ref):
    j = pl.program_id(0)

    @pl.when(j == 0)
    def _():
        m = mem_ref[...]
        r = _rms_scale(jnp.sum(m * m, axis=-1, keepdims=True), D_MODEL)
        h_ref[...] = (m * r * g_ref[...]).astype(BF16)

    y = jnp.dot(h_ref[...], w_ref[...].astype(BF16), preferred_element_type=F32)

    @pl.when(j < C_HEADS)
    def _():
        r = _rms_scale(jnp.sum(y * y, axis=-1, keepdims=True), C_HEAD_DIM)
        o_ref[...] = (y * r * ckn_ref[...]).astype(BF16)

    @pl.when(j >= C_HEADS)
    def _():
        o_ref[...] = y.astype(BF16)


def _mem_kv(mem, mem_g, w_mem_kv, ck_g):
    n_mem = mem.shape[0]
    return pl.pallas_call(
        _mem_kv_kernel,
        grid=(2 * C_HEADS,),
        in_specs=[
            pl.BlockSpec((n_mem, D_MODEL), lambda j: (0, 0)),
            pl.BlockSpec((1, D_MODEL), lambda j: (0, 0)),
            pl.BlockSpec((D_MODEL, C_HEAD_DIM), lambda j: (0, j)),
            pl.BlockSpec((1, C_HEAD_DIM), lambda j: (0, 0)),
        ],
        out_specs=pl.BlockSpec((n_mem, C_HEAD_DIM), lambda j: (0, j)),
        out_shape=jax.ShapeDtypeStruct((n_mem, 2 * C_WIDTH), BF16),
        scratch_shapes=[pltpu.VMEM((n_mem, D_MODEL), BF16)],
        compiler_params=_cparams("arbitrary"),
        name="mem_kv",
    )(mem, mem_g, w_mem_kv, ck_g)


def _in_proj_kernel(x_ref, g_ref, w_ref, o_ref, h_ref, *, row_chunk):
    @pl.when(pl.program_id(1) == 0)
    def _():
        g = g_ref[...]
        for c in range(x_ref.shape[0] // row_chunk):
            rows = pl.ds(c * row_chunk, row_chunk)
            x = x_ref[rows, :]
            r = _rms_scale(jnp.sum(x * x, axis=-1, keepdims=True), D_MODEL)
            h_ref[rows, :] = (x * r * g).astype(BF16)

    o_ref[...] = jnp.dot(h_ref[...], w_ref[...], preferred_element_type=F32).astype(o_ref.dtype)


def _in_proj(x, norm_g, w_cat, *, tm, tn):
    s = x.shape[0]
    return pl.pallas_call(
        functools.partial(_in_proj_kernel, row_chunk=min(tm, 256)),
        grid=(s // tm, PROJ_WIDTH // tn),
        in_specs=[
            pl.BlockSpec((tm, D_MODEL), lambda i, j: (i, 0)),
            pl.BlockSpec((1, D_MODEL), lambda i, j: (0, 0)),
            pl.BlockSpec((D_MODEL, tn), lambda i, j: (0, j)),
        ],
        out_specs=pl.BlockSpec((tm, tn), lambda i, j: (i, j)),
        out_shape=jax.ShapeDtypeStruct((s, PROJ_WIDTH), BF16),
        scratch_shapes=[pltpu.VMEM((tm, D_MODEL), BF16)],
        compiler_params=_cparams("parallel", "arbitrary"),
        name="in_proj",
    )(x, norm_g, w_cat)


def _rope128(x, cos, sin):
    half = B_ROPE // 2
    lane = lax.broadcasted_iota(I32, x.shape, 1)
    partner = jnp.where(lane < half, pltpu.roll(x, LANES - half, 1), pltpu.roll(x, half, 1))
    return x * cos + partner * sin


def _prep_kernel(aq_ref, ak_ref, av_ref, ikw_ref, bcq_ref, bckv_ref, bkpe_ref, cq_ref,
                 cos_ref, sin_ref, aqg_ref, akg_ref, bqlg_ref, bkvlg_ref, bqg_ref, bkg_ref, cqg_ref,
                 wuq_ref, wuk_ref, wuvt_ref,
                 aqn_ref, akn_ref, kab_ref, wt_ref, avt_ref, qb_ref, kb_ref, vbt_ref, cqn_ref):
    cos = cos_ref[...]
    sin = sin_ref[...]
    eye = jnp.where(lax.broadcasted_iota(I32, (LANES, LANES), 0) == lax.broadcasted_iota(I32, (LANES, LANES), 1),
                    1.0, 0.0).astype(BF16)

    aqg = aqg_ref[...] * (A_SCALE * LOG2E)
    for h in range(A_HEADS):
        cols = slice(h * A_HEAD_DIM, (h + 1) * A_HEAD_DIM)
        v = aq_ref[:, cols].astype(F32)
        r = _rms_scale(jnp.sum(v * v, axis=-1, keepdims=True), A_HEAD_DIM)
        aqn_ref[:, cols] = (v * r * aqg).astype(BF16)
    akg = akg_ref[...]
    for h in range(A_KV_HEADS):
        cols = slice(h * A_HEAD_DIM, (h + 1) * A_HEAD_DIM)
        v = ak_ref[:, cols].astype(F32)
        r = _rms_scale(jnp.sum(v * v, axis=-1, keepdims=True), A_HEAD_DIM)
        akn_ref[:, cols] = (v * r * akg).astype(BF16)
        avt_ref[h, 0] = _dot_t(eye, av_ref[:, cols]).astype(BF16)

    ikw = ikw_ref[...]
    lane = lax.broadcasted_iota(I32, ikw.shape, 1)
    zero = jnp.zeros_like(ikw)
    kab_ref[:, 0:LANES] = jnp.where(lane < IDX_DIM, ikw, zero)
    kab_ref[:, LANES:2 * LANES] = jnp.where(lane >= IDX_DIM, pltpu.roll(ikw.astype(F32), IDX_DIM, 1).astype(BF16), zero)
    wt_ref[...] = _dot_t(eye, ikw) * IDX_SCALE

    cq_lat = bcq_ref[...].astype(F32)
    r = _rms_scale(jnp.sum(cq_lat * cq_lat, axis=-1, keepdims=True), B_Q_RANK)
    qlat = (cq_lat * r * bqlg_ref[...]).astype(BF16)
    bqg = bqg_ref[...] * (B_SCALE * LOG2E)
    for h in range(B_HEADS):
        cols = slice(h * B_QK_PAD, (h + 1) * B_QK_PAD)
        qh = jnp.dot(qlat, wuq_ref[:, cols], preferred_element_type=F32)
        r = _rms_scale(jnp.sum(qh * qh, axis=-1, keepdims=True), B_QK_DIM)
        qn = qh * r * bqg
        qb_ref[:, h * B_QK_PAD:h * B_QK_PAD + LANES] = qn[:, :LANES].astype(BF16)
        qb_ref[:, h * B_QK_PAD + LANES:(h + 1) * B_QK_PAD] = _rope128(qn[:, LANES:], cos, sin).astype(BF16)

    ckv = bckv_ref[...].astype(F32)
    r = _rms_scale(jnp.sum(ckv * ckv, axis=-1, keepdims=True), B_KV_RANK)
    kvlat = (ckv * r * bkvlg_ref[...]).astype(BF16)
    kpe = bkpe_ref[...].astype(F32)
    ss_pe = jnp.sum(kpe * kpe, axis=-1, keepdims=True)
    bkg = bkg_ref[...]
    kpe_rot = _rope128(kpe * bkg[:, LANES:], cos, sin)
    for h in range(B_HEADS):
        kn = jnp.dot(kvlat, wuk_ref[:, h * B_NOPE:(h + 1) * B_NOPE], preferred_element_type=F32)
        r = _rms_scale(jnp.sum(kn * kn, axis=-1, keepdims=True) + ss_pe, B_QK_DIM)
        kb_ref[:, h * B_QK_PAD:h * B_QK_PAD + LANES] = (kn * r * bkg[:, :LANES]).astype(BF16)
        kb_ref[:, h * B_QK_PAD + LANES:(h + 1) * B_QK_PAD] = (kpe_rot * r).astype(BF16)
        vbt_ref[h, 0] = _dot_t(wuvt_ref[h], kvlat).astype(BF16)

    cqg = cqg_ref[...] * C_SCALE
    for h in range(C_HEADS):
        cols = slice(h * C_HEAD_DIM, (h + 1) * C_HEAD_DIM)
        v = cq_ref[:, cols].astype(F32)
        r = _rms_scale(jnp.sum(v * v, axis=-1, keepdims=True), C_HEAD_DIM)
        cqn_ref[:, cols] = (v * r * cqg).astype(BF16)


def _seg_spec(tm, name):
    units = SEG_UNITS[name]
    blk = SEG_START[name] // units
    return pl.BlockSpec((tm, units * LANES), lambda i: (i, blk))


def _full_spec(shape):
    return pl.BlockSpec(shape, lambda i: (0,) * len(shape))


def _prep(proj, cos128, sin128, aqg, akg, bqlg, bkvlg, bqg, bkg, cqg, wuq, wuk, wuvt):
    s = proj.shape[0]
    tm = KV_TILE
    nt = s // tm
    row = lambda w: pl.BlockSpec((tm, w), lambda i: (i, 0))
    tiles_t = lambda n: pl.BlockSpec((n, 1, LANES, tm), lambda i: (0, i, 0, 0))
    small = [aqg, akg, bqlg, bkvlg, bqg, bkg, cqg, wuq, wuk, wuvt]
    out_specs = [row(A_WIDTH), row(A_KV_WIDTH), row(2 * LANES),
                 pl.BlockSpec((LANES, tm), lambda i: (0, i)), tiles_t(A_KV_HEADS),
                 row(B_HEADS * B_QK_PAD), row(B_HEADS * B_QK_PAD), tiles_t(B_HEADS), row(C_WIDTH)]
    out_shape = [
        jax.ShapeDtypeStruct((s, A_WIDTH), BF16),
        jax.ShapeDtypeStruct((s, A_KV_WIDTH), BF16),
        jax.ShapeDtypeStruct((s, 2 * LANES), BF16),
        jax.ShapeDtypeStruct((LANES, s), F32),
        jax.ShapeDtypeStruct((A_KV_HEADS, nt, LANES, tm), BF16),
        jax.ShapeDtypeStruct((s, B_HEADS * B_QK_PAD), BF16),
        jax.ShapeDtypeStruct((s, B_HEADS * B_QK_PAD), BF16),
        jax.ShapeDtypeStruct((B_HEADS, nt, B_V, tm), BF16),
        jax.ShapeDtypeStruct((s, C_WIDTH), BF16),
    ]
    return pl.pallas_call(
        _prep_kernel,
        grid=(nt,),
        in_specs=[_seg_spec(tm, n) for n in ("aq", "ak", "av", "ikw", "bcq", "bckv", "bkpe", "cq")]
        + [row(LANES), row(LANES)] + [_full_spec(a.shape) for a in small],
        out_specs=out_specs,
        out_shape=out_shape,
        compiler_params=_cparams("parallel"),
        name="prep",
    )(*([proj] * 8), cos128, sin128, *small)


def _reduce_keys(x, op):
    tk, tq = x.shape
    n = tk // SUBLANES
    assert n & (n - 1) == 0
    t = x.reshape(n, SUBLANES, tq)
    while n > 1:
        n //= 2
        t = op(t[:n], t[n:2 * n])
    red = jnp.max if op is jnp.maximum else jnp.sum
    return red(t[0], axis=0, keepdims=True)


def _softmax_step_t(st, vt, m_ref, l_ref, acc_ref):
    m_prev = m_ref[...]
    m_new = jnp.maximum(m_prev, _reduce_keys(st, jnp.maximum))
    alpha = jnp.exp2(m_prev - m_new)
    p = jnp.exp2(st - m_new)
    l_ref[...] = alpha * l_ref[...] + _reduce_keys(p, jnp.add)
    acc_ref[...] = alpha * acc_ref[...] + jnp.dot(vt, p.astype(BF16), preferred_element_type=F32)
    m_ref[...] = m_new


def _causal_ok_t(i, j, tq, tk):
    key = j * tk + lax.broadcasted_iota(I32, (tk, tq), 0)
    qry = i * tq + lax.broadcasted_iota(I32, (tk, tq), 1)
    return key <= qry


def _init_softmax_state(m_ref, l_ref, acc_ref):
    m_ref[...] = jnp.full(m_ref.shape, MASK_VALUE, F32)
    l_ref[...] = jnp.zeros(l_ref.shape, F32)
    acc_ref[...] = jnp.zeros(acc_ref.shape, F32)


def _pingpong_tiles(n_full, logits, consume, buf_a, buf_b):
    logits(0, buf_a)

    def pair(p, c):
        j = 2 * p
        logits(j + 1, buf_b)
        consume(j, buf_a, False)
        logits(j + 2, buf_a)
        consume(j + 1, buf_b, False)
        return c

    lax.fori_loop(0, n_full // 2, pair, 0)
    odd = n_full % 2 == 1

    @pl.when(odd)
    def _():
        logits(n_full, buf_b)
        consume(n_full - 1, buf_a, False)
        consume(n_full, buf_b, True)

    @pl.when(jnp.logical_not(odd))
    def _():
        consume(n_full, buf_a, True)


def _attn_b_kernel(q_ref, k_ref, vt_ref, o_ref, sa_ref, sb_ref, m_ref, l_ref, acc_ref, *, tq, tk):
    i = pl.program_id(1)
    _init_softmax_state(m_ref, l_ref, acc_ref)
    j_diag = (i * tq) // tk

    def logits(j, buf):
        rows = pl.ds(pl.multiple_of(j * tk, tk), tk)
        for hh in range(B_HEADS_PER_STEP):
            cols = slice(hh * B_QK_PAD, (hh + 1) * B_QK_PAD)
            buf[hh] = _dot_t(k_ref[rows, cols], q_ref[:, cols])

    def consume(j, buf, masked):
        for hh in range(B_HEADS_PER_STEP):
            st = buf[hh]
            if masked:
                st = jnp.where(_causal_ok_t(i, j, tq, tk), st, MASK_VALUE)
            _softmax_step_t(st, vt_ref[hh, j], m_ref.at[hh], l_ref.at[hh], acc_ref.at[hh])

    _pingpong_tiles(j_diag, logits, consume, sa_ref, sb_ref)
    for hh in range(B_HEADS_PER_STEP):
        o_ref[:, hh * B_V:(hh + 1) * B_V] = (acc_ref[hh] / l_ref[hh]).T.astype(o_ref.dtype)


def _attn_b(qb, kb, vbt, *, tq):
    s = qb.shape[0]
    tk = KV_TILE
    hps = B_HEADS_PER_STEP
    assert tk % tq == 0 or tq % tk == 0
    assert tq <= tk
    return pl.pallas_call(
        functools.partial(_attn_b_kernel, tq=tq, tk=tk),
        grid=(B_HEADS // hps, s // tq),
        in_specs=[
            pl.BlockSpec((tq, hps * B_QK_PAD), lambda h, i: (i, h)),
            pl.BlockSpec((s, hps * B_QK_PAD), lambda h, i: (0, h)),
            pl.BlockSpec((hps, s // tk, B_V, tk), lambda h, i: (h, 0, 0, 0)),
        ],
        out_specs=pl.BlockSpec((tq, hps * B_V), lambda h, i: (i, h)),
        out_shape=jax.ShapeDtypeStruct((s, B_WIDTH), BF16),
        scratch_shapes=[pltpu.VMEM((hps, tk, tq), F32), pltpu.VMEM((hps, tk, tq), F32),
                        pltpu.VMEM((hps, 1, tq), F32), pltpu.VMEM((hps, 1, tq), F32),
                        pltpu.VMEM((hps, B_V, tq), F32)],
        compiler_params=_cparams("parallel", "arbitrary"),
        name="attn_b",
    )(qb, kb, vbt)


def _order_key(x):
    b = lax.bitcast_convert_type(x + 0.0, I32)
    return b ^ (lax.shift_right_arithmetic(b, 31) & 0x7FFFFFFF)


def _count_hits(hit):
    tk, tq = hit.shape
    n = tk // SUBLANES
    t = jnp.where(hit, 1, 0).reshape(n, SUBLANES, tq)
    while n > 1:
        n //= 2
        t = t[:n] + t[n:2 * n]
    return t[0]


def _attn_a_kernel(qmin_ref, kmax_ref,
                   iq_ref, wt_ref, kab_ref, aqn_ref, akn_ref, avt_ref, pq_ref, pk_ref, lut_ref, cfar_ref,
                   o_ref,
                   key_ref, sa_ref, sb_ref, m_ref, l_ref, acc_ref, *, tq, tk, topk):
    i = pl.program_id(0)
    j_diag = (i * tq) // tk
    n_tiles = j_diag + 1
    int_min = jnp.int32(-2 ** 31)

    def score_tile(j, masked):
        rows = pl.ds(pl.multiple_of(j * tk, tk), tk)
        k_lo = kab_ref[rows, 0:LANES]
        k_hi = kab_ref[rows, LANES:2 * LANES]
        acc = jnp.zeros((tk, tq), F32)
        for p in range(IDX_HEADS // 2):
            rhs = iq_ref[:, p * LANES:(p + 1) * LANES]
            w_row = IDX_DIM + 2 * p
            acc = acc + jnp.maximum(_dot_t(k_lo, rhs), 0.0) * wt_ref[w_row:w_row + 1, :]
            acc = acc + jnp.maximum(_dot_t(k_hi, rhs), 0.0) * wt_ref[w_row + 1:w_row + 2, :]
        if masked:
            acc = jnp.where(_causal_ok_t(i, j, tq, tk), acc, -jnp.inf)
        key_ref[j] = _order_key(acc)

    def score_body(j, c):
        score_tile(j, False)
        return c

    lax.fori_loop(0, j_diag, score_body, 0)
    score_tile(j_diag, True)

    def count_queries(pred_fn):
        def body(j, cnt):
            return cnt + _count_hits(pred_fn(key_ref[j], j))
        cnt = lax.fori_loop(0, n_tiles, body, jnp.zeros((SUBLANES, tq), I32))
        return jnp.sum(cnt, axis=0, keepdims=True)

    def bit_body(it, t):
        cand = t + lax.shift_left(jnp.int32(1), 31 - it)
        cnt = count_queries(lambda x, j: x >= cand)
        return jnp.where(cnt >= topk, cand, t)

    thr = lax.fori_loop(0, 32, bit_body, jnp.full((1, tq), int_min, I32))
    cnt_ge = count_queries(lambda x, j: x >= thr)

    @pl.when(jnp.max(cnt_ge) > topk)
    def _():
        cnt_gt = count_queries(lambda x, j: x > thr)
        need = topk - cnt_gt
        kpos = lax.broadcasted_iota(I32, (tk, tq), 0)

        def pos_body(it, cut):
            cand = cut + lax.shift_left(jnp.int32(1), 30 - it)
            cnt = count_queries(lambda x, j: (x == thr) & (j * tk + kpos < cand))
            return jnp.where(cnt < need, cand, cut)

        cut = lax.fori_loop(0, 31, pos_body, jnp.zeros((1, tq), I32))

        def demote_body(j, c):
            x = key_ref[j]
            drop = (x == thr) & (j * tk + kpos > cut) & (thr > int_min)
            key_ref[j] = jnp.where(drop, x - 1, x)
            return c

        lax.fori_loop(0, n_tiles, demote_body, 0)

    _init_softmax_state(m_ref, l_ref, acc_ref)
    pq = pq_ref[...]

    def attend(j, masked, near):
        rows = pl.ds(pl.multiple_of(j * tk, tk), tk)
        sel = key_ref[j] >= thr
        if masked:
            sel = sel & _causal_ok_t(i, j, tq, tk)
        mask_bias = jnp.where(sel, 0.0, MASK_VALUE)
        if near:
            dist = jnp.clip(pq - pk_ref[rows, :], 0, LANES - 1)

        def logits(h, buf):
            g = h // A_REP
            buf[...] = _dot_t(akn_ref[rows, g * A_HEAD_DIM:(g + 1) * A_HEAD_DIM],
                              aqn_ref[:, h * A_HEAD_DIM:(h + 1) * A_HEAD_DIM])

        bufs = (sa_ref, sb_ref)
        logits(0, bufs[0])
        for h in range(A_HEADS):
            if h + 1 < A_HEADS:
                logits(h + 1, bufs[(h + 1) % 2])
            st = bufs[h % 2][...] + mask_bias
            if near:
                table = jnp.broadcast_to(lut_ref[h:h + 1, :], (tk, LANES))
                st = st + jnp.concatenate(
                    [jnp.take_along_axis(table, dist[:, c * LANES:(c + 1) * LANES], axis=1)
                     for c in range(tq // LANES)], axis=1)
            else:
                st = st + cfar_ref[h]
            _softmax_step_t(st, avt_ref[h // A_REP, j], m_ref.at[h], l_ref.at[h], acc_ref.at[h])

    def attend_dyn(j, masked):
        far = qmin_ref[i] - kmax_ref[j] >= T5_FAR

        @pl.when(far)
        def _():
            attend(j, masked, False)

        @pl.when(jnp.logical_not(far))
        def _():
            attend(j, masked, True)

    def attend_body(j, c):
        attend_dyn(j, False)
        return c

    lax.fori_loop(0, j_diag, attend_body, 0)
    attend_dyn(j_diag, True)

    for h in range(A_HEADS):
        q_cols = slice(h * A_HEAD_DIM, (h + 1) * A_HEAD_DIM)
        o_ref[:, q_cols] = (acc_ref[h] / l_ref[h]).T.astype(o_ref.dtype)


def _attn_a(proj, wt, kab, aqn, akn, avt, pos_row, pos_col, lut_t, c_far, qmin, kmax, *, tq, topk):
    s = proj.shape[0]
    tk = KV_TILE
    assert tk % tq == 0
    iq_blk = SEG_START["iq"] // SEG_UNITS["iq"]
    grid_spec = pltpu.PrefetchScalarGridSpec(
        num_scalar_prefetch=2,
        grid=(s // tq,),
        in_specs=[
            pl.BlockSpec((tq, IDX_HEADS * IDX_DIM), lambda i, *_: (i, iq_blk)),
            pl.BlockSpec((LANES, tq), lambda i, *_: (0, i)),
            pl.BlockSpec((s, 2 * LANES), lambda i, *_: (0, 0)),
            pl.BlockSpec((tq, A_WIDTH), lambda i, *_: (i, 0)),
            pl.BlockSpec((s, A_KV_WIDTH), lambda i, *_: (0, 0)),
            pl.BlockSpec((A_KV_HEADS, s // tk, A_HEAD_DIM, tk), lambda i, *_: (0, 0, 0, 0)),
            pl.BlockSpec((1, tq), lambda i, *_: (0, i)),
            pl.BlockSpec((s, 1), lambda i, *_: (0, 0)),
            pl.BlockSpec((A_HEADS, LANES), lambda i, *_: (0, 0)),
            pl.BlockSpec(memory_space=pltpu.SMEM),
        ],
        out_specs=pl.BlockSpec((tq, A_WIDTH), lambda i, *_: (i, 0)),
        scratch_shapes=[
            pltpu.VMEM((s // tk, tk, tq), I32),
            pltpu.VMEM((tk, tq), F32),
            pltpu.VMEM((tk, tq), F32),
            pltpu.VMEM((A_HEADS, 1, tq), F32),
            pltpu.VMEM((A_HEADS, 1, tq), F32),
            pltpu.VMEM((A_HEADS, A_HEAD_DIM, tq), F32),
        ],
    )
    return pl.pallas_call(
        functools.partial(_attn_a_kernel, tq=tq, tk=tk, topk=topk),
        grid_spec=grid_spec,
        out_shape=jax.ShapeDtypeStruct((s, A_WIDTH), BF16),
        compiler_params=_cparams("arbitrary"),
        name="attn_a",
    )(qmin, kmax, proj, wt, kab, aqn, akn, avt, pos_row, pos_col, lut_t, c_far)


def _attn_c_kernel(q_ref, kv_ref, o_ref):
    for h in range(C_HEADS):
        cols = slice(h * C_HEAD_DIM, (h + 1) * C_HEAD_DIM)
        s = _dot_t(q_ref[:, cols], kv_ref[:, cols])
        p = jnp.exp(s - jnp.max(s, axis=-1, keepdims=True))
        o = jnp.dot(p.astype(BF16), kv_ref[:, C_WIDTH + h * C_HEAD_DIM:C_WIDTH + (h + 1) * C_HEAD_DIM],
                    preferred_element_type=F32)
        o_ref[:, cols] = (o / jnp.sum(p, axis=-1, keepdims=True)).astype(o_ref.dtype)


def _attn_c(cqn, mem_kv, *, tm):
    s = cqn.shape[0]
    return pl.pallas_call(
        _attn_c_kernel,
        grid=(s // tm,),
        in_specs=[pl.BlockSpec((tm, C_WIDTH), lambda i: (i, 0)), _full_spec(mem_kv.shape)],
        out_specs=pl.BlockSpec((tm, C_WIDTH), lambda i: (i, 0)),
        out_shape=jax.ShapeDtypeStruct((s, C_WIDTH), BF16),
        compiler_params=_cparams("parallel"),
        name="attn_c",
    )(cqn, mem_kv)


def _merge_out_kernel(x_ref, oa_ref, ob_ref, oc_ref, az_ref, bz_ref, cz_ref, ga_ref, gb_ref, gc_ref,
                      wbr_ref, wout_ref, o_ref):
    merged = None
    for n, (o_r, z_r, g_r) in enumerate(((oa_ref, az_ref, ga_ref), (ob_ref, bz_ref, gb_ref),
                                         (oc_ref, cz_ref, gc_ref))):
        z = z_r[...].astype(F32)
        u = (o_r[...].astype(F32) * (z * jax.nn.sigmoid(z))).astype(BF16)
        y = jnp.dot(u, wbr_ref[n], preferred_element_type=F32)
        t = jax.nn.sigmoid(g_r[...].astype(F32)) * y
        merged = t if merged is None else merged + t
    o_ref[...] = x_ref[...] + jnp.dot(merged.astype(BF16), wout_ref[...], preferred_element_type=F32)


def _merge_out(x, o_a, o_b, o_c, proj, w_branch, w_out, *, tm):
    s = x.shape[0]
    row = lambda w: pl.BlockSpec((tm, w), lambda i: (i, 0))
    gate_blk = SEG_START["gates"] * LANES // D_MODEL
    gate = lambda n: pl.BlockSpec((tm, D_MODEL), lambda i: (i, gate_blk + n))
    single = pl.Buffered(1)
    return pl.pallas_call(
        _merge_out_kernel,
        grid=(s // tm,),
        in_specs=[row(D_MODEL), row(BRANCH_WIDTH), row(BRANCH_WIDTH), row(BRANCH_WIDTH),
                  _seg_spec(tm, "az"), _seg_spec(tm, "bz"), _seg_spec(tm, "cz"),
                  gate(0), gate(1), gate(2),
                  pl.BlockSpec(w_branch.shape, lambda i: (0, 0, 0), pipeline_mode=single),
                  pl.BlockSpec(w_out.shape, lambda i: (0, 0), pipeline_mode=single)],
        out_specs=row(D_MODEL),
        out_shape=jax.ShapeDtypeStruct((s, D_MODEL), F32),
        compiler_params=_cparams("parallel"),
        name="merge_out",
    )(x, o_a, o_b, o_c, proj, proj, proj, proj, proj, proj, w_branch, w_out)


def _regroup_w_in(w_in):
    (aq, ak, av, az, iq, ik, iw, bcq, bckv, bkpe, bz, cq, cz, gates) = jnp.split(w_in, IN_OFFSETS, axis=1)
    d = w_in.shape[0]
    ikw = jnp.concatenate([ik, iw, jnp.zeros((d, LANES - IDX_DIM - IDX_HEADS), w_in.dtype)], axis=1)
    bkpe_p = jnp.concatenate([bkpe, jnp.zeros((d, LANES - B_ROPE), w_in.dtype)], axis=1)
    segs = dict(gates=gates, aq=aq, az=az, iq=iq, bz=bz, cq=cq, cz=cz, bcq=bcq, ak=ak, av=av,
                bckv=bckv, ikw=ikw, bkpe=bkpe_p)
    return jnp.concatenate([segs[n] for n in SEG_UNITS], axis=1).astype(BF16)


def _pad_heads(w, n_heads, width, pad_to):
    r = w.shape[0]
    w = w.reshape(r, n_heads, width)
    w = jnp.pad(w, ((0, 0), (0, 0), (0, pad_to - width)))
    return w.reshape(r, n_heads * pad_to)


def kernel(x, mem, positions, rel_bias, norm_g, mem_norm_g, w_in, a_q_norm_g, a_k_norm_g, b_q_lat_norm_g,
           b_kv_lat_norm_g, w_b_uq, w_b_ukv, b_q_norm_g, b_k_norm_g, w_mem_kv, c_q_norm_g, c_k_norm_g,
           w_branch, w_out):
    bsz, seq, d = x.shape
    assert d == D_MODEL and norm_g.shape[0] == 1
    topk = min(TOPK_MAX, seq // 4)
    tq_a = 256
    tq_b = 512
    tm_in = min(1024, seq)

    inv_freq = 1.0 / (ROPE_THETA ** (jnp.arange(0, B_ROPE, 2, dtype=F32) / B_ROPE))
    zeros_half = jnp.zeros((B_ROPE,), F32)
    gq_pad = jnp.concatenate([b_q_norm_g[0], zeros_half])[None, :]
    gk_pad = jnp.concatenate([b_k_norm_g[0], zeros_half])[None, :]
    lut_t = rel_bias[jnp.asarray(T5_TABLE)].T.astype(F32) * LOG2E
    c_far = rel_bias[REL_BUCKETS - 1].astype(F32) * LOG2E

    w_cat = _regroup_w_in(w_in[0])
    wuq = _pad_heads(w_b_uq[0], B_HEADS, B_QK_DIM, B_QK_PAD).astype(BF16)
    wukv = w_b_ukv[0].reshape(B_KV_RANK, B_HEADS, B_NOPE + B_V)
    wuk = wukv[:, :, :B_NOPE].reshape(B_KV_RANK, B_HEADS * B_NOPE).astype(BF16)
    wuvt = jnp.transpose(wukv[:, :, B_NOPE:], (1, 2, 0)).astype(BF16)
    wbr = w_branch[0].astype(BF16)
    wout = w_out[0].astype(BF16)

    outs = []
    for b in range(bsz):
        pos = positions[b]
        ang = pos.astype(F32)[:, None] * inv_freq
        cos, sin = jnp.cos(ang), jnp.sin(ang)
        zpad = jnp.zeros((seq, LANES - B_ROPE), F32)
        cos128 = jnp.concatenate([cos, cos, zpad], axis=1)
        sin128 = jnp.concatenate([-sin, sin, zpad], axis=1)

        mem_kv = _mem_kv(mem[b], mem_norm_g[0][None, :], w_mem_kv[0], c_k_norm_g[0][None, :])
        proj = _in_proj(x[b], norm_g[0][None, :], w_cat, tm=tm_in, tn=512)
        aqn, akn, kab, wt, avt, qb, kb, vbt, cqn = _prep(
            proj, cos128, sin128, a_q_norm_g[0][None, :], a_k_norm_g[0][None, :],
            b_q_lat_norm_g[0][None, :], b_kv_lat_norm_g[0][None, :], gq_pad, gk_pad,
            c_q_norm_g[0][None, :], wuq, wuk, wuvt)
        o_b = _attn_b(qb, kb, vbt, tq=tq_b)
        qmin = jnp.min(pos.reshape(seq // tq_a, tq_a), axis=1)
        kmax = jnp.max(pos.reshape(seq // KV_TILE, KV_TILE), axis=1)
        o_a = _attn_a(proj, wt, kab, aqn, akn, avt, pos[None, :], pos[:, None],
                      lut_t, c_far, qmin, kmax, tq=tq_a, topk=topk)
        o_c = _attn_c(cqn, mem_kv, tm=512)
        outs.append(_merge_out(x[b], o_a, o_b, o_c, proj, wbr, wout, tm=256))
    return jnp.stack(outs, axis=0)
```

```python
import functools
import math

import numpy as np
import jax
import jax.numpy as jnp
from jax import lax
from jax.experimental import pallas as pl
from jax.experimental.pallas import tpu as pltpu

F32 = jnp.float32
BF16 = jnp.bfloat16
I32 = jnp.int32

D_MODEL = 2048
BRANCH_WIDTH = 1024
N_BRANCH = 3
A_HEADS = 8
A_KV_HEADS = 2
A_HEAD_DIM = 128
A_WIDTH = A_HEADS * A_HEAD_DIM
A_KV_WIDTH = A_KV_HEADS * A_HEAD_DIM
A_REP = A_HEADS // A_KV_HEADS
IDX_HEADS = 16
IDX_DIM = 64
TOPK_MAX = 256
IDX_SCALE = (IDX_DIM ** -0.5) * (IDX_HEADS ** -0.5)
A_SCALE = A_HEAD_DIM ** -0.5
B_HEADS = 8
B_Q_RANK = 512
B_KV_RANK = 256
B_NOPE = 128
B_ROPE = 64
B_QK_DIM = B_NOPE + B_ROPE
B_V = 128
B_WIDTH = B_HEADS * B_V
B_SCALE = B_QK_DIM ** -0.5
ROPE_THETA = 10000.0
C_HEADS = 4
C_HEAD_DIM = 256
C_WIDTH = C_HEADS * C_HEAD_DIM
C_SCALE = C_HEAD_DIM ** -0.5
REL_BUCKETS = 32
REL_MAX_DIST = 128
EPS = 1e-6

IN_SIZES = (A_WIDTH, A_KV_WIDTH, A_KV_WIDTH, A_WIDTH, IDX_HEADS * IDX_DIM, IDX_DIM, IDX_HEADS,
            B_Q_RANK, B_KV_RANK, B_ROPE, B_WIDTH, C_WIDTH, C_WIDTH, N_BRANCH * D_MODEL)
IN_OFFSETS = [int(o) for o in np.cumsum(IN_SIZES)[:-1]]

LANES = 128
SUBLANES = 8
B_QK_PAD = 2 * LANES
VMEM_LIMIT_BYTES = 56 * 1024 * 1024
MASK_VALUE = -1e30
LOG2E = math.log2(math.e)
KV_TILE = 512
B_HEADS_PER_STEP = 2

SEG_UNITS = dict(gates=48, aq=8, az=8, iq=8, bz=8, cq=8, cz=8, bcq=4, ak=2, av=2, bckv=2, ikw=1, bkpe=1)
SEG_START = {}
_u = 0
for _name, _w in SEG_UNITS.items():
    SEG_START[_name] = _u
    _u += _w
PROJ_UNITS = _u
PROJ_WIDTH = PROJ_UNITS * LANES


def _t5_bucket_table():
    d = np.arange(LANES)
    max_exact = REL_BUCKETS // 2
    nf = np.maximum(d, 1).astype(np.float64)
    large = max_exact + (np.log(nf / max_exact) / math.log(REL_MAX_DIST / max_exact)
                         * (REL_BUCKETS - max_exact)).astype(np.int64)
    large = np.minimum(large, REL_BUCKETS - 1)
    table = np.where(d < max_exact, d, large).astype(np.int32)
    far = int(np.min(np.nonzero(table == REL_BUCKETS - 1)[0]))
    assert np.all(table[far:] == REL_BUCKETS - 1)
    return table, far


T5_TABLE, T5_FAR = _t5_bucket_table()


def _cparams(*sem):
    return pltpu.CompilerParams(dimension_semantics=sem, vmem_limit_bytes=VMEM_LIMIT_BYTES)


def _dot_t(a, b):
    return lax.dot_general(a, b, (((1,), (1,)), ((), ())), preferred_element_type=F32)


def _rms_scale(ss, n):
    return lax.rsqrt(ss * (1.0 / n) + EPS)


def _mem_kv_kernel(mem_ref, g_ref, w_ref, ckn_ref, o_ref, h_ref):
    j = pl.program_id(0)

    @pl.when(j == 0)
    def _():
        m = mem_ref[...]
        r = _rms_scale(jnp.sum(m * m, axis=-1, keepdims=True), D_MODEL)
        h_ref[...] = (m * r * g_ref[...]).astype(BF16)

    y = jnp.dot(h_ref[...], w_ref[...].astype(BF16), preferred_element_type=F32)

    @pl.when(j < C_HEADS)
    def _():
        r = _rms_scale(jnp.sum(y * y, axis=-1, keepdims=True), C_HEAD_DIM)
        o_ref[...] = (y * r * ckn_ref[...]).astype(BF16)

    @pl.when(j >= C_HEADS)
    def _():
        o_ref[...] = y.astype(BF16)


def _mem_kv(mem, mem_g, w_mem_kv, ck_g):
    n_mem = mem.shape[0]
    return pl.pallas_call(
        _mem_kv_kernel,
        grid=(2 * C_HEADS,),
        in_specs=[
            pl.BlockSpec((n_mem, D_MODEL), lambda j: (0, 0)),
            pl.BlockSpec((1, D_MODEL), lambda j: (0, 0)),
            pl.BlockSpec((D_MODEL, C_HEAD_DIM), lambda j: (0, j)),
            pl.BlockSpec((1, C_HEAD_DIM), lambda j: (0, 0)),
        ],
        out_specs=pl.BlockSpec((n_mem, C_HEAD_DIM), lambda j: (0, j)),
        out_shape=jax.ShapeDtypeStruct((n_mem, 2 * C_WIDTH), BF16),
        scratch_shapes=[pltpu.VMEM((n_mem, D_MODEL), BF16)],
        compiler_params=_cparams("arbitrary"),
        name="mem_kv",
    )(mem, mem_g, w_mem_kv, ck_g)


def _in_proj_kernel(x_ref, g_ref, w_ref, o_ref, h_ref, *, row_chunk):
    @pl.when(pl.program_id(1) == 0)
    def _():
        g = g_ref[...]
        for c in range(x_ref.shape[0] // row_chunk):
            rows = pl.ds(c * row_chunk, row_chunk)
            x = x_ref[rows, :]
            r = _rms_scale(jnp.sum(x * x, axis=-1, keepdims=True), D_MODEL)
            h_ref[rows, :] = (x * r * g).astype(BF16)

    o_ref[...] = jnp.dot(h_ref[...], w_ref[...], preferred_element_type=F32).astype(o_ref.dtype)


def _in_proj(x, norm_g, w_cat, *, tm, tn):
    s = x.shape[0]
    return pl.pallas_call(
        functools.partial(_in_proj_kernel, row_chunk=min(tm, 256)),
        grid=(s // tm, PROJ_WIDTH // tn),
        in_specs=[
            pl.BlockSpec((tm, D_MODEL), lambda i, j: (i, 0)),
            pl.BlockSpec((1, D_MODEL), lambda i, j: (0, 0)),
            pl.BlockSpec((D_MODEL, tn), lambda i, j: (0, j)),
        ],
        out_specs=pl.BlockSpec((tm, tn), lambda i, j: (i, j)),
        out_shape=jax.ShapeDtypeStruct((s, PROJ_WIDTH), BF16),
        scratch_shapes=[pltpu.VMEM((tm, D_MODEL), BF16)],
        compiler_params=_cparams("parallel", "arbitrary"),
        name="in_proj",
    )(x, norm_g, w_cat)


def _rope128(x, cos, sin):
    half = B_ROPE // 2
    lane = lax.broadcasted_iota(I32, x.shape, 1)
    partner = jnp.where(lane < half, pltpu.roll(x, LANES - half, 1), pltpu.roll(x, half, 1))
    return x * cos + partner * sin


def _prep_kernel(aq_ref, ak_ref, av_ref, ikw_ref, bcq_ref, bckv_ref, bkpe_ref, cq_ref,
                 cos_ref, sin_ref, aqg_ref, akg_ref, bqlg_ref, bkvlg_ref, bqg_ref, bkg_ref, cqg_ref,
                 wuq_ref, wuk_ref, wuvt_ref,
                 aqn_ref, akn_ref, kab_ref, wt_ref, avt_ref, qb_ref, kb_ref, vbt_ref, cqn_ref):
    cos = cos_ref[...]
    sin = sin_ref[...]
    eye = jnp.where(lax.broadcasted_iota(I32, (LANES, LANES), 0) == lax.broadcasted_iota(I32, (LANES, LANES), 1),
                    1.0, 0.0).astype(BF16)

    aqg = aqg_ref[...] * (A_SCALE * LOG2E)
    for h in range(A_HEADS):
        cols = slice(h * A_HEAD_DIM, (h + 1) * A_HEAD_DIM)
        v = aq_ref[:, cols].astype(F32)
        r = _rms_scale(jnp.sum(v * v, axis=-1, keepdims=True), A_HEAD_DIM)
        aqn_ref[:, cols] = (v * r * aqg).astype(BF16)
    akg = akg_ref[...]
    for h in range(A_KV_HEADS):
        cols = slice(h * A_HEAD_DIM, (h + 1) * A_HEAD_DIM)
        v = ak_ref[:, cols].astype(F32)
        r = _rms_scale(jnp.sum(v * v, axis=-1, keepdims=True), A_HEAD_DIM)
        akn_ref[:, cols] = (v * r * akg).astype(BF16)
        avt_ref[h, 0] = _dot_t(eye, av_ref[:, cols]).astype(BF16)

    ikw = ikw_ref[...]
    lane = lax.broadcasted_iota(I32, ikw.shape, 1)
    zero = jnp.zeros_like(ikw)
    kab_ref[:, 0:LANES] = jnp.where(lane < IDX_DIM, ikw, zero)
    kab_ref[:, LANES:2 * LANES] = jnp.where(lane >= IDX_DIM, pltpu.roll(ikw.astype(F32), IDX_DIM, 1).astype(BF16), zero)
    wt_ref[...] = _dot_t(eye, ikw) * IDX_SCALE

    cq_lat = bcq_ref[...].astype(F32)
    r = _rms_scale(jnp.sum(cq_lat * cq_lat, axis=-1, keepdims=True), B_Q_RANK)
    qlat = (cq_lat * r * bqlg_ref[...]).astype(BF16)
    bqg = bqg_ref[...] * (B_SCALE * LOG2E)
    for h in range(B_HEADS):
        cols = slice(h * B_QK_PAD, (h + 1) * B_QK_PAD)
        qh = jnp.dot(qlat, wuq_ref[:, cols], preferred_element_type=F32)
        r = _rms_scale(jnp.sum(qh * qh, axis=-1, keepdims=True), B_QK_DIM)
        qn = qh * r * bqg
        qb_ref[:, h * B_QK_PAD:h * B_QK_PAD + LANES] = qn[:, :LANES].astype(BF16)
        qb_ref[:, h * B_QK_PAD + LANES:(h + 1) * B_QK_PAD] = _rope128(qn[:, LANES:], cos, sin).astype(BF16)

    ckv = bckv_ref[...].astype(F32)
    r = _rms_scale(jnp.sum(ckv * ckv, axis=-1, keepdims=True), B_KV_RANK)
    kvlat = (ckv * r * bkvlg_ref[...]).astype(BF16)
    kpe = bkpe_ref[...].astype(F32)
    ss_pe = jnp.sum(kpe * kpe, axis=-1, keepdims=True)
    bkg = bkg_ref[...]
    kpe_rot = _rope128(kpe * bkg[:, LANES:], cos, sin)
    for h in range(B_HEADS):
        kn = jnp.dot(kvlat, wuk_ref[:, h * B_NOPE:(h + 1) * B_NOPE], preferred_element_type=F32)
        r = _rms_scale(jnp.sum(kn * kn, axis=-1, keepdims=True) + ss_pe, B_QK_DIM)
        kb_ref[:, h * B_QK_PAD:h * B_QK_PAD + LANES] = (kn * r * bkg[:, :LANES]).astype(BF16)
        kb_ref[:, h * B_QK_PAD + LANES:(h + 1) * B_QK_PAD] = (kpe_rot * r).astype(BF16)
        vbt_ref[h, 0] = _dot_t(wuvt_ref[h], kvlat).astype(BF16)

    cqg = cqg_ref[...] * C_SCALE
    for h in range(C_HEADS):
        cols = slice(h * C_HEAD_DIM, (h + 1) * C_HEAD_DIM)
        v = cq_ref[:, cols].astype(F32)
        r = _rms_scale(jnp.sum(v * v, axis=-1, keepdims=True), C_HEAD_DIM)
        cqn_ref[:, cols] = (v * r * cqg).astype(BF16)


def _seg_spec(tm, name):
    units = SEG_UNITS[name]
    blk = SEG_START[name] // units
    return pl.BlockSpec((tm, units * LANES), lambda i: (i, blk))


def _full_spec(shape):
    return pl.BlockSpec(shape, lambda i: (0,) * len(shape))


def _prep(proj, cos128, sin128, aqg, akg, bqlg, bkvlg, bqg, bkg, cqg, wuq, wuk, wuvt):
    s = proj.shape[0]
    tm = KV_TILE
    nt = s // tm
    row = lambda w: pl.BlockSpec((tm, w), lambda i: (i, 0))
    tiles_t = lambda n: pl.BlockSpec((n, 1, LANES, tm), lambda i: (0, i, 0, 0))
    small = [aqg, akg, bqlg, bkvlg, bqg, bkg, cqg, wuq, wuk, wuvt]
    out_specs = [row(A_WIDTH), row(A_KV_WIDTH), row(2 * LANES),
                 pl.BlockSpec((LANES, tm), lambda i: (0, i)), tiles_t(A_KV_HEADS),
                 row(B_HEADS * B_QK_PAD), row(B_HEADS * B_QK_PAD), tiles_t(B_HEADS), row(C_WIDTH)]
    out_shape = [
        jax.ShapeDtypeStruct((s, A_WIDTH), BF16),
        jax.ShapeDtypeStruct((s, A_KV_WIDTH), BF16),
        jax.ShapeDtypeStruct((s, 2 * LANES), BF16),
        jax.ShapeDtypeStruct((LANES, s), F32),
        jax.ShapeDtypeStruct((A_KV_HEADS, nt, LANES, tm), BF16),
        jax.ShapeDtypeStruct((s, B_HEADS * B_QK_PAD), BF16),
        jax.ShapeDtypeStruct((s, B_HEADS * B_QK_PAD), BF16),
        jax.ShapeDtypeStruct((B_HEADS, nt, B_V, tm), BF16),
        jax.ShapeDtypeStruct((s, C_WIDTH), BF16),
    ]
    return pl.pallas_call(
        _prep_kernel,
        grid=(nt,),
        in_specs=[_seg_spec(tm, n) for n in ("aq", "ak", "av", "ikw", "bcq", "bckv", "bkpe", "cq")]
        + [row(LANES), row(LANES)] + [_full_spec(a.shape) for a in small],
        out_specs=out_specs,
        out_shape=out_shape,
        compiler_params=_cparams("parallel"),
        name="prep",
    )(*([proj] * 8), cos128, sin128, *small)


def _reduce_keys(x, op):
    tk, tq = x.shape
    n = tk // SUBLANES
    assert n & (n - 1) == 0
    t = x.reshape(n, SUBLANES, tq)
    while n > 1:
        n //= 2
        t = op(t[:n], t[n:2 * n])
    red = jnp.max if op is jnp.maximum else jnp.sum
    return red(t[0], axis=0, keepdims=True)


def _softmax_step_t(st, vt, m_ref, l_ref, acc_ref):
    m_prev = m_ref[...]
    m_new = jnp.maximum(m_prev, _reduce_keys(st, jnp.maximum))
    alpha = jnp.exp2(m_prev - m_new)
    p = jnp.exp2(st - m_new)
    l_ref[...] = alpha * l_ref[...] + _reduce_keys(p, jnp.add)
    acc_ref[...] = alpha * acc_ref[...] + jnp.dot(vt, p.astype(BF16), preferred_element_type=F32)
    m_ref[...] = m_new


def _causal_ok_t(i, j, tq, tk):
    key = j * tk + lax.broadcasted_iota(I32, (tk, tq), 0)
    qry = i * tq + lax.broadcasted_iota(I32, (tk, tq), 1)
    return key <= qry


def _init_softmax_state(m_ref, l_ref, acc_ref):
    m_ref[...] = jnp.full(m_ref.shape, MASK_VALUE, F32)
    l_ref[...] = jnp.zeros(l_ref.shape, F32)
    acc_ref[...] = jnp.zeros(acc_ref.shape, F32)


def _pingpong_tiles(n_full, logits, consume, buf_a, buf_b):
    logits(0, buf_a)

    def pair(p, c):
        j = 2 * p
        logits(j + 1, buf_b)
        consume(j, buf_a, False)
        logits(j + 2, buf_a)
        consume(j + 1, buf_b, False)
        return c

    lax.fori_loop(0, n_full // 2, pair, 0)
    odd = n_full % 2 == 1

    @pl.when(odd)
    def _():
        logits(n_full, buf_b)
        consume(n_full - 1, buf_a, False)
        consume(n_full, buf_b, True)

    @pl.when(jnp.logical_not(odd))
    def _():
        consume(n_full, buf_a, True)


def _attn_b_kernel(q_ref, k_ref, vt_ref, o_ref, sa_ref, sb_ref, m_ref, l_ref, acc_ref, *, tq, tk):
    i = pl.program_id(1)
    _init_softmax_state(m_ref, l_ref, acc_ref)
    j_diag = (i * tq) // tk

    def logits(j, buf):
        rows = pl.ds(pl.multiple_of(j * tk, tk), tk)
        for hh in range(B_HEADS_PER_STEP):
            cols = slice(hh * B_QK_PAD, (hh + 1) * B_QK_PAD)
            buf[hh] = _dot_t(k_ref[rows, cols], q_ref[:, cols])

    def consume(j, buf, masked):
        for hh in range(B_HEADS_PER_STEP):
            st = buf[hh]
            if masked:
                st = jnp.where(_causal_ok_t(i, j, tq, tk), st, MASK_VALUE)
            _softmax_step_t(st, vt_ref[hh, j], m_ref.at[hh], l_ref.at[hh], acc_ref.at[hh])

    _pingpong_tiles(j_diag, logits, consume, sa_ref, sb_ref)
    for hh in range(B_HEADS_PER_STEP):
        o_ref[:, hh * B_V:(hh + 1) * B_V] = (acc_ref[hh] / l_ref[hh]).T.astype(o_ref.dtype)


def _attn_b(qb, kb, vbt, *, tq):
    s = qb.shape[0]
    tk = KV_TILE
    hps = B_HEADS_PER_STEP
    assert tk % tq == 0 or tq % tk == 0
    assert tq <= tk
    return pl.pallas_call(
        functools.partial(_attn_b_kernel, tq=tq, tk=tk),
        grid=(B_HEADS // hps, s // tq),
        in_specs=[
            pl.BlockSpec((tq, hps * B_QK_PAD), lambda h, i: (i, h)),
            pl.BlockSpec((s, hps * B_QK_PAD), lambda h, i: (0, h)),
            pl.BlockSpec((hps, s // tk, B_V, tk), lambda h, i: (h, 0, 0, 0)),
        ],
        out_specs=pl.BlockSpec((tq, hps * B_V), lambda h, i: (i, h)),
        out_shape=jax.ShapeDtypeStruct((s, B_WIDTH), BF16),
        scratch_shapes=[pltpu.VMEM((hps, tk, tq), F32), pltpu.VMEM((hps, tk, tq), F32),
                        pltpu.VMEM((hps, 1, tq), F32), pltpu.VMEM((hps, 1, tq), F32),
                        pltpu.VMEM((hps, B_V, tq), F32)],
        compiler_params=_cparams("parallel", "arbitrary"),
        name="attn_b",
    )(qb, kb, vbt)


def _order_key(x):
    b = lax.bitcast_convert_type(x + 0.0, I32)
    return b ^ (lax.shift_right_arithmetic(b, 31) & 0x7FFFFFFF)


def _count_hits(hit):
    tk, tq = hit.shape
    n = tk // SUBLANES
    t = jnp.where(hit, 1, 0).reshape(n, SUBLANES, tq)
    while n > 1:
        n //= 2
        t = t[:n] + t[n:2 * n]
    return t[0]


def _attn_a_kernel(qmin_ref, kmax_ref,
                   iq_ref, wt_ref, kab_ref, aqn_ref, akn_ref, avt_ref, pq_ref, pk_ref, lut_ref,
                   o_ref,
                   key_ref, sa_ref, sb_ref, m_ref, l_ref, acc_ref, *, tq, tk, topk):
    i = pl.program_id(0)
    j_diag = (i * tq) // tk
    n_tiles = j_diag + 1
    int_min = jnp.int32(-2 ** 31)

    def score_tile(j, masked):
        rows = pl.ds(pl.multiple_of(j * tk, tk), tk)
        k_lo = kab_ref[rows, 0:LANES]
        k_hi = kab_ref[rows, LANES:2 * LANES]
        acc = jnp.zeros((tk, tq), F32)
        for p in range(IDX_HEADS // 2):
            rhs = iq_ref[:, p * LANES:(p + 1) * LANES]
            w_row = IDX_DIM + 2 * p
            acc = acc + jnp.maximum(_dot_t(k_lo, rhs), 0.0) * wt_ref[w_row:w_row + 1, :]
            acc = acc + jnp.maximum(_dot_t(k_hi, rhs), 0.0) * wt_ref[w_row + 1:w_row + 2, :]
        if masked:
            acc = jnp.where(_causal_ok_t(i, j, tq, tk), acc, -jnp.inf)
        key_ref[j] = _order_key(acc)

    def score_body(j, c):
        score_tile(j, False)
        return c

    lax.fori_loop(0, j_diag, score_body, 0)
    score_tile(j_diag, True)

    def count_queries(pred_fn):
        def body(j, cnt):
            return cnt + _count_hits(pred_fn(key_ref[j], j))
        cnt = lax.fori_loop(0, n_tiles, body, jnp.zeros((SUBLANES, tq), I32))
        return jnp.sum(cnt, axis=0, keepdims=True)

    def bit_cond(c):
        it, _, settled = c
        return (it < 32) & (jnp.min(settled) == 0)

    def bit_body(c):
        it, t, settled = c
        cand = t + lax.shift_left(jnp.int32(1), 31 - it)
        cnt = count_queries(lambda x, j: x >= cand)
        t = jnp.where((settled == 0) & (cnt >= topk), cand, t)
        return it + 1, t, jnp.where(cnt == topk, 1, settled)

    _, thr, settled = lax.while_loop(
        bit_cond, bit_body, (jnp.int32(0), jnp.full((1, tq), int_min, I32), jnp.zeros((1, tq), I32)))

    def demote_excess_ties():
        cnt_gt = count_queries(lambda x, j: x > thr)
        need = topk - cnt_gt
        kpos = lax.broadcasted_iota(I32, (tk, tq), 0)

        def pos_body(it, cut):
            cand = cut + lax.shift_left(jnp.int32(1), 30 - it)
            cnt = count_queries(lambda x, j: (x == thr) & (j * tk + kpos < cand))
            return jnp.where(cnt < need, cand, cut)

        cut = lax.fori_loop(0, 31, pos_body, jnp.zeros((1, tq), I32))

        def demote_body(j, c):
            x = key_ref[j]
            drop = (x == thr) & (j * tk + kpos > cut) & (thr > int_min)
            key_ref[j] = jnp.where(drop, x - 1, x)
            return c

        lax.fori_loop(0, n_tiles, demote_body, 0)

    @pl.when(jnp.min(settled) == 0)
    def _():
        cnt_ge = count_queries(lambda x, j: x >= thr)
        pl.when(jnp.max(cnt_ge) > topk)(demote_excess_ties)

    _init_softmax_state(m_ref, l_ref, acc_ref)
    pq = pq_ref[...]

    def attend(j, masked, near):
        rows = pl.ds(pl.multiple_of(j * tk, tk), tk)
        sel = key_ref[j] >= thr
        if masked:
            sel = sel & _causal_ok_t(i, j, tq, tk)
        mask_bias = jnp.where(sel, 0.0, MASK_VALUE)
        if near:
            dist = jnp.clip(pq - pk_ref[rows, :], 0, LANES - 1)

        def logits(h, buf):
            g = h // A_REP
            buf[...] = _dot_t(akn_ref[rows, g * A_HEAD_DIM:(g + 1) * A_HEAD_DIM],
                              aqn_ref[:, h * A_HEAD_DIM:(h + 1) * A_HEAD_DIM])

        bufs = (sa_ref, sb_ref)
        logits(0, bufs[0])
        for h in range(A_HEADS):
            if h + 1 < A_HEADS:
                logits(h + 1, bufs[(h + 1) % 2])
            st = bufs[h % 2][...] + mask_bias
            if near:
                table = jnp.broadcast_to(lut_ref[h:h + 1, :], (tk, LANES))
                st = st + jnp.concatenate(
                    [jnp.take_along_axis(table, dist[:, c * LANES:(c + 1) * LANES], axis=1,
                                         mode="promise_in_bounds")
                     for c in range(tq // LANES)], axis=1)
            _softmax_step_t(st, avt_ref[h // A_REP, j], m_ref.at[h], l_ref.at[h], acc_ref.at[h])

    def attend_dyn(j, masked):
        far = qmin_ref[i] - kmax_ref[j] >= T5_FAR

        @pl.when(far)
        def _():
            attend(j, masked, False)

        @pl.when(jnp.logical_not(far))
        def _():
            attend(j, masked, True)

    def attend_body(j, c):
        attend_dyn(j, False)
        return c

    lax.fori_loop(0, j_diag, attend_body, 0)
    attend_dyn(j_diag, True)

    for h in range(A_HEADS):
        q_cols = slice(h * A_HEAD_DIM, (h + 1) * A_HEAD_DIM)
        o_ref[:, q_cols] = (acc_ref[h] / l_ref[h]).T.astype(o_ref.dtype)


def _attn_a(proj, wt, kab, aqn, akn, avt, pos_row, pos_col, lut_t, qmin, kmax, *, tq, topk):
    s = proj.shape[0]
    tk = KV_TILE
    assert tk % tq == 0
    iq_blk = SEG_START["iq"] // SEG_UNITS["iq"]
    grid_spec = pltpu.PrefetchScalarGridSpec(
        num_scalar_prefetch=2,
        grid=(s // tq,),
        in_specs=[
            pl.BlockSpec((tq, IDX_HEADS * IDX_DIM), lambda i, *_: (i, iq_blk)),
            pl.BlockSpec((LANES, tq), lambda i, *_: (0, i)),
            pl.BlockSpec((s, 2 * LANES), lambda i, *_: (0, 0)),
            pl.BlockSpec((tq, A_WIDTH), lambda i, *_: (i, 0)),
            pl.BlockSpec((s, A_KV_WIDTH), lambda i, *_: (0, 0)),
            pl.BlockSpec((A_KV_HEADS, s // tk, A_HEAD_DIM, tk), lambda i, *_: (0, 0, 0, 0)),
            pl.BlockSpec((1, tq), lambda i, *_: (0, i)),
            pl.BlockSpec((s, 1), lambda i, *_: (0, 0)),
            pl.BlockSpec((A_HEADS, LANES), lambda i, *_: (0, 0)),
        ],
        out_specs=pl.BlockSpec((tq, A_WIDTH), lambda i, *_: (i, 0)),
        scratch_shapes=[
            pltpu.VMEM((s // tk, tk, tq), I32),
            pltpu.VMEM((tk, tq), F32),
            pltpu.VMEM((tk, tq), F32),
            pltpu.VMEM((A_HEADS, 1, tq), F32),
            pltpu.VMEM((A_HEADS, 1, tq), F32),
            pltpu.VMEM((A_HEADS, A_HEAD_DIM, tq), F32),
        ],
    )
    return pl.pallas_call(
        functools.partial(_attn_a_kernel, tq=tq, tk=tk, topk=topk),
        grid_spec=grid_spec,
        out_shape=jax.ShapeDtypeStruct((s, A_WIDTH), BF16),
        compiler_params=_cparams("arbitrary"),
        name="attn_a",
    )(qmin, kmax, proj, wt, kab, aqn, akn, avt, pos_row, pos_col, lut_t)


def _attn_c_kernel(q_ref, kv_ref, o_ref):
    for h in range(C_HEADS):
        cols = slice(h * C_HEAD_DIM, (h + 1) * C_HEAD_DIM)
        s = _dot_t(q_ref[:, cols], kv_ref[:, cols])
        p = jnp.exp(s - jnp.max(s, axis=-1, keepdims=True))
        o = jnp.dot(p.astype(BF16), kv_ref[:, C_WIDTH + h * C_HEAD_DIM:C_WIDTH + (h + 1) * C_HEAD_DIM],
                    preferred_element_type=F32)
        o_ref[:, cols] = (o / jnp.sum(p, axis=-1, keepdims=True)).astype(o_ref.dtype)


def _attn_c(cqn, mem_kv, *, tm):
    s = cqn.shape[0]
    return pl.pallas_call(
        _attn_c_kernel,
        grid=(s // tm,),
        in_specs=[pl.BlockSpec((tm, C_WIDTH), lambda i: (i, 0)), _full_spec(mem_kv.shape)],
        out_specs=pl.BlockSpec((tm, C_WIDTH), lambda i: (i, 0)),
        out_shape=jax.ShapeDtypeStruct((s, C_WIDTH), BF16),
        compiler_params=_cparams("parallel"),
        name="attn_c",
    )(cqn, mem_kv)


def _merge_out_kernel(x_ref, oa_ref, ob_ref, oc_ref, az_ref, bz_ref, cz_ref, ga_ref, gb_ref, gc_ref,
                      wbr_ref, wout_ref, o_ref):
    merged = None
    for n, (o_r, z_r, g_r) in enumerate(((oa_ref, az_ref, ga_ref), (ob_ref, bz_ref, gb_ref),
                                         (oc_ref, cz_ref, gc_ref))):
        z = z_r[...].astype(F32)
        u = (o_r[...].astype(F32) * (z * jax.nn.sigmoid(z))).astype(BF16)
        y = jnp.dot(u, wbr_ref[n], preferred_element_type=F32)
        t = jax.nn.sigmoid(g_r[...].astype(F32)) * y
        merged = t if merged is None else merged + t
    o_ref[...] = x_ref[...] + jnp.dot(merged.astype(BF16), wout_ref[...], preferred_element_type=F32)


def _merge_out(x, o_a, o_b, o_c, proj, w_branch, w_out, *, tm):
    s = x.shape[0]
    row = lambda w: pl.BlockSpec((tm, w), lambda i: (i, 0))
    gate_blk = SEG_START["gates"] * LANES // D_MODEL
    gate = lambda n: pl.BlockSpec((tm, D_MODEL), lambda i: (i, gate_blk + n))
    single = pl.Buffered(1)
    return pl.pallas_call(
        _merge_out_kernel,
        grid=(s // tm,),
        in_specs=[row(D_MODEL), row(BRANCH_WIDTH), row(BRANCH_WIDTH), row(BRANCH_WIDTH),
                  _seg_spec(tm, "az"), _seg_spec(tm, "bz"), _seg_spec(tm, "cz"),
                  gate(0), gate(1), gate(2),
                  pl.BlockSpec(w_branch.shape, lambda i: (0, 0, 0), pipeline_mode=single),
                  pl.BlockSpec(w_out.shape, lambda i: (0, 0), pipeline_mode=single)],
        out_specs=row(D_MODEL),
        out_shape=jax.ShapeDtypeStruct((s, D_MODEL), F32),
        compiler_params=_cparams("parallel"),
        name="merge_out",
    )(x, o_a, o_b, o_c, proj, proj, proj, proj, proj, proj, w_branch, w_out)


def _regroup_tables():
    names = ("aq", "ak", "av", "az", "iq", "ik", "iw", "bcq", "bckv", "bkpe", "bz", "cq", "cz", "gates")
    src_off = dict(zip(names, [0] + IN_OFFSETS))
    src_off["ikw"] = src_off["ik"]
    valid_cols = dict(ikw=IDX_DIM + IDX_HEADS, bkpe=B_ROPE)
    tile, shift, valid = [], [], []
    for name, units in SEG_UNITS.items():
        for u in range(units):
            col = src_off[name] + u * LANES
            tile.append(col // LANES)
            shift.append(col % LANES)
            valid.append(valid_cols.get(name, LANES))
    return np.asarray(tile, np.int32), np.asarray(shift, np.int32), np.asarray(valid, np.int32)


REGROUP_TILE, REGROUP_SHIFT, REGROUP_VALID = _regroup_tables()


def _regroup_kernel(tile_ref, shift_ref, valid_ref, lo_ref, hi_ref, o_ref):
    b = pl.program_id(0)
    lane = lax.broadcasted_iota(I32, o_ref.shape, 1)
    keep = lane < valid_ref[b]
    for r in sorted(set(int(v) for v in REGROUP_SHIFT)):
        @pl.when(shift_ref[b] == r)
        def _(r=r):
            v = lo_ref[...]
            if r:
                v = jnp.where(lane < LANES - r, pltpu.roll(v, LANES - r, 1), pltpu.roll(hi_ref[...], LANES - r, 1))
            o_ref[...] = jnp.where(keep, v, 0.0).astype(BF16)


def _regroup_w_in(w_in):
    d, width = w_in.shape
    last = (width - 1) // LANES
    grid_spec = pltpu.PrefetchScalarGridSpec(
        num_scalar_prefetch=3,
        grid=(PROJ_UNITS,),
        in_specs=[
            pl.BlockSpec((d, LANES), lambda b, tile, *_: (0, tile[b])),
            pl.BlockSpec((d, LANES), lambda b, tile, *_: (0, jnp.minimum(tile[b] + 1, last))),
        ],
        out_specs=pl.BlockSpec((d, LANES), lambda b, *_: (0, b)),
    )
    return pl.pallas_call(
        _regroup_kernel,
        grid_spec=grid_spec,
        out_shape=jax.ShapeDtypeStruct((d, PROJ_WIDTH), BF16),
        compiler_params=_cparams("arbitrary"),
        name="regroup_w_in",
    )(jnp.asarray(REGROUP_TILE), jnp.asarray(REGROUP_SHIFT), jnp.asarray(REGROUP_VALID), w_in, w_in)


def _pad_heads(w, n_heads, width, pad_to):
    r = w.shape[0]
    w = w.reshape(r, n_heads, width)
    w = jnp.pad(w, ((0, 0), (0, 0), (0, pad_to - width)))
    return w.reshape(r, n_heads * pad_to)


def kernel(x, mem, positions, rel_bias, norm_g, mem_norm_g, w_in, a_q_norm_g, a_k_norm_g, b_q_lat_norm_g,
           b_kv_lat_norm_g, w_b_uq, w_b_ukv, b_q_norm_g, b_k_norm_g, w_mem_kv, c_q_norm_g, c_k_norm_g,
           w_branch, w_out):
    bsz, seq, d = x.shape
    assert d == D_MODEL and norm_g.shape[0] == 1
    topk = min(TOPK_MAX, seq // 4)
    tq_a = 256
    tq_b = 512
    tm_in = min(1024, seq)

    inv_freq = 1.0 / (ROPE_THETA ** (jnp.arange(0, B_ROPE, 2, dtype=F32) / B_ROPE))
    zeros_half = jnp.zeros((B_ROPE,), F32)
    gq_pad = jnp.concatenate([b_q_norm_g[0], zeros_half])[None, :]
    gk_pad = jnp.concatenate([b_k_norm_g[0], zeros_half])[None, :]
    lut_t = ((rel_bias[jnp.asarray(T5_TABLE)] - rel_bias[REL_BUCKETS - 1]).T * LOG2E).astype(F32)

    w_cat = _regroup_w_in(w_in[0])
    wuq = _pad_heads(w_b_uq[0], B_HEADS, B_QK_DIM, B_QK_PAD).astype(BF16)
    wukv = w_b_ukv[0].reshape(B_KV_RANK, B_HEADS, B_NOPE + B_V)
    wuk = wukv[:, :, :B_NOPE].reshape(B_KV_RANK, B_HEADS * B_NOPE).astype(BF16)
    wuvt = jnp.transpose(wukv[:, :, B_NOPE:], (1, 2, 0)).astype(BF16)
    wbr = w_branch[0].astype(BF16)
    wout = w_out[0].astype(BF16)

    outs = []
    for b in range(bsz):
        pos = positions[b]
        ang = pos.astype(F32)[:, None] * inv_freq
        cos, sin = jnp.cos(ang), jnp.sin(ang)
        zpad = jnp.zeros((seq, LANES - B_ROPE), F32)
        cos128 = jnp.concatenate([cos, cos, zpad], axis=1)
        sin128 = jnp.concatenate([-sin, sin, zpad], axis=1)

        mem_kv = _mem_kv(mem[b], mem_norm_g[0][None, :], w_mem_kv[0], c_k_norm_g[0][None, :])
        proj = _in_proj(x[b], norm_g[0][None, :], w_cat, tm=tm_in, tn=512)
        aqn, akn, kab, wt, avt, qb, kb, vbt, cqn = _prep(
            proj, cos128, sin128, a_q_norm_g[0][None, :], a_k_norm_g[0][None, :],
            b_q_lat_norm_g[0][None, :], b_kv_lat_norm_g[0][None, :], gq_pad, gk_pad,
            c_q_norm_g[0][None, :], wuq, wuk, wuvt)
        o_b = _attn_b(qb, kb, vbt, tq=tq_b)
        qmin = jnp.min(pos.reshape(seq // tq_a, tq_a), axis=1)
        kmax = jnp.max(pos.reshape(seq // KV_TILE, KV_TILE), axis=1)
        o_a = _attn_a(proj, wt, kab, aqn, akn, avt, pos[None, :], pos[:, None],
                      lut_t, qmin, kmax, tq=tq_a, topk=topk)
        o_c = _attn_c(cqn, mem_kv, tm=512)
        outs.append(_merge_out(x[b], o_a, o_b, o_c, proj, wbr, wout, tm=256))
    return jnp.stack(outs, axis=0)
```

```python
import functools
import math

import numpy as np
import jax
import jax.numpy as jnp
from jax import lax
from jax.experimental import pallas as pl
from jax.experimental.pallas import tpu as pltpu

F32 = jnp.float32
BF16 = jnp.bfloat16
I32 = jnp.int32

D_MODEL = 2048
BRANCH_WIDTH = 1024
N_BRANCH = 3
A_HEADS = 8
A_KV_HEADS = 2
A_HEAD_DIM = 128
A_WIDTH = A_HEADS * A_HEAD_DIM
A_KV_WIDTH = A_KV_HEADS * A_HEAD_DIM
A_REP = A_HEADS // A_KV_HEADS
IDX_HEADS = 16
IDX_DIM = 64
TOPK_MAX = 256
IDX_SCALE = (IDX_DIM ** -0.5) * (IDX_HEADS ** -0.5)
A_SCALE = A_HEAD_DIM ** -0.5
B_HEADS = 8
B_Q_RANK = 512
B_KV_RANK = 256
B_NOPE = 128
B_ROPE = 64
B_QK_DIM = B_NOPE + B_ROPE
B_V = 128
B_WIDTH = B_HEADS * B_V
B_SCALE = B_QK_DIM ** -0.5
ROPE_THETA = 10000.0
C_HEADS = 4
C_HEAD_DIM = 256
C_WIDTH = C_HEADS * C_HEAD_DIM
C_SCALE = C_HEAD_DIM ** -0.5
REL_BUCKETS = 32
REL_MAX_DIST = 128
EPS = 1e-6

IN_SIZES = (A_WIDTH, A_KV_WIDTH, A_KV_WIDTH, A_WIDTH, IDX_HEADS * IDX_DIM, IDX_DIM, IDX_HEADS,
            B_Q_RANK, B_KV_RANK, B_ROPE, B_WIDTH, C_WIDTH, C_WIDTH, N_BRANCH * D_MODEL)
IN_OFFSETS = [int(o) for o in np.cumsum(IN_SIZES)[:-1]]

LANES = 128
SUBLANES = 8
B_QK_PAD = 2 * LANES
VMEM_LIMIT_BYTES = 56 * 1024 * 1024
MASK_VALUE = -1e30
LOG2E = math.log2(math.e)
SHIFT_MARGIN = 1.02
MAX_SHIFT_SPAN = 100.0
KNORM_ROWS = 16
DV = 128
ONES_ROWS = 16
DV_AUG = DV + ONES_ROWS
assert DV == A_HEAD_DIM == B_V
KV_TILE = 512
B_HEADS_PER_STEP = 2

SEG_UNITS = dict(gates=48, aq=8, az=8, iq=8, bz=8, cq=8, cz=8, bcq=4, ak=2, av=2, bckv=2, ikw=1, bkpe=1)
SEG_START = {}
_u = 0
for _name, _w in SEG_UNITS.items():
    SEG_START[_name] = _u
    _u += _w
PROJ_UNITS = _u
PROJ_WIDTH = PROJ_UNITS * LANES


def _t5_bucket_table():
    d = np.arange(LANES)
    max_exact = REL_BUCKETS // 2
    nf = np.maximum(d, 1).astype(np.float64)
    large = max_exact + (np.log(nf / max_exact) / math.log(REL_MAX_DIST / max_exact)
                         * (REL_BUCKETS - max_exact)).astype(np.int64)
    large = np.minimum(large, REL_BUCKETS - 1)
    table = np.where(d < max_exact, d, large).astype(np.int32)
    far = int(np.min(np.nonzero(table == REL_BUCKETS - 1)[0]))
    assert np.all(table[far:] == REL_BUCKETS - 1)
    return table, far


T5_TABLE, T5_FAR = _t5_bucket_table()


def _cparams(*sem):
    return pltpu.CompilerParams(dimension_semantics=sem, vmem_limit_bytes=VMEM_LIMIT_BYTES)


def _dot_t(a, b):
    return lax.dot_general(a, b, (((1,), (1,)), ((), ())), preferred_element_type=F32)


def _rms_scale(ss, n):
    return lax.rsqrt(ss * (1.0 / n) + EPS)


def _mem_kv_kernel(mem_ref, g_ref, w_ref, ckn_ref, o_ref, h_ref):
    j = pl.program_id(0)

    @pl.when(j == 0)
    def _():
        m = mem_ref[...]
        r = _rms_scale(jnp.sum(m * m, axis=-1, keepdims=True), D_MODEL)
        h_ref[...] = (m * r * g_ref[...]).astype(BF16)

    y = jnp.dot(h_ref[...], w_ref[...].astype(BF16), preferred_element_type=F32)

    @pl.when(j < C_HEADS)
    def _():
        r = _rms_scale(jnp.sum(y * y, axis=-1, keepdims=True), C_HEAD_DIM)
        o_ref[...] = (y * r * ckn_ref[...]).astype(BF16)

    @pl.when(j >= C_HEADS)
    def _():
        o_ref[...] = y.astype(BF16)


def _mem_kv(mem, mem_g, w_mem_kv, ck_g):
    n_mem = mem.shape[0]
    return pl.pallas_call(
        _mem_kv_kernel,
        grid=(2 * C_HEADS,),
        in_specs=[
            pl.BlockSpec((n_mem, D_MODEL), lambda j: (0, 0)),
            pl.BlockSpec((1, D_MODEL), lambda j: (0, 0)),
            pl.BlockSpec((D_MODEL, C_HEAD_DIM), lambda j: (0, j)),
            pl.BlockSpec((1, C_HEAD_DIM), lambda j: (0, 0)),
        ],
        out_specs=pl.BlockSpec((n_mem, C_HEAD_DIM), lambda j: (0, j)),
        out_shape=jax.ShapeDtypeStruct((n_mem, 2 * C_WIDTH), BF16),
        scratch_shapes=[pltpu.VMEM((n_mem, D_MODEL), BF16)],
        compiler_params=_cparams("arbitrary"),
        name="mem_kv",
    )(mem, mem_g, w_mem_kv, ck_g)


def _in_proj_kernel(x_ref, g_ref, w_ref, o_ref, h_ref, *, row_chunk):
    @pl.when(pl.program_id(1) == 0)
    def _():
        g = g_ref[...]
        for c in range(x_ref.shape[0] // row_chunk):
            rows = pl.ds(c * row_chunk, row_chunk)
            x = x_ref[rows, :]
            r = _rms_scale(jnp.sum(x * x, axis=-1, keepdims=True), D_MODEL)
            h_ref[rows, :] = (x * r * g).astype(BF16)

    o_ref[...] = _dot_t(h_ref[...], w_ref[...]).astype(o_ref.dtype)


def _in_proj(x, norm_g, w_cat, *, tm, tn):
    s = x.shape[0]
    return pl.pallas_call(
        functools.partial(_in_proj_kernel, row_chunk=min(tm, 256)),
        grid=(s // tm, PROJ_WIDTH // tn),
        in_specs=[
            pl.BlockSpec((tm, D_MODEL), lambda i, j: (i, 0)),
            pl.BlockSpec((1, D_MODEL), lambda i, j: (0, 0)),
            pl.BlockSpec((tn, D_MODEL), lambda i, j: (j, 0)),
        ],
        out_specs=pl.BlockSpec((tm, tn), lambda i, j: (i, j)),
        out_shape=jax.ShapeDtypeStruct((s, PROJ_WIDTH), BF16),
        scratch_shapes=[pltpu.VMEM((tm, D_MODEL), BF16)],
        compiler_params=_cparams("parallel", "arbitrary"),
        name="in_proj",
    )(x, norm_g, w_cat)


def _rope128(x, cos, sin):
    half = B_ROPE // 2
    lane = lax.broadcasted_iota(I32, x.shape, 1)
    partner = jnp.where(lane < half, pltpu.roll(x, LANES - half, 1), pltpu.roll(x, half, 1))
    return x * cos + partner * sin


def _prep_kernel(aq_ref, ak_ref, av_ref, ikw_ref, bcq_ref, bckv_ref, bkpe_ref, cq_ref,
                 cos_ref, sin_ref, aqg_ref, akg_ref, bqlg_ref, bkvlg_ref, bqg_ref, bkg_ref, cqg_ref,
                 wuq_ref, wuk_ref, wuvt_ref,
                 aqn_ref, akn_ref, kab_ref, wt_ref, avt_ref, qb_ref, kb_ref, vbt_ref, cqn_ref, kn_ref):
    cos = cos_ref[...]
    sin = sin_ref[...]

    @pl.when(pl.program_id(0) == 0)
    def _():
        kn_ref[...] = jnp.zeros(kn_ref.shape, F32)

    def note_key_norm(row, sq_norms):
        top = jnp.max(sq_norms, axis=0, keepdims=True)
        kn_ref[row:row + 1, :] = jnp.maximum(kn_ref[row:row + 1, :], top)

    eye = jnp.where(lax.broadcasted_iota(I32, (LANES, LANES), 0) == lax.broadcasted_iota(I32, (LANES, LANES), 1),
                    1.0, 0.0).astype(BF16)

    aqg = aqg_ref[...] * (A_SCALE * LOG2E)
    for h in range(A_HEADS):
        cols = slice(h * A_HEAD_DIM, (h + 1) * A_HEAD_DIM)
        v = aq_ref[:, cols].astype(F32)
        r = _rms_scale(jnp.sum(v * v, axis=-1, keepdims=True), A_HEAD_DIM)
        aqn_ref[:, cols] = (v * r * aqg).astype(BF16)
    akg = akg_ref[...]
    for h in range(A_KV_HEADS):
        cols = slice(h * A_HEAD_DIM, (h + 1) * A_HEAD_DIM)
        v = ak_ref[:, cols].astype(F32)
        r = _rms_scale(jnp.sum(v * v, axis=-1, keepdims=True), A_HEAD_DIM)
        kn = v * r * akg
        akn_ref[:, cols] = kn.astype(BF16)
        note_key_norm(B_HEADS + h, jnp.sum(kn * kn, axis=-1, keepdims=True))
        avt_ref[h, 0, 0:DV] = _dot_t(eye, av_ref[:, cols]).astype(BF16)
        avt_ref[h, 0, DV:DV_AUG] = jnp.ones((ONES_ROWS, avt_ref.shape[-1]), BF16)

    ikw = ikw_ref[...]
    lane = lax.broadcasted_iota(I32, ikw.shape, 1)
    zero = jnp.zeros_like(ikw)
    kab_ref[:, 0:LANES] = jnp.where(lane < IDX_DIM, ikw, zero)
    kab_ref[:, LANES:2 * LANES] = jnp.where(lane >= IDX_DIM, pltpu.roll(ikw.astype(F32), IDX_DIM, 1).astype(BF16), zero)
    wt_ref[...] = _dot_t(eye, ikw) * IDX_SCALE

    cq_lat = bcq_ref[...].astype(F32)
    r = _rms_scale(jnp.sum(cq_lat * cq_lat, axis=-1, keepdims=True), B_Q_RANK)
    qlat = (cq_lat * r * bqlg_ref[...]).astype(BF16)
    bqg = bqg_ref[...] * (B_SCALE * LOG2E)
    for h in range(B_HEADS):
        cols = slice(h * B_QK_PAD, (h + 1) * B_QK_PAD)
        qh = jnp.dot(qlat, wuq_ref[:, cols], preferred_element_type=F32)
        r = _rms_scale(jnp.sum(qh * qh, axis=-1, keepdims=True), B_QK_DIM)
        qn = qh * r * bqg
        qb_ref[:, h * B_QK_PAD:h * B_QK_PAD + LANES] = qn[:, :LANES].astype(BF16)
        qb_ref[:, h * B_QK_PAD + LANES:(h + 1) * B_QK_PAD] = _rope128(qn[:, LANES:], cos, sin).astype(BF16)

    ckv = bckv_ref[...].astype(F32)
    r = _rms_scale(jnp.sum(ckv * ckv, axis=-1, keepdims=True), B_KV_RANK)
    kvlat = (ckv * r * bkvlg_ref[...]).astype(BF16)
    kpe = bkpe_ref[...].astype(F32)
    ss_pe = jnp.sum(kpe * kpe, axis=-1, keepdims=True)
    bkg = bkg_ref[...]
    kpe_rot = _rope128(kpe * bkg[:, LANES:], cos, sin)
    for h in range(B_HEADS):
        kn = jnp.dot(kvlat, wuk_ref[:, h * B_NOPE:(h + 1) * B_NOPE], preferred_element_type=F32)
        r = _rms_scale(jnp.sum(kn * kn, axis=-1, keepdims=True) + ss_pe, B_QK_DIM)
        k_nope = kn * r * bkg[:, :LANES]
        k_pe = kpe_rot * r
        kb_ref[:, h * B_QK_PAD:h * B_QK_PAD + LANES] = k_nope.astype(BF16)
        kb_ref[:, h * B_QK_PAD + LANES:(h + 1) * B_QK_PAD] = k_pe.astype(BF16)
        note_key_norm(h, jnp.sum(k_nope * k_nope, axis=-1, keepdims=True)
                      + jnp.sum(k_pe * k_pe, axis=-1, keepdims=True))
        vbt_ref[h, 0, 0:DV] = _dot_t(wuvt_ref[h], kvlat).astype(BF16)
        vbt_ref[h, 0, DV:DV_AUG] = jnp.ones((ONES_ROWS, vbt_ref.shape[-1]), BF16)

    cqg = cqg_ref[...] * C_SCALE
    for h in range(C_HEADS):
        cols = slice(h * C_HEAD_DIM, (h + 1) * C_HEAD_DIM)
        v = cq_ref[:, cols].astype(F32)
        r = _rms_scale(jnp.sum(v * v, axis=-1, keepdims=True), C_HEAD_DIM)
        cqn_ref[:, cols] = (v * r * cqg).astype(BF16)


def _seg_spec(tm, name):
    units = SEG_UNITS[name]
    blk = SEG_START[name] // units
    return pl.BlockSpec((tm, units * LANES), lambda i: (i, blk))


def _full_spec(shape):
    return pl.BlockSpec(shape, lambda i: (0,) * len(shape))


def _prep(proj, cos128, sin128, aqg, akg, bqlg, bkvlg, bqg, bkg, cqg, wuq, wuk, wuvt):
    s = proj.shape[0]
    tm = KV_TILE
    nt = s // tm
    row = lambda w: pl.BlockSpec((tm, w), lambda i: (i, 0))
    tiles_t = lambda n: pl.BlockSpec((n, 1, DV_AUG, tm), lambda i: (0, i, 0, 0))
    small = [aqg, akg, bqlg, bkvlg, bqg, bkg, cqg, wuq, wuk, wuvt]
    out_specs = [row(A_WIDTH), row(A_KV_WIDTH), row(2 * LANES),
                 pl.BlockSpec((LANES, tm), lambda i: (0, i)), tiles_t(A_KV_HEADS),
                 row(B_HEADS * B_QK_PAD), row(B_HEADS * B_QK_PAD), tiles_t(B_HEADS), row(C_WIDTH),
                 _full_spec((KNORM_ROWS, LANES))]
    out_shape = [
        jax.ShapeDtypeStruct((s, A_WIDTH), BF16),
        jax.ShapeDtypeStruct((s, A_KV_WIDTH), BF16),
        jax.ShapeDtypeStruct((s, 2 * LANES), BF16),
        jax.ShapeDtypeStruct((LANES, s), F32),
        jax.ShapeDtypeStruct((A_KV_HEADS, nt, DV_AUG, tm), BF16),
        jax.ShapeDtypeStruct((s, B_HEADS * B_QK_PAD), BF16),
        jax.ShapeDtypeStruct((s, B_HEADS * B_QK_PAD), BF16),
        jax.ShapeDtypeStruct((B_HEADS, nt, DV_AUG, tm), BF16),
        jax.ShapeDtypeStruct((s, C_WIDTH), BF16),
        jax.ShapeDtypeStruct((KNORM_ROWS, LANES), F32),
    ]
    return pl.pallas_call(
        _prep_kernel,
        grid=(nt,),
        in_specs=[_seg_spec(tm, n) for n in ("aq", "ak", "av", "ikw", "bcq", "bckv", "bkpe", "cq")]
        + [row(LANES), row(LANES)] + [_full_spec(a.shape) for a in small],
        out_specs=out_specs,
        out_shape=out_shape,
        compiler_params=_cparams("arbitrary"),
        name="prep",
    )(*([proj] * 8), cos128, sin128, *small)


def _reduce_keys(x, op):
    tk, tq = x.shape
    n = tk // SUBLANES
    assert n & (n - 1) == 0
    t = x.reshape(n, SUBLANES, tq)
    while n > 1:
        n //= 2
        t = op(t[:n], t[n:2 * n])
    red = jnp.max if op is jnp.maximum else jnp.sum
    return red(t[0], axis=0, keepdims=True)


def _softmax_step_t(st, vt, m_ref, acc_ref):
    m_prev = m_ref[...]
    m_new = jnp.maximum(m_prev, _reduce_keys(st, jnp.maximum))
    alpha = jnp.exp2(m_prev - m_new)
    p = jnp.exp2(st - m_new).astype(BF16)
    acc_ref[...] = alpha * acc_ref[...] + jnp.dot(vt, p, preferred_element_type=F32)
    m_ref[...] = m_new


def _fixed_shift_step_t(st, vt, shift, acc_ref):
    p = jnp.exp2(st - shift).astype(BF16)
    acc_ref[...] += jnp.dot(vt, p, preferred_element_type=F32)


def _normalised_output(acc):
    return (acc[0:DV] / acc[DV:DV + 1]).T


def _logit_bound(q, kmax2):
    qf = q.astype(F32)
    ones = jnp.ones((SUBLANES, q.shape[1]), BF16)
    qn2 = _dot_t(ones, (qf * qf).astype(BF16))[0:1]
    return jnp.sqrt(qn2 * kmax2) * SHIFT_MARGIN


def _causal_ok_t(i, j, tq, tk):
    key = j * tk + lax.broadcasted_iota(I32, (tk, tq), 0)
    qry = i * tq + lax.broadcasted_iota(I32, (tk, tq), 1)
    return key <= qry


def _init_softmax_state(m_ref, acc_ref):
    m_ref[...] = jnp.full(m_ref.shape, MASK_VALUE, F32)
    acc_ref[...] = jnp.zeros(acc_ref.shape, F32)


def _pingpong_tiles(n_full, logits, consume, buf_a, buf_b):
    logits(0, buf_a)

    def pair(p, c):
        j = 2 * p
        logits(j + 1, buf_b)
        consume(j, buf_a, False)
        logits(j + 2, buf_a)
        consume(j + 1, buf_b, False)
        return c

    lax.fori_loop(0, n_full // 2, pair, 0)
    odd = n_full % 2 == 1

    @pl.when(odd)
    def _():
        logits(n_full, buf_b)
        consume(n_full - 1, buf_a, False)
        consume(n_full, buf_b, True)

    @pl.when(jnp.logical_not(odd))
    def _():
        consume(n_full, buf_a, True)


def _attn_b_kernel(q_ref, k_ref, vt_ref, kn_ref, o_ref, sa_ref, sb_ref, m_ref, acc_ref, *, tq, tk):
    hp = pl.program_id(0)
    i = pl.program_id(1)
    _init_softmax_state(m_ref, acc_ref)
    j_diag = (i * tq) // tk
    heads = range(B_HEADS_PER_STEP)
    shifts = [_logit_bound(q_ref[:, hh * B_QK_PAD:(hh + 1) * B_QK_PAD],
                           kn_ref[pl.ds(hp * B_HEADS_PER_STEP + hh, 1), 0:1]) for hh in heads]
    fixed_ok = jnp.max(2.0 * functools.reduce(jnp.maximum, shifts)) <= MAX_SHIFT_SPAN

    def logits(j, buf):
        rows = pl.ds(pl.multiple_of(j * tk, tk), tk)
        for hh in heads:
            cols = slice(hh * B_QK_PAD, (hh + 1) * B_QK_PAD)
            buf[hh] = _dot_t(k_ref[rows, cols], q_ref[:, cols])

    def consume(j, buf, masked, fixed):
        for hh in heads:
            st = buf[hh]
            if masked:
                st = jnp.where(_causal_ok_t(i, j, tq, tk), st, MASK_VALUE)
            if fixed:
                _fixed_shift_step_t(st, vt_ref[hh, j], shifts[hh], acc_ref.at[hh])
            else:
                _softmax_step_t(st, vt_ref[hh, j], m_ref.at[hh], acc_ref.at[hh])

    @pl.when(fixed_ok)
    def _():
        _pingpong_tiles(j_diag, logits, functools.partial(consume, fixed=True), sa_ref, sb_ref)

    @pl.when(jnp.logical_not(fixed_ok))
    def _():
        _pingpong_tiles(j_diag, logits, functools.partial(consume, fixed=False), sa_ref, sb_ref)

    for hh in heads:
        o_ref[:, hh * B_V:(hh + 1) * B_V] = _normalised_output(acc_ref[hh]).astype(o_ref.dtype)


def _attn_b(qb, kb, vbt, knorm2, *, tq):
    s = qb.shape[0]
    tk = KV_TILE
    hps = B_HEADS_PER_STEP
    assert tk % tq == 0 or tq % tk == 0
    assert tq <= tk
    return pl.pallas_call(
        functools.partial(_attn_b_kernel, tq=tq, tk=tk),
        grid=(B_HEADS // hps, s // tq),
        in_specs=[
            pl.BlockSpec((tq, hps * B_QK_PAD), lambda h, i: (i, h)),
            pl.BlockSpec((s, hps * B_QK_PAD), lambda h, i: (0, h)),
            pl.BlockSpec((hps, s // tk, DV_AUG, tk), lambda h, i: (h, 0, 0, 0)),
            pl.BlockSpec((KNORM_ROWS, LANES), lambda h, i: (0, 0)),
        ],
        out_specs=pl.BlockSpec((tq, hps * B_V), lambda h, i: (i, h)),
        out_shape=jax.ShapeDtypeStruct((s, B_WIDTH), BF16),
        scratch_shapes=[pltpu.VMEM((hps, tk, tq), F32), pltpu.VMEM((hps, tk, tq), F32),
                        pltpu.VMEM((hps, 1, tq), F32), pltpu.VMEM((hps, DV_AUG, tq), F32)],
        compiler_params=_cparams("parallel", "arbitrary"),
        name="attn_b",
    )(qb, kb, vbt, knorm2)


def _order_key(x):
    b = lax.bitcast_convert_type(x + 0.0, I32)
    return b ^ (lax.shift_right_arithmetic(b, 31) & 0x7FFFFFFF)


BISECT_WARMUP = 12
BISECT_GROUP = 4
assert (32 - BISECT_WARMUP) % BISECT_GROUP == 0


def _count_hits(hit):
    tk, tq = hit.shape
    n = tk // SUBLANES
    t = jnp.where(hit, 1, 0).reshape(n, SUBLANES, tq)
    while n > 1:
        n //= 2
        t = t[:n] + t[n:2 * n]
    return t[0]


def _attn_a_kernel(qmin_ref, kmax_ref,
                   iq_ref, wt_ref, kab_ref, aqn_ref, akn_ref, avt_ref, pq_ref, pk_ref, lut_ref, kn_ref,
                   o_ref,
                   key_ref, sa_ref, sb_ref, m_ref, acc_ref, *, tq, tk, topk):
    i = pl.program_id(0)
    j_diag = (i * tq) // tk
    n_tiles = j_diag + 1
    int_min = jnp.int32(-2 ** 31)

    def score_tile(j, masked):
        rows = pl.ds(pl.multiple_of(j * tk, tk), tk)
        k_lo = kab_ref[rows, 0:LANES]
        k_hi = kab_ref[rows, LANES:2 * LANES]
        acc = jnp.zeros((tk, tq), F32)
        for p in range(IDX_HEADS // 2):
            rhs = iq_ref[:, p * LANES:(p + 1) * LANES]
            w_row = IDX_DIM + 2 * p
            acc = acc + jnp.maximum(_dot_t(k_lo, rhs), 0.0) * wt_ref[w_row:w_row + 1, :]
            acc = acc + jnp.maximum(_dot_t(k_hi, rhs), 0.0) * wt_ref[w_row + 1:w_row + 2, :]
        if masked:
            acc = jnp.where(_causal_ok_t(i, j, tq, tk), acc, -jnp.inf)
        key_ref[j] = _order_key(acc)

    def score_body(j, c):
        score_tile(j, False)
        return c

    lax.fori_loop(0, j_diag, score_body, 0)
    score_tile(j_diag, True)

    def count_queries(pred_fn):
        def body(j, cnt):
            return cnt + _count_hits(pred_fn(key_ref[j], j))
        cnt = lax.fori_loop(0, n_tiles, body, jnp.zeros((SUBLANES, tq), I32))
        return jnp.sum(cnt, axis=0, keepdims=True)

    def bit_step(it, t, settled):
        cand = t + lax.shift_left(jnp.int32(1), 31 - it)
        cnt = count_queries(lambda x, j: x >= cand)
        t = jnp.where((settled == 0) & (cnt >= topk), cand, t)
        return t, jnp.where(cnt == topk, 1, settled)

    state = lax.fori_loop(0, BISECT_WARMUP, lambda it, c: bit_step(it, *c),
                          (jnp.full((1, tq), int_min, I32), jnp.zeros((1, tq), I32)))

    def group_cond(c):
        g, _, settled = c
        return (g < (32 - BISECT_WARMUP) // BISECT_GROUP) & (jnp.min(settled) == 0)

    def group_body(c):
        g, t, settled = c
        for u in range(BISECT_GROUP):
            t, settled = bit_step(BISECT_WARMUP + g * BISECT_GROUP + u, t, settled)
        return g + 1, t, settled

    _, thr, settled = lax.while_loop(group_cond, group_body, (jnp.int32(0),) + state)

    def demote_excess_ties():
        cnt_gt = count_queries(lambda x, j: x > thr)
        need = topk - cnt_gt
        kpos = lax.broadcasted_iota(I32, (tk, tq), 0)

        def pos_body(it, cut):
            cand = cut + lax.shift_left(jnp.int32(1), 30 - it)
            cnt = count_queries(lambda x, j: (x == thr) & (j * tk + kpos < cand))
            return jnp.where(cnt < need, cand, cut)

        cut = lax.fori_loop(0, 31, pos_body, jnp.zeros((1, tq), I32))

        def demote_body(j, c):
            x = key_ref[j]
            drop = (x == thr) & (j * tk + kpos > cut) & (thr > int_min)
            key_ref[j] = jnp.where(drop, x - 1, x)
            return c

        lax.fori_loop(0, n_tiles, demote_body, 0)

    @pl.when(jnp.min(settled) == 0)
    def _():
        cnt_ge = count_queries(lambda x, j: x >= thr)
        pl.when(jnp.max(cnt_ge) > topk)(demote_excess_ties)

    _init_softmax_state(m_ref, acc_ref)
    pq = pq_ref[...]
    lut = lut_ref[...]
    bias_hi = jnp.max(lut, axis=1, keepdims=True)
    bias_span = bias_hi - jnp.min(lut, axis=1, keepdims=True)
    bounds = [_logit_bound(aqn_ref[:, h * A_HEAD_DIM:(h + 1) * A_HEAD_DIM],
                           kn_ref[B_HEADS + h // A_REP:B_HEADS + h // A_REP + 1, 0:1]) for h in range(A_HEADS)]
    shifts = [bounds[h] + bias_hi[h:h + 1] for h in range(A_HEADS)]
    spans = [2.0 * bounds[h] + bias_span[h:h + 1] for h in range(A_HEADS)]
    fixed_ok = jnp.max(functools.reduce(jnp.maximum, spans)) <= MAX_SHIFT_SPAN

    def attend(j, masked, near, fixed):
        rows = pl.ds(pl.multiple_of(j * tk, tk), tk)
        sel = key_ref[j] >= thr
        if masked:
            sel = sel & _causal_ok_t(i, j, tq, tk)
        mask_bias = jnp.where(sel, 0.0, MASK_VALUE)
        if near:
            dist = jnp.clip(pq - pk_ref[rows, :], 0, LANES - 1)

        def logits(h, buf):
            g = h // A_REP
            buf[...] = _dot_t(akn_ref[rows, g * A_HEAD_DIM:(g + 1) * A_HEAD_DIM],
                              aqn_ref[:, h * A_HEAD_DIM:(h + 1) * A_HEAD_DIM])

        bufs = (sa_ref, sb_ref)
        logits(0, bufs[0])
        for h in range(A_HEADS):
            if h + 1 < A_HEADS:
                logits(h + 1, bufs[(h + 1) % 2])
            st = bufs[h % 2][...] + mask_bias
            if near:
                table = jnp.broadcast_to(lut_ref[h:h + 1, :], (tk, LANES))
                st = st + jnp.concatenate(
                    [jnp.take_along_axis(table, dist[:, c * LANES:(c + 1) * LANES], axis=1,
                                         mode="promise_in_bounds")
                     for c in range(tq // LANES)], axis=1)
            if fixed:
                _fixed_shift_step_t(st, avt_ref[h // A_REP, j], shifts[h], acc_ref.at[h])
            else:
                _softmax_step_t(st, avt_ref[h // A_REP, j], m_ref.at[h], acc_ref.at[h])

    def attend_dyn(j, masked, fixed):
        far = qmin_ref[i] - kmax_ref[j] >= T5_FAR

        @pl.when(far)
        def _():
            attend(j, masked, False, fixed)

        @pl.when(jnp.logical_not(far))
        def _():
            attend(j, masked, True, fixed)

    def attend_all(fixed):
        def attend_body(j, c):
            attend_dyn(j, False, fixed)
            return c

        lax.fori_loop(0, j_diag, attend_body, 0)
        attend_dyn(j_diag, True, fixed)

    pl.when(fixed_ok)(functools.partial(attend_all, True))
    pl.when(jnp.logical_not(fixed_ok))(functools.partial(attend_all, False))

    for h in range(A_HEADS):
        q_cols = slice(h * A_HEAD_DIM, (h + 1) * A_HEAD_DIM)
        o_ref[:, q_cols] = _normalised_output(acc_ref[h]).astype(o_ref.dtype)


def _attn_a(proj, wt, kab, aqn, akn, avt, pos_row, pos_col, lut_t, knorm2, qmin, kmax, *, tq, topk):
    s = proj.shape[0]
    tk = KV_TILE
    assert tk % tq == 0
    iq_blk = SEG_START["iq"] // SEG_UNITS["iq"]
    grid_spec = pltpu.PrefetchScalarGridSpec(
        num_scalar_prefetch=2,
        grid=(s // tq,),
        in_specs=[
            pl.BlockSpec((tq, IDX_HEADS * IDX_DIM), lambda i, *_: (i, iq_blk)),
            pl.BlockSpec((LANES, tq), lambda i, *_: (0, i)),
            pl.BlockSpec((s, 2 * LANES), lambda i, *_: (0, 0)),
            pl.BlockSpec((tq, A_WIDTH), lambda i, *_: (i, 0)),
            pl.BlockSpec((s, A_KV_WIDTH), lambda i, *_: (0, 0)),
            pl.BlockSpec((A_KV_HEADS, s // tk, DV_AUG, tk), lambda i, *_: (0, 0, 0, 0)),
            pl.BlockSpec((1, tq), lambda i, *_: (0, i)),
            pl.BlockSpec((s, 1), lambda i, *_: (0, 0)),
            pl.BlockSpec((A_HEADS, LANES), lambda i, *_: (0, 0)),
            pl.BlockSpec((KNORM_ROWS, LANES), lambda i, *_: (0, 0)),
        ],
        out_specs=pl.BlockSpec((tq, A_WIDTH), lambda i, *_: (i, 0)),
        scratch_shapes=[
            pltpu.VMEM((s // tk, tk, tq), I32),
            pltpu.VMEM((tk, tq), F32),
            pltpu.VMEM((tk, tq), F32),
            pltpu.VMEM((A_HEADS, 1, tq), F32),
            pltpu.VMEM((A_HEADS, DV_AUG, tq), F32),
        ],
    )
    return pl.pallas_call(
        functools.partial(_attn_a_kernel, tq=tq, tk=tk, topk=topk),
        grid_spec=grid_spec,
        out_shape=jax.ShapeDtypeStruct((s, A_WIDTH), BF16),
        compiler_params=_cparams("arbitrary"),
        name="attn_a",
    )(qmin, kmax, proj, wt, kab, aqn, akn, avt, pos_row, pos_col, lut_t, knorm2)


def _attn_c_kernel(q_ref, kv_ref, o_ref):
    for h in range(C_HEADS):
        cols = slice(h * C_HEAD_DIM, (h + 1) * C_HEAD_DIM)
        s = _dot_t(q_ref[:, cols], kv_ref[:, cols])
        p = jnp.exp(s - jnp.max(s, axis=-1, keepdims=True))
        o = jnp.dot(p.astype(BF16), kv_ref[:, C_WIDTH + h * C_HEAD_DIM:C_WIDTH + (h + 1) * C_HEAD_DIM],
                    preferred_element_type=F32)
        o_ref[:, cols] = (o / jnp.sum(p, axis=-1, keepdims=True)).astype(o_ref.dtype)


def _attn_c(cqn, mem_kv, *, tm):
    s = cqn.shape[0]
    return pl.pallas_call(
        _attn_c_kernel,
        grid=(s // tm,),
        in_specs=[pl.BlockSpec((tm, C_WIDTH), lambda i: (i, 0)), _full_spec(mem_kv.shape)],
        out_specs=pl.BlockSpec((tm, C_WIDTH), lambda i: (i, 0)),
        out_shape=jax.ShapeDtypeStruct((s, C_WIDTH), BF16),
        compiler_params=_cparams("parallel"),
        name="attn_c",
    )(cqn, mem_kv)


def _merge_out_kernel(x_ref, oa_ref, ob_ref, oc_ref, az_ref, bz_ref, cz_ref, ga_ref, gb_ref, gc_ref,
                      wbr_ref, wout_ref, o_ref):
    merged = None
    for n, (o_r, z_r, g_r) in enumerate(((oa_ref, az_ref, ga_ref), (ob_ref, bz_ref, gb_ref),
                                         (oc_ref, cz_ref, gc_ref))):
        z = z_r[...].astype(F32)
        u = (o_r[...].astype(F32) * (z * jax.nn.sigmoid(z))).astype(BF16)
        y = jnp.dot(u, wbr_ref[n], preferred_element_type=F32)
        t = jax.nn.sigmoid(g_r[...].astype(F32)) * y
        merged = t if merged is None else merged + t
    o_ref[...] = x_ref[...] + jnp.dot(merged.astype(BF16), wout_ref[...], preferred_element_type=F32)


def _merge_out(x, o_a, o_b, o_c, proj, w_branch, w_out, *, tm):
    s = x.shape[0]
    row = lambda w: pl.BlockSpec((tm, w), lambda i: (i, 0))
    gate_blk = SEG_START["gates"] * LANES // D_MODEL
    gate = lambda n: pl.BlockSpec((tm, D_MODEL), lambda i: (i, gate_blk + n))
    single = pl.Buffered(1)
    return pl.pallas_call(
        _merge_out_kernel,
        grid=(s // tm,),
        in_specs=[row(D_MODEL), row(BRANCH_WIDTH), row(BRANCH_WIDTH), row(BRANCH_WIDTH),
                  _seg_spec(tm, "az"), _seg_spec(tm, "bz"), _seg_spec(tm, "cz"),
                  gate(0), gate(1), gate(2),
                  pl.BlockSpec(w_branch.shape, lambda i: (0, 0, 0), pipeline_mode=single),
                  pl.BlockSpec(w_out.shape, lambda i: (0, 0), pipeline_mode=single)],
        out_specs=row(D_MODEL),
        out_shape=jax.ShapeDtypeStruct((s, D_MODEL), F32),
        compiler_params=_cparams("parallel"),
        name="merge_out",
    )(x, o_a, o_b, o_c, proj, proj, proj, proj, proj, proj, w_branch, w_out)


def _regroup_tables():
    names = ("aq", "ak", "av", "az", "iq", "ik", "iw", "bcq", "bckv", "bkpe", "bz", "cq", "cz", "gates")
    src_off = dict(zip(names, [0] + IN_OFFSETS))
    src_off["ikw"] = src_off["ik"]
    valid_rows = dict(ikw=IDX_DIM + IDX_HEADS, bkpe=B_ROPE)
    start, valid = [], []
    for name, units in SEG_UNITS.items():
        for u in range(units):
            start.append(src_off[name] + u * LANES)
            valid.append(valid_rows.get(name, LANES))
    return np.asarray(start, np.int32), np.asarray(valid, np.int32)


REGROUP_START, REGROUP_VALID = _regroup_tables()
REGROUP_ALIGN = math.gcd(*(int(v) for v in REGROUP_START if v))
assert REGROUP_ALIGN % SUBLANES == 0


def _regroup_kernel(start_ref, valid_ref, w_ref, o_ref):
    row = lax.broadcasted_iota(I32, o_ref.shape, 0)
    o_ref[...] = jnp.where(row < valid_ref[pl.program_id(0)], w_ref[...], 0.0).astype(BF16)


def _regroup_w_in_t(w_in_t):
    d = w_in_t.shape[1]
    grid_spec = pltpu.PrefetchScalarGridSpec(
        num_scalar_prefetch=2,
        grid=(PROJ_UNITS,),
        in_specs=[pl.BlockSpec((pl.Element(LANES), pl.Element(d)),
                               lambda b, start, valid: (start[b] * REGROUP_ALIGN, 0))],
        out_specs=pl.BlockSpec((LANES, d), lambda b, *_: (b, 0)),
    )
    return pl.pallas_call(
        _regroup_kernel,
        grid_spec=grid_spec,
        out_shape=jax.ShapeDtypeStruct((PROJ_WIDTH, d), BF16),
        compiler_params=_cparams("arbitrary"),
        name="regroup_w_in",
    )(jnp.asarray(REGROUP_START // REGROUP_ALIGN), jnp.asarray(REGROUP_VALID), w_in_t)


def _pad_heads(w, n_heads, width, pad_to):
    r = w.shape[0]
    w = w.reshape(r, n_heads, width)
    w = jnp.pad(w, ((0, 0), (0, 0), (0, pad_to - width)))
    return w.reshape(r, n_heads * pad_to)


def kernel(x, mem, positions, rel_bias, norm_g, mem_norm_g, w_in, a_q_norm_g, a_k_norm_g, b_q_lat_norm_g,
           b_kv_lat_norm_g, w_b_uq, w_b_ukv, b_q_norm_g, b_k_norm_g, w_mem_kv, c_q_norm_g, c_k_norm_g,
           w_branch, w_out):
    bsz, seq, d = x.shape
    assert d == D_MODEL and norm_g.shape[0] == 1
    topk = min(TOPK_MAX, seq // 4)
    tq_a = 256
    tq_b = 512
    tm_in = min(1024, seq)

    inv_freq = 1.0 / (ROPE_THETA ** (jnp.arange(0, B_ROPE, 2, dtype=F32) / B_ROPE))
    zeros_half = jnp.zeros((B_ROPE,), F32)
    gq_pad = jnp.concatenate([b_q_norm_g[0], zeros_half])[None, :]
    gk_pad = jnp.concatenate([b_k_norm_g[0], zeros_half])[None, :]
    lut_t = ((rel_bias[jnp.asarray(T5_TABLE)] - rel_bias[REL_BUCKETS - 1]).T * LOG2E).astype(F32)

    w_cat = _regroup_w_in_t(jnp.transpose(w_in[0]))
    wuq = _pad_heads(w_b_uq[0], B_HEADS, B_QK_DIM, B_QK_PAD).astype(BF16)
    wukv = w_b_ukv[0].reshape(B_KV_RANK, B_HEADS, B_NOPE + B_V)
    wuk = wukv[:, :, :B_NOPE].reshape(B_KV_RANK, B_HEADS * B_NOPE).astype(BF16)
    wuvt = jnp.transpose(wukv[:, :, B_NOPE:], (1, 2, 0)).astype(BF16)
    wbr = w_branch[0].astype(BF16)
    wout = w_out[0].astype(BF16)

    outs = []
    for b in range(bsz):
        pos = positions[b]
        ang = pos.astype(F32)[:, None] * inv_freq
        cos, sin = jnp.cos(ang), jnp.sin(ang)
        zpad = jnp.zeros((seq, LANES - B_ROPE), F32)
        cos128 = jnp.concatenate([cos, cos, zpad], axis=1)
        sin128 = jnp.concatenate([-sin, sin, zpad], axis=1)

        mem_kv = _mem_kv(mem[b], mem_norm_g[0][None, :], w_mem_kv[0], c_k_norm_g[0][None, :])
        proj = _in_proj(x[b], norm_g[0][None, :], w_cat, tm=tm_in, tn=512)
        aqn, akn, kab, wt, avt, qb, kb, vbt, cqn, knorm2 = _prep(
            proj, cos128, sin128, a_q_norm_g[0][None, :], a_k_norm_g[0][None, :],
            b_q_lat_norm_g[0][None, :], b_kv_lat_norm_g[0][None, :], gq_pad, gk_pad,
            c_q_norm_g[0][None, :], wuq, wuk, wuvt)
        o_b = _attn_b(qb, kb, vbt, knorm2, tq=tq_b)
        qmin = jnp.min(pos.reshape(seq // tq_a, tq_a), axis=1)
        kmax = jnp.max(pos.reshape(seq // KV_TILE, KV_TILE), axis=1)
        o_a = _attn_a(proj, wt, kab, aqn, akn, avt, pos[None, :], pos[:, None],
                      lut_t, knorm2, qmin, kmax, tq=tq_a, topk=topk)
        o_c = _attn_c(cqn, mem_kv, tm=512)
        outs.append(_merge_out(x[b], o_a, o_b, o_c, proj, wbr, wout, tm=256))
    return jnp.stack(outs, axis=0)
```

```python
import functools
import math

import numpy as np
import jax
import jax.numpy as jnp
from jax import lax
from jax.experimental import pallas as pl
from jax.experimental.pallas import tpu as pltpu

F32 = jnp.float32
BF16 = jnp.bfloat16
I32 = jnp.int32

D_MODEL = 2048
BRANCH_WIDTH = 1024
N_BRANCH = 3
A_HEADS = 8
A_KV_HEADS = 2
A_HEAD_DIM = 128
A_WIDTH = A_HEADS * A_HEAD_DIM
A_KV_WIDTH = A_KV_HEADS * A_HEAD_DIM
A_REP = A_HEADS // A_KV_HEADS
IDX_HEADS = 16
IDX_DIM = 64
TOPK_MAX = 256
IDX_SCALE = (IDX_DIM ** -0.5) * (IDX_HEADS ** -0.5)
A_SCALE = A_HEAD_DIM ** -0.5
B_HEADS = 8
B_Q_RANK = 512
B_KV_RANK = 256
B_NOPE = 128
B_ROPE = 64
B_QK_DIM = B_NOPE + B_ROPE
B_V = 128
B_WIDTH = B_HEADS * B_V
B_SCALE = B_QK_DIM ** -0.5
ROPE_THETA = 10000.0
C_HEADS = 4
C_HEAD_DIM = 256
C_WIDTH = C_HEADS * C_HEAD_DIM
C_SCALE = C_HEAD_DIM ** -0.5
REL_BUCKETS = 32
REL_MAX_DIST = 128
EPS = 1e-6

IN_SIZES = (A_WIDTH, A_KV_WIDTH, A_KV_WIDTH, A_WIDTH, IDX_HEADS * IDX_DIM, IDX_DIM, IDX_HEADS,
            B_Q_RANK, B_KV_RANK, B_ROPE, B_WIDTH, C_WIDTH, C_WIDTH, N_BRANCH * D_MODEL)
IN_OFFSETS = [int(o) for o in np.cumsum(IN_SIZES)[:-1]]

LANES = 128
SUBLANES = 8
B_QK_PAD = 2 * LANES
VMEM_LIMIT_BYTES = 56 * 1024 * 1024
MASK_VALUE = -1e30
LOG2E = math.log2(math.e)
SHIFT_MARGIN = 1.02
MAX_SHIFT_SPAN = 100.0
KNORM_ROWS = 16
DV = 128
ONES_ROWS = 16
DV_AUG = DV + ONES_ROWS
assert DV == A_HEAD_DIM == B_V
KV_TILE = 512
B_HEADS_PER_STEP = 2

SEG_UNITS = dict(gates=48, aq=8, az=8, iq=8, bz=8, cq=8, cz=8, bcq=4, ak=2, av=2, bckv=2, ikw=1, bkpe=1)
SEG_START = {}
_u = 0
for _name, _w in SEG_UNITS.items():
    SEG_START[_name] = _u
    _u += _w
PROJ_UNITS = _u
PROJ_WIDTH = PROJ_UNITS * LANES


def _t5_bucket_table():
    d = np.arange(LANES)
    max_exact = REL_BUCKETS // 2
    nf = np.maximum(d, 1).astype(np.float64)
    large = max_exact + (np.log(nf / max_exact) / math.log(REL_MAX_DIST / max_exact)
                         * (REL_BUCKETS - max_exact)).astype(np.int64)
    large = np.minimum(large, REL_BUCKETS - 1)
    table = np.where(d < max_exact, d, large).astype(np.int32)
    far = int(np.min(np.nonzero(table == REL_BUCKETS - 1)[0]))
    assert np.all(table[far:] == REL_BUCKETS - 1)
    return table, far


T5_TABLE, T5_FAR = _t5_bucket_table()


def _cparams(*sem):
    return pltpu.CompilerParams(dimension_semantics=sem, vmem_limit_bytes=VMEM_LIMIT_BYTES)


def _dot_t(a, b):
    return lax.dot_general(a, b, (((1,), (1,)), ((), ())), preferred_element_type=F32)


def _rms_scale(ss, n):
    return lax.rsqrt(ss * (1.0 / n) + EPS)


def _mem_kv_kernel(mem_ref, g_ref, w_ref, ckn_ref, o_ref, h_ref):
    j = pl.program_id(0)

    @pl.when(j == 0)
    def _():
        m = mem_ref[...]
        r = _rms_scale(jnp.sum(m * m, axis=-1, keepdims=True), D_MODEL)
        h_ref[...] = (m * r * g_ref[...]).astype(BF16)

    y = jnp.dot(h_ref[...], w_ref[...].astype(BF16), preferred_element_type=F32)

    @pl.when(j < C_HEADS)
    def _():
        r = _rms_scale(jnp.sum(y * y, axis=-1, keepdims=True), C_HEAD_DIM)
        o_ref[...] = (y * r * ckn_ref[...]).astype(BF16)

    @pl.when(j >= C_HEADS)
    def _():
        o_ref[...] = y.astype(BF16)


def _mem_kv(mem, mem_g, w_mem_kv, ck_g):
    n_mem = mem.shape[0]
    return pl.pallas_call(
        _mem_kv_kernel,
        grid=(2 * C_HEADS,),
        in_specs=[
            pl.BlockSpec((n_mem, D_MODEL), lambda j: (0, 0)),
            pl.BlockSpec((1, D_MODEL), lambda j: (0, 0)),
            pl.BlockSpec((D_MODEL, C_HEAD_DIM), lambda j: (0, j)),
            pl.BlockSpec((1, C_HEAD_DIM), lambda j: (0, 0)),
        ],
        out_specs=pl.BlockSpec((n_mem, C_HEAD_DIM), lambda j: (0, j)),
        out_shape=jax.ShapeDtypeStruct((n_mem, 2 * C_WIDTH), BF16),
        scratch_shapes=[pltpu.VMEM((n_mem, D_MODEL), BF16)],
        compiler_params=_cparams("arbitrary"),
        name="mem_kv",
    )(mem, mem_g, w_mem_kv, ck_g)


def _in_proj_kernel(x_ref, g_ref, w_ref, o_ref, h_ref, *, row_chunk):
    @pl.when(pl.program_id(1) == 0)
    def _():
        g = g_ref[...]
        for c in range(x_ref.shape[0] // row_chunk):
            rows = pl.ds(c * row_chunk, row_chunk)
            x = x_ref[rows, :]
            r = _rms_scale(jnp.sum(x * x, axis=-1, keepdims=True), D_MODEL)
            h_ref[rows, :] = (x * r * g).astype(BF16)

    o_ref[...] = _dot_t(h_ref[...], w_ref[...]).astype(o_ref.dtype)


def _in_proj(x, norm_g, w_cat, *, tm, tn):
    s = x.shape[0]
    return pl.pallas_call(
        functools.partial(_in_proj_kernel, row_chunk=min(tm, 256)),
        grid=(s // tm, PROJ_WIDTH // tn),
        in_specs=[
            pl.BlockSpec((tm, D_MODEL), lambda i, j: (i, 0)),
            pl.BlockSpec((1, D_MODEL), lambda i, j: (0, 0)),
            pl.BlockSpec((tn, D_MODEL), lambda i, j: (j, 0)),
        ],
        out_specs=pl.BlockSpec((tm, tn), lambda i, j: (i, j)),
        out_shape=jax.ShapeDtypeStruct((s, PROJ_WIDTH), BF16),
        scratch_shapes=[pltpu.VMEM((tm, D_MODEL), BF16)],
        compiler_params=_cparams("parallel", "arbitrary"),
        name="in_proj",
    )(x, norm_g, w_cat)


def _rope128(x, cos, sin):
    half = B_ROPE // 2
    lane = lax.broadcasted_iota(I32, x.shape, 1)
    partner = jnp.where(lane < half, pltpu.roll(x, LANES - half, 1), pltpu.roll(x, half, 1))
    return x * cos + partner * sin


def _prep_kernel(aq_ref, ak_ref, av_ref, ikw_ref, bcq_ref, bckv_ref, bkpe_ref, cq_ref,
                 cos_ref, sin_ref, aqg_ref, akg_ref, bqlg_ref, bkvlg_ref, bqg_ref, bkg_ref, cqg_ref,
                 wuq_ref, wuk_ref, wuvt_ref,
                 aqn_ref, akn_ref, kab_ref, wt_ref, avt_ref, qb_ref, kb_ref, vbt_ref, cqn_ref, kn_ref):
    cos = cos_ref[...]
    sin = sin_ref[...]

    @pl.when(pl.program_id(0) == 0)
    def _():
        kn_ref[...] = jnp.zeros(kn_ref.shape, F32)

    def note_key_norm(row, sq_norms):
        top = jnp.max(sq_norms, axis=0, keepdims=True)
        kn_ref[row:row + 1, :] = jnp.maximum(kn_ref[row:row + 1, :], top)

    eye = jnp.where(lax.broadcasted_iota(I32, (LANES, LANES), 0) == lax.broadcasted_iota(I32, (LANES, LANES), 1),
                    1.0, 0.0).astype(BF16)

    aqg = aqg_ref[...] * (A_SCALE * LOG2E)
    for h in range(A_HEADS):
        cols = slice(h * A_HEAD_DIM, (h + 1) * A_HEAD_DIM)
        v = aq_ref[:, cols].astype(F32)
        r = _rms_scale(jnp.sum(v * v, axis=-1, keepdims=True), A_HEAD_DIM)
        aqn_ref[:, cols] = (v * r * aqg).astype(BF16)
    akg = akg_ref[...]
    for h in range(A_KV_HEADS):
        cols = slice(h * A_HEAD_DIM, (h + 1) * A_HEAD_DIM)
        v = ak_ref[:, cols].astype(F32)
        r = _rms_scale(jnp.sum(v * v, axis=-1, keepdims=True), A_HEAD_DIM)
        kn = v * r * akg
        akn_ref[:, cols] = kn.astype(BF16)
        note_key_norm(B_HEADS + h, jnp.sum(kn * kn, axis=-1, keepdims=True))
        avt_ref[h, 0, 0:DV] = _dot_t(eye, av_ref[:, cols]).astype(BF16)
        avt_ref[h, 0, DV:DV_AUG] = jnp.ones((ONES_ROWS, avt_ref.shape[-1]), BF16)

    ikw = ikw_ref[...]
    lane = lax.broadcasted_iota(I32, ikw.shape, 1)
    zero = jnp.zeros_like(ikw)
    kab_ref[:, 0:LANES] = jnp.where(lane < IDX_DIM, ikw, zero)
    kab_ref[:, LANES:2 * LANES] = jnp.where(lane >= IDX_DIM, pltpu.roll(ikw.astype(F32), IDX_DIM, 1).astype(BF16), zero)
    wt_ref[...] = _dot_t(eye, ikw) * IDX_SCALE

    cq_lat = bcq_ref[...].astype(F32)
    r = _rms_scale(jnp.sum(cq_lat * cq_lat, axis=-1, keepdims=True), B_Q_RANK)
    qlat = (cq_lat * r * bqlg_ref[...]).astype(BF16)
    bqg = bqg_ref[...] * (B_SCALE * LOG2E)
    for h in range(B_HEADS):
        cols = slice(h * B_QK_PAD, (h + 1) * B_QK_PAD)
        qh = jnp.dot(qlat, wuq_ref[:, cols], preferred_element_type=F32)
        r = _rms_scale(jnp.sum(qh * qh, axis=-1, keepdims=True), B_QK_DIM)
        qn = qh * r * bqg
        qb_ref[:, h * B_QK_PAD:h * B_QK_PAD + LANES] = qn[:, :LANES].astype(BF16)
        qb_ref[:, h * B_QK_PAD + LANES:(h + 1) * B_QK_PAD] = _rope128(qn[:, LANES:], cos, sin).astype(BF16)

    ckv = bckv_ref[...].astype(F32)
    r = _rms_scale(jnp.sum(ckv * ckv, axis=-1, keepdims=True), B_KV_RANK)
    kvlat = (ckv * r * bkvlg_ref[...]).astype(BF16)
    kpe = bkpe_ref[...].astype(F32)
    ss_pe = jnp.sum(kpe * kpe, axis=-1, keepdims=True)
    bkg = bkg_ref[...]
    kpe_rot = _rope128(kpe * bkg[:, LANES:], cos, sin)
    for h in range(B_HEADS):
        kn = jnp.dot(kvlat, wuk_ref[:, h * B_NOPE:(h + 1) * B_NOPE], preferred_element_type=F32)
        r = _rms_scale(jnp.sum(kn * kn, axis=-1, keepdims=True) + ss_pe, B_QK_DIM)
        k_nope = kn * r * bkg[:, :LANES]
        k_pe = kpe_rot * r
        kb_ref[:, h * B_QK_PAD:h * B_QK_PAD + LANES] = k_nope.astype(BF16)
        kb_ref[:, h * B_QK_PAD + LANES:(h + 1) * B_QK_PAD] = k_pe.astype(BF16)
        note_key_norm(h, jnp.sum(k_nope * k_nope, axis=-1, keepdims=True)
                      + jnp.sum(k_pe * k_pe, axis=-1, keepdims=True))
        vbt_ref[h, 0, 0:DV] = _dot_t(wuvt_ref[h], kvlat).astype(BF16)
        vbt_ref[h, 0, DV:DV_AUG] = jnp.ones((ONES_ROWS, vbt_ref.shape[-1]), BF16)

    cqg = cqg_ref[...] * C_SCALE
    for h in range(C_HEADS):
        cols = slice(h * C_HEAD_DIM, (h + 1) * C_HEAD_DIM)
        v = cq_ref[:, cols].astype(F32)
        r = _rms_scale(jnp.sum(v * v, axis=-1, keepdims=True), C_HEAD_DIM)
        cqn_ref[:, cols] = (v * r * cqg).astype(BF16)


def _seg_spec(tm, name):
    units = SEG_UNITS[name]
    blk = SEG_START[name] // units
    return pl.BlockSpec((tm, units * LANES), lambda i: (i, blk))


def _full_spec(shape):
    return pl.BlockSpec(shape, lambda i: (0,) * len(shape))


def _prep(proj, cos128, sin128, aqg, akg, bqlg, bkvlg, bqg, bkg, cqg, wuq, wuk, wuvt):
    s = proj.shape[0]
    tm = KV_TILE
    nt = s // tm
    row = lambda w: pl.BlockSpec((tm, w), lambda i: (i, 0))
    tiles_t = lambda n: pl.BlockSpec((n, 1, DV_AUG, tm), lambda i: (0, i, 0, 0))
    small = [aqg, akg, bqlg, bkvlg, bqg, bkg, cqg, wuq, wuk, wuvt]
    out_specs = [row(A_WIDTH), row(A_KV_WIDTH), row(2 * LANES),
                 pl.BlockSpec((LANES, tm), lambda i: (0, i)), tiles_t(A_KV_HEADS),
                 row(B_HEADS * B_QK_PAD), row(B_HEADS * B_QK_PAD), tiles_t(B_HEADS), row(C_WIDTH),
                 _full_spec((KNORM_ROWS, LANES))]
    out_shape = [
        jax.ShapeDtypeStruct((s, A_WIDTH), BF16),
        jax.ShapeDtypeStruct((s, A_KV_WIDTH), BF16),
        jax.ShapeDtypeStruct((s, 2 * LANES), BF16),
        jax.ShapeDtypeStruct((LANES, s), F32),
        jax.ShapeDtypeStruct((A_KV_HEADS, nt, DV_AUG, tm), BF16),
        jax.ShapeDtypeStruct((s, B_HEADS * B_QK_PAD), BF16),
        jax.ShapeDtypeStruct((s, B_HEADS * B_QK_PAD), BF16),
        jax.ShapeDtypeStruct((B_HEADS, nt, DV_AUG, tm), BF16),
        jax.ShapeDtypeStruct((s, C_WIDTH), BF16),
        jax.ShapeDtypeStruct((KNORM_ROWS, LANES), F32),
    ]
    return pl.pallas_call(
        _prep_kernel,
        grid=(nt,),
        in_specs=[_seg_spec(tm, n) for n in ("aq", "ak", "av", "ikw", "bcq", "bckv", "bkpe", "cq")]
        + [row(LANES), row(LANES)] + [_full_spec(a.shape) for a in small],
        out_specs=out_specs,
        out_shape=out_shape,
        compiler_params=_cparams("arbitrary"),
        name="prep",
    )(*([proj] * 8), cos128, sin128, *small)


def _reduce_keys(x, op):
    tk, tq = x.shape
    n = tk // SUBLANES
    assert n & (n - 1) == 0
    t = x.reshape(n, SUBLANES, tq)
    while n > 1:
        n //= 2
        t = op(t[:n], t[n:2 * n])
    red = jnp.max if op is jnp.maximum else jnp.sum
    return red(t[0], axis=0, keepdims=True)


def _softmax_step_t(st, vt, m_ref, acc_ref):
    m_prev = m_ref[...]
    m_new = jnp.maximum(m_prev, _reduce_keys(st, jnp.maximum))
    alpha = jnp.exp2(m_prev - m_new)
    p = jnp.exp2(st - m_new).astype(BF16)
    acc_ref[...] = alpha * acc_ref[...] + jnp.dot(vt, p, preferred_element_type=F32)
    m_ref[...] = m_new


def _fixed_shift_step_t(st, vt, shift, acc_ref):
    p = jnp.exp2(st - shift).astype(BF16)
    acc_ref[...] += jnp.dot(vt, p, preferred_element_type=F32)


def _normalised_output(acc):
    return (acc[0:DV] / acc[DV:DV + 1]).T


def _logit_bound(q, kmax2):
    qf = q.astype(F32)
    ones = jnp.ones((SUBLANES, q.shape[1]), BF16)
    qn2 = _dot_t(ones, (qf * qf).astype(BF16))[0:1]
    return jnp.sqrt(qn2 * kmax2) * SHIFT_MARGIN


def _causal_ok_t(i, j, tq, tk):
    key = j * tk + lax.broadcasted_iota(I32, (tk, tq), 0)
    qry = i * tq + lax.broadcasted_iota(I32, (tk, tq), 1)
    return key <= qry


def _init_softmax_state(m_ref, acc_ref):
    m_ref[...] = jnp.full(m_ref.shape, MASK_VALUE, F32)
    acc_ref[...] = jnp.zeros(acc_ref.shape, F32)


def _pingpong_tiles(n_full, logits, consume, buf_a, buf_b):
    logits(0, buf_a)

    def pair(p, c):
        j = 2 * p
        logits(j + 1, buf_b)
        consume(j, buf_a, False)
        logits(j + 2, buf_a)
        consume(j + 1, buf_b, False)
        return c

    lax.fori_loop(0, n_full // 2, pair, 0)
    odd = n_full % 2 == 1

    @pl.when(odd)
    def _():
        logits(n_full, buf_b)
        consume(n_full - 1, buf_a, False)
        consume(n_full, buf_b, True)

    @pl.when(jnp.logical_not(odd))
    def _():
        consume(n_full, buf_a, True)


def _attn_b_kernel(q_ref, k_ref, vt_ref, kn_ref, o_ref, sa_ref, sb_ref, m_ref, acc_ref, *, tq, tk):
    hp = pl.program_id(0)
    i = pl.program_id(1)
    _init_softmax_state(m_ref, acc_ref)
    j_diag = (i * tq) // tk
    heads = range(B_HEADS_PER_STEP)
    shifts = [_logit_bound(q_ref[:, hh * B_QK_PAD:(hh + 1) * B_QK_PAD],
                           kn_ref[pl.ds(hp * B_HEADS_PER_STEP + hh, 1), 0:1]) for hh in heads]
    fixed_ok = jnp.max(2.0 * functools.reduce(jnp.maximum, shifts)) <= MAX_SHIFT_SPAN

    def logits(j, buf):
        rows = pl.ds(pl.multiple_of(j * tk, tk), tk)
        for hh in heads:
            cols = slice(hh * B_QK_PAD, (hh + 1) * B_QK_PAD)
            buf[hh] = _dot_t(k_ref[rows, cols], q_ref[:, cols])

    def consume(j, buf, masked, fixed):
        for hh in heads:
            st = buf[hh]
            if masked:
                st = jnp.where(_causal_ok_t(i, j, tq, tk), st, MASK_VALUE)
            if fixed:
                _fixed_shift_step_t(st, vt_ref[hh, j], shifts[hh], acc_ref.at[hh])
            else:
                _softmax_step_t(st, vt_ref[hh, j], m_ref.at[hh], acc_ref.at[hh])

    @pl.when(fixed_ok)
    def _():
        _pingpong_tiles(j_diag, logits, functools.partial(consume, fixed=True), sa_ref, sb_ref)

    @pl.when(jnp.logical_not(fixed_ok))
    def _():
        _pingpong_tiles(j_diag, logits, functools.partial(consume, fixed=False), sa_ref, sb_ref)

    for hh in heads:
        o_ref[:, hh * B_V:(hh + 1) * B_V] = _normalised_output(acc_ref[hh]).astype(o_ref.dtype)


def _attn_b(qb, kb, vbt, knorm2, *, tq):
    s = qb.shape[0]
    tk = KV_TILE
    hps = B_HEADS_PER_STEP
    assert tk % tq == 0 or tq % tk == 0
    assert tq <= tk
    return pl.pallas_call(
        functools.partial(_attn_b_kernel, tq=tq, tk=tk),
        grid=(B_HEADS // hps, s // tq),
        in_specs=[
            pl.BlockSpec((tq, hps * B_QK_PAD), lambda h, i: (i, h)),
            pl.BlockSpec((s, hps * B_QK_PAD), lambda h, i: (0, h)),
            pl.BlockSpec((hps, s // tk, DV_AUG, tk), lambda h, i: (h, 0, 0, 0)),
            pl.BlockSpec((KNORM_ROWS, LANES), lambda h, i: (0, 0)),
        ],
        out_specs=pl.BlockSpec((tq, hps * B_V), lambda h, i: (i, h)),
        out_shape=jax.ShapeDtypeStruct((s, B_WIDTH), BF16),
        scratch_shapes=[pltpu.VMEM((hps, tk, tq), F32), pltpu.VMEM((hps, tk, tq), F32),
                        pltpu.VMEM((hps, 1, tq), F32), pltpu.VMEM((hps, DV_AUG, tq), F32)],
        compiler_params=_cparams("parallel", "arbitrary"),
        name="attn_b",
    )(qb, kb, vbt, knorm2)


def _order_keys(x):
    b = lax.bitcast_convert_type(x + 0.0, I32)
    key = b ^ (lax.shift_right_arithmetic(b, 31) & 0x7FFFFFFF)
    coarse = lax.bitcast_convert_type(b & jnp.int32(-(1 << COARSE_BITS)), F32).astype(BF16)
    return key, coarse


def _coarse_key_as_bf16(k):
    b = k ^ (lax.shift_right_arithmetic(k, COARSE_BITS - 1) & ((1 << (COARSE_BITS - 1)) - 1))
    return lax.bitcast_convert_type(lax.shift_left(b, COARSE_BITS), F32).astype(BF16)


KEY_BITS = 32
COARSE_BITS = 16
BISECT_GROUP = 4
PACKED_SUBLANES = 16
assert COARSE_BITS % BISECT_GROUP == 0 and KEY_BITS % BISECT_GROUP == 0


def _count_hits_packed(hit):
    tk, tq = hit.shape
    n = tk // PACKED_SUBLANES
    assert n <= 256
    t = jnp.where(hit, jnp.ones((), BF16), jnp.zeros((), BF16)).reshape(n, PACKED_SUBLANES, tq)
    while n > 1:
        n //= 2
        t = t[:n] + t[n:2 * n]
    return t[0].astype(F32)


def _count_hits(hit):
    tk, tq = hit.shape
    n = tk // SUBLANES
    t = jnp.where(hit, 1, 0).reshape(n, SUBLANES, tq)
    while n > 1:
        n //= 2
        t = t[:n] + t[n:2 * n]
    return t[0]


def _attn_a_kernel(qmin_ref, kmax_ref,
                   iq_ref, wt_ref, kab_ref, aqn_ref, akn_ref, avt_ref, pq_ref, pk_ref, lut_ref, kn_ref,
                   o_ref,
                   key_ref, coarse_ref, thr_ref, settled_ref, sa_ref, sb_ref, m_ref, acc_ref, *, tq, tk, topk):
    i = pl.program_id(0)
    j_diag = (i * tq) // tk
    n_tiles = j_diag + 1
    int_min = jnp.int32(-2 ** 31)

    def score_tile(j, masked):
        rows = pl.ds(pl.multiple_of(j * tk, tk), tk)
        k_lo = kab_ref[rows, 0:LANES]
        k_hi = kab_ref[rows, LANES:2 * LANES]
        acc = jnp.zeros((tk, tq), F32)
        for p in range(IDX_HEADS // 2):
            rhs = iq_ref[:, p * LANES:(p + 1) * LANES]
            w_row = IDX_DIM + 2 * p
            acc = acc + jnp.maximum(_dot_t(k_lo, rhs), 0.0) * wt_ref[w_row:w_row + 1, :]
            acc = acc + jnp.maximum(_dot_t(k_hi, rhs), 0.0) * wt_ref[w_row + 1:w_row + 2, :]
        if masked:
            acc = jnp.where(_causal_ok_t(i, j, tq, tk), acc, -jnp.inf)
        key_ref[j], coarse_ref[j] = _order_keys(acc)

    def score_body(j, c):
        score_tile(j, False)
        return c

    lax.fori_loop(0, j_diag, score_body, 0)
    score_tile(j_diag, True)

    def count_queries(pred_fn):
        def body(j, cnt):
            return cnt + _count_hits(pred_fn(key_ref[j], j))
        cnt = lax.fori_loop(0, n_tiles, body, jnp.zeros((SUBLANES, tq), I32))
        return jnp.sum(cnt, axis=0, keepdims=True)

    def bit_step(bit, t, settled):
        cand = t + lax.shift_left(jnp.int32(1), bit)
        cnt = count_queries(lambda x, j: x >= cand)
        t = jnp.where((settled == 0) & (cnt >= topk), cand, t)
        return t, jnp.where(cnt == topk, 1, settled)

    def bisect_low_bits(t, settled, n_bits):
        def cond(c):
            g, _, s = c
            return (g < n_bits // BISECT_GROUP) & (jnp.min(s) == 0)

        def body(c):
            g, t, s = c
            for u in range(BISECT_GROUP):
                t, s = bit_step(n_bits - 1 - (g * BISECT_GROUP + u), t, s)
            return g + 1, t, s

        _, t, s = lax.while_loop(cond, body, (jnp.int32(0), t, settled))
        thr_ref[...] = t
        settled_ref[...] = s

    def count_coarse(cand):
        def body(j, cnt):
            return cnt + _count_hits_packed(coarse_ref[j] >= cand)
        cnt = lax.fori_loop(0, n_tiles, body, jnp.zeros((PACKED_SUBLANES, tq), F32))
        return jnp.sum(cnt, axis=0, keepdims=True)

    def coarse_step(it, t):
        cand = t + lax.shift_left(jnp.int32(1), COARSE_BITS - 1 - it)
        return jnp.where(count_coarse(_coarse_key_as_bf16(cand)) >= topk, cand, t)

    coarse_min = -(1 << (COARSE_BITS - 1))
    t_coarse = lax.fori_loop(0, COARSE_BITS, coarse_step, jnp.full((1, tq), coarse_min, I32))

    low_bits = KEY_BITS - COARSE_BITS
    lo_edge = lax.shift_left(t_coarse, low_bits)
    cnt_lo = count_queries(lambda x, j: x >= lo_edge)
    cnt_hi = count_queries(lambda x, j: x >= lax.shift_left(t_coarse + 1, low_bits))
    bracket_ok = (cnt_lo >= topk) & ((cnt_hi < topk) | (t_coarse == -coarse_min - 1))
    coarse_ok = jnp.min(jnp.where(bracket_ok, 1, 0)) == 1

    @pl.when(coarse_ok)
    def _():
        bisect_low_bits(lo_edge, jnp.where(cnt_lo == topk, 1, 0), low_bits)

    @pl.when(jnp.logical_not(coarse_ok))
    def _():
        bisect_low_bits(jnp.full((1, tq), int_min, I32), jnp.zeros((1, tq), I32), KEY_BITS)

    thr = thr_ref[...]
    settled = settled_ref[...]

    def demote_excess_ties():
        cnt_gt = count_queries(lambda x, j: x > thr)
        need = topk - cnt_gt
        kpos = lax.broadcasted_iota(I32, (tk, tq), 0)

        def pos_body(it, cut):
            cand = cut + lax.shift_left(jnp.int32(1), 30 - it)
            cnt = count_queries(lambda x, j: (x == thr) & (j * tk + kpos < cand))
            return jnp.where(cnt < need, cand, cut)

        cut = lax.fori_loop(0, 31, pos_body, jnp.zeros((1, tq), I32))

        def demote_body(j, c):
            x = key_ref[j]
            drop = (x == thr) & (j * tk + kpos > cut) & (thr > int_min)
            key_ref[j] = jnp.where(drop, x - 1, x)
            return c

        lax.fori_loop(0, n_tiles, demote_body, 0)

    @pl.when(jnp.min(settled) == 0)
    def _():
        cnt_ge = count_queries(lambda x, j: x >= thr)
        pl.when(jnp.max(cnt_ge) > topk)(demote_excess_ties)

    _init_softmax_state(m_ref, acc_ref)
    pq = pq_ref[...]
    lut = lut_ref[...]
    bias_hi = jnp.max(lut, axis=1, keepdims=True)
    bias_span = bias_hi - jnp.min(lut, axis=1, keepdims=True)
    bounds = [_logit_bound(aqn_ref[:, h * A_HEAD_DIM:(h + 1) * A_HEAD_DIM],
                           kn_ref[B_HEADS + h // A_REP:B_HEADS + h // A_REP + 1, 0:1]) for h in range(A_HEADS)]
    shifts = [bounds[h] + bias_hi[h:h + 1] for h in range(A_HEADS)]
    spans = [2.0 * bounds[h] + bias_span[h:h + 1] for h in range(A_HEADS)]
    fixed_ok = jnp.max(functools.reduce(jnp.maximum, spans)) <= MAX_SHIFT_SPAN

    def attend(j, masked, near, fixed):
        rows = pl.ds(pl.multiple_of(j * tk, tk), tk)
        sel = key_ref[j] >= thr
        if masked:
            sel = sel & _causal_ok_t(i, j, tq, tk)
        mask_bias = jnp.where(sel, 0.0, MASK_VALUE)
        if near:
            dist = jnp.clip(pq - pk_ref[rows, :], 0, LANES - 1)

        def logits(h, buf):
            g = h // A_REP
            buf[...] = _dot_t(akn_ref[rows, g * A_HEAD_DIM:(g + 1) * A_HEAD_DIM],
                              aqn_ref[:, h * A_HEAD_DIM:(h + 1) * A_HEAD_DIM])

        bufs = (sa_ref, sb_ref)
        logits(0, bufs[0])
        for h in range(A_HEADS):
            if h + 1 < A_HEADS:
                logits(h + 1, bufs[(h + 1) % 2])
            st = bufs[h % 2][...] + mask_bias
            if near:
                table = jnp.broadcast_to(lut_ref[h:h + 1, :], (tk, LANES))
                st = st + jnp.concatenate(
                    [jnp.take_along_axis(table, dist[:, c * LANES:(c + 1) * LANES], axis=1,
                                         mode="promise_in_bounds")
                     for c in range(tq // LANES)], axis=1)
            if fixed:
                _fixed_shift_step_t(st, avt_ref[h // A_REP, j], shifts[h], acc_ref.at[h])
            else:
                _softmax_step_t(st, avt_ref[h // A_REP, j], m_ref.at[h], acc_ref.at[h])

    def attend_dyn(j, masked, fixed):
        far = qmin_ref[i] - kmax_ref[j] >= T5_FAR

        @pl.when(far)
        def _():
            attend(j, masked, False, fixed)

        @pl.when(jnp.logical_not(far))
        def _():
            attend(j, masked, True, fixed)

    def attend_all(fixed):
        def attend_body(j, c):
            attend_dyn(j, False, fixed)
            return c

        lax.fori_loop(0, j_diag, attend_body, 0)
        attend_dyn(j_diag, True, fixed)

    pl.when(fixed_ok)(functools.partial(attend_all, True))
    pl.when(jnp.logical_not(fixed_ok))(functools.partial(attend_all, False))

    for h in range(A_HEADS):
        q_cols = slice(h * A_HEAD_DIM, (h + 1) * A_HEAD_DIM)
        o_ref[:, q_cols] = _normalised_output(acc_ref[h]).astype(o_ref.dtype)


def _attn_a(proj, wt, kab, aqn, akn, avt, pos_row, pos_col, lut_t, knorm2, qmin, kmax, *, tq, topk):
    s = proj.shape[0]
    tk = KV_TILE
    assert tk % tq == 0
    iq_blk = SEG_START["iq"] // SEG_UNITS["iq"]
    grid_spec = pltpu.PrefetchScalarGridSpec(
        num_scalar_prefetch=2,
        grid=(s // tq,),
        in_specs=[
            pl.BlockSpec((tq, IDX_HEADS * IDX_DIM), lambda i, *_: (i, iq_blk)),
            pl.BlockSpec((LANES, tq), lambda i, *_: (0, i)),
            pl.BlockSpec((s, 2 * LANES), lambda i, *_: (0, 0)),
            pl.BlockSpec((tq, A_WIDTH), lambda i, *_: (i, 0)),
            pl.BlockSpec((s, A_KV_WIDTH), lambda i, *_: (0, 0)),
            pl.BlockSpec((A_KV_HEADS, s // tk, DV_AUG, tk), lambda i, *_: (0, 0, 0, 0)),
            pl.BlockSpec((1, tq), lambda i, *_: (0, i)),
            pl.BlockSpec((s, 1), lambda i, *_: (0, 0)),
            pl.BlockSpec((A_HEADS, LANES), lambda i, *_: (0, 0)),
            pl.BlockSpec((KNORM_ROWS, LANES), lambda i, *_: (0, 0)),
        ],
        out_specs=pl.BlockSpec((tq, A_WIDTH), lambda i, *_: (i, 0)),
        scratch_shapes=[
            pltpu.VMEM((s // tk, tk, tq), I32),
            pltpu.VMEM((s // tk, tk, tq), BF16),
            pltpu.VMEM((1, tq), I32),
            pltpu.VMEM((1, tq), I32),
            pltpu.VMEM((tk, tq), F32),
            pltpu.VMEM((tk, tq), F32),
            pltpu.VMEM((A_HEADS, 1, tq), F32),
            pltpu.VMEM((A_HEADS, DV_AUG, tq), F32),
        ],
    )
    return pl.pallas_call(
        functools.partial(_attn_a_kernel, tq=tq, tk=tk, topk=topk),
        grid_spec=grid_spec,
        out_shape=jax.ShapeDtypeStruct((s, A_WIDTH), BF16),
        compiler_params=_cparams("arbitrary"),
        name="attn_a",
    )(qmin, kmax, proj, wt, kab, aqn, akn, avt, pos_row, pos_col, lut_t, knorm2)


def _attn_c_kernel(q_ref, kv_ref, o_ref):
    for h in range(C_HEADS):
        cols = slice(h * C_HEAD_DIM, (h + 1) * C_HEAD_DIM)
        s = _dot_t(q_ref[:, cols], kv_ref[:, cols])
        p = jnp.exp(s - jnp.max(s, axis=-1, keepdims=True))
        o = jnp.dot(p.astype(BF16), kv_ref[:, C_WIDTH + h * C_HEAD_DIM:C_WIDTH + (h + 1) * C_HEAD_DIM],
                    preferred_element_type=F32)
        o_ref[:, cols] = (o / jnp.sum(p, axis=-1, keepdims=True)).astype(o_ref.dtype)


def _attn_c(cqn, mem_kv, *, tm):
    s = cqn.shape[0]
    return pl.pallas_call(
        _attn_c_kernel,
        grid=(s // tm,),
        in_specs=[pl.BlockSpec((tm, C_WIDTH), lambda i: (i, 0)), _full_spec(mem_kv.shape)],
        out_specs=pl.BlockSpec((tm, C_WIDTH), lambda i: (i, 0)),
        out_shape=jax.ShapeDtypeStruct((s, C_WIDTH), BF16),
        compiler_params=_cparams("parallel"),
        name="attn_c",
    )(cqn, mem_kv)


def _merge_out_kernel(x_ref, oa_ref, ob_ref, oc_ref, az_ref, bz_ref, cz_ref, ga_ref, gb_ref, gc_ref,
                      wbr_ref, wout_ref, o_ref):
    merged = None
    for n, (o_r, z_r, g_r) in enumerate(((oa_ref, az_ref, ga_ref), (ob_ref, bz_ref, gb_ref),
                                         (oc_ref, cz_ref, gc_ref))):
        z = z_r[...].astype(F32)
        u = (o_r[...].astype(F32) * (z * jax.nn.sigmoid(z))).astype(BF16)
        y = jnp.dot(u, wbr_ref[n], preferred_element_type=F32)
        t = jax.nn.sigmoid(g_r[...].astype(F32)) * y
        merged = t if merged is None else merged + t
    o_ref[...] = x_ref[...] + jnp.dot(merged.astype(BF16), wout_ref[...], preferred_element_type=F32)


def _merge_out(x, o_a, o_b, o_c, proj, w_branch, w_out, *, tm):
    s = x.shape[0]
    row = lambda w: pl.BlockSpec((tm, w), lambda i: (i, 0))
    gate_blk = SEG_START["gates"] * LANES // D_MODEL
    gate = lambda n: pl.BlockSpec((tm, D_MODEL), lambda i: (i, gate_blk + n))
    single = pl.Buffered(1)
    return pl.pallas_call(
        _merge_out_kernel,
        grid=(s // tm,),
        in_specs=[row(D_MODEL), row(BRANCH_WIDTH), row(BRANCH_WIDTH), row(BRANCH_WIDTH),
                  _seg_spec(tm, "az"), _seg_spec(tm, "bz"), _seg_spec(tm, "cz"),
                  gate(0), gate(1), gate(2),
                  pl.BlockSpec(w_branch.shape, lambda i: (0, 0, 0), pipeline_mode=single),
                  pl.BlockSpec(w_out.shape, lambda i: (0, 0), pipeline_mode=single)],
        out_specs=row(D_MODEL),
        out_shape=jax.ShapeDtypeStruct((s, D_MODEL), F32),
        compiler_params=_cparams("parallel"),
        name="merge_out",
    )(x, o_a, o_b, o_c, proj, proj, proj, proj, proj, proj, w_branch, w_out)


def _regroup_tables():
    names = ("aq", "ak", "av", "az", "iq", "ik", "iw", "bcq", "bckv", "bkpe", "bz", "cq", "cz", "gates")
    src_off = dict(zip(names, [0] + IN_OFFSETS))
    src_off["ikw"] = src_off["ik"]
    valid_rows = dict(ikw=IDX_DIM + IDX_HEADS, bkpe=B_ROPE)
    start, valid = [], []
    for name, units in SEG_UNITS.items():
        for u in range(units):
            start.append(src_off[name] + u * LANES)
            valid.append(valid_rows.get(name, LANES))
    return np.asarray(start, np.int32), np.asarray(valid, np.int32)


REGROUP_START, REGROUP_VALID = _regroup_tables()
REGROUP_ALIGN = math.gcd(*(int(v) for v in REGROUP_START if v))
assert REGROUP_ALIGN % SUBLANES == 0


REGROUP_UNITS_PER_STEP = 4
assert PROJ_UNITS % REGROUP_UNITS_PER_STEP == 0


def _regroup_kernel(start_ref, valid_ref, *refs):
    w_refs, o_ref = refs[:-1], refs[-1]
    row = lax.broadcasted_iota(I32, (LANES, o_ref.shape[1]), 0)
    for u, w_ref in enumerate(w_refs):
        valid = valid_ref[pl.program_id(0) * REGROUP_UNITS_PER_STEP + u]
        o_ref[u * LANES:(u + 1) * LANES, :] = jnp.where(row < valid, w_ref[...], 0.0).astype(BF16)


def _regroup_w_in_t(w_in_t):
    d = w_in_t.shape[1]
    ups = REGROUP_UNITS_PER_STEP

    def window(u):
        return pl.BlockSpec((pl.Element(LANES), pl.Element(d)),
                            lambda b, start, valid: (start[b * ups + u] * REGROUP_ALIGN, 0))

    grid_spec = pltpu.PrefetchScalarGridSpec(
        num_scalar_prefetch=2,
        grid=(PROJ_UNITS // ups,),
        in_specs=[window(u) for u in range(ups)],
        out_specs=pl.BlockSpec((ups * LANES, d), lambda b, *_: (b, 0)),
    )
    return pl.pallas_call(
        _regroup_kernel,
        grid_spec=grid_spec,
        out_shape=jax.ShapeDtypeStruct((PROJ_WIDTH, d), BF16),
        compiler_params=_cparams("arbitrary"),
        name="regroup_w_in",
    )(jnp.asarray(REGROUP_START // REGROUP_ALIGN), jnp.asarray(REGROUP_VALID), *([w_in_t] * ups))


def _pad_heads(w, n_heads, width, pad_to):
    r = w.shape[0]
    w = w.reshape(r, n_heads, width)
    w = jnp.pad(w, ((0, 0), (0, 0), (0, pad_to - width)))
    return w.reshape(r, n_heads * pad_to)


def kernel(x, mem, positions, rel_bias, norm_g, mem_norm_g, w_in, a_q_norm_g, a_k_norm_g, b_q_lat_norm_g,
           b_kv_lat_norm_g, w_b_uq, w_b_ukv, b_q_norm_g, b_k_norm_g, w_mem_kv, c_q_norm_g, c_k_norm_g,
           w_branch, w_out):
    bsz, seq, d = x.shape
    assert d == D_MODEL and norm_g.shape[0] == 1
    topk = min(TOPK_MAX, seq // 4)
    tq_a = 256
    tq_b = 512
    tm_in = min(1024, seq)

    inv_freq = 1.0 / (ROPE_THETA ** (jnp.arange(0, B_ROPE, 2, dtype=F32) / B_ROPE))
    zeros_half = jnp.zeros((B_ROPE,), F32)
    gq_pad = jnp.concatenate([b_q_norm_g[0], zeros_half])[None, :]
    gk_pad = jnp.concatenate([b_k_norm_g[0], zeros_half])[None, :]
    lut_t = ((rel_bias[jnp.asarray(T5_TABLE)] - rel_bias[REL_BUCKETS - 1]).T * LOG2E).astype(F32)

    w_cat = _regroup_w_in_t(jnp.transpose(w_in[0]))
    wuq = _pad_heads(w_b_uq[0], B_HEADS, B_QK_DIM, B_QK_PAD).astype(BF16)
    wukv = w_b_ukv[0].reshape(B_KV_RANK, B_HEADS, B_NOPE + B_V)
    wuk = wukv[:, :, :B_NOPE].reshape(B_KV_RANK, B_HEADS * B_NOPE).astype(BF16)
    wuvt = jnp.transpose(wukv[:, :, B_NOPE:], (1, 2, 0)).astype(BF16)
    wbr = w_branch[0].astype(BF16)
    wout = w_out[0].astype(BF16)

    outs = []
    for b in range(bsz):
        pos = positions[b]
        ang = pos.astype(F32)[:, None] * inv_freq
        cos, sin = jnp.cos(ang), jnp.sin(ang)
        zpad = jnp.zeros((seq, LANES - B_ROPE), F32)
        cos128 = jnp.concatenate([cos, cos, zpad], axis=1)
        sin128 = jnp.concatenate([-sin, sin, zpad], axis=1)

        mem_kv = _mem_kv(mem[b], mem_norm_g[0][None, :], w_mem_kv[0], c_k_norm_g[0][None, :])
        proj = _in_proj(x[b], norm_g[0][None, :], w_cat, tm=tm_in, tn=512)
        aqn, akn, kab, wt, avt, qb, kb, vbt, cqn, knorm2 = _prep(
            proj, cos128, sin128, a_q_norm_g[0][None, :], a_k_norm_g[0][None, :],
            b_q_lat_norm_g[0][None, :], b_kv_lat_norm_g[0][None, :], gq_pad, gk_pad,
            c_q_norm_g[0][None, :], wuq, wuk, wuvt)
        o_b = _attn_b(qb, kb, vbt, knorm2, tq=tq_b)
        qmin = jnp.min(pos.reshape(seq // tq_a, tq_a), axis=1)
        kmax = jnp.max(pos.reshape(seq // KV_TILE, KV_TILE), axis=1)
        o_a = _attn_a(proj, wt, kab, aqn, akn, avt, pos[None, :], pos[:, None],
                      lut_t, knorm2, qmin, kmax, tq=tq_a, topk=topk)
        o_c = _attn_c(cqn, mem_kv, tm=512)
        outs.append(_merge_out(x[b], o_a, o_b, o_c, proj, wbr, wout, tm=256))
    return jnp.stack(outs, axis=0)
```

```python
import functools
import math
from typing import NamedTuple

import numpy as np
import jax
import jax.numpy as jnp
from jax import lax
from jax.experimental import pallas as pl
from jax.experimental.pallas import tpu as pltpu

F32 = jnp.float32
BF16 = jnp.bfloat16
I32 = jnp.int32

D_MODEL = 2048
BRANCH_WIDTH = 1024
N_BRANCH = 3
A_HEADS = 8
A_KV_HEADS = 2
A_HEAD_DIM = 128
A_WIDTH = A_HEADS * A_HEAD_DIM
A_KV_WIDTH = A_KV_HEADS * A_HEAD_DIM
A_REP = A_HEADS // A_KV_HEADS
IDX_HEADS = 16
IDX_DIM = 64
TOPK_MAX = 256
IDX_SCALE = (IDX_DIM ** -0.5) * (IDX_HEADS ** -0.5)
A_SCALE = A_HEAD_DIM ** -0.5
B_HEADS = 8
B_Q_RANK = 512
B_KV_RANK = 256
B_NOPE = 128
B_ROPE = 64
B_QK_DIM = B_NOPE + B_ROPE
B_V = 128
B_WIDTH = B_HEADS * B_V
B_SCALE = B_QK_DIM ** -0.5
ROPE_THETA = 10000.0
C_HEADS = 4
C_HEAD_DIM = 256
C_WIDTH = C_HEADS * C_HEAD_DIM
C_SCALE = C_HEAD_DIM ** -0.5
REL_BUCKETS = 32
REL_MAX_DIST = 128
EPS = 1e-6

IN_SIZES = (A_WIDTH, A_KV_WIDTH, A_KV_WIDTH, A_WIDTH, IDX_HEADS * IDX_DIM, IDX_DIM, IDX_HEADS,
            B_Q_RANK, B_KV_RANK, B_ROPE, B_WIDTH, C_WIDTH, C_WIDTH, N_BRANCH * D_MODEL)
IN_OFFSETS = [int(o) for o in np.cumsum(IN_SIZES)[:-1]]

LANES = 128
SUBLANES = 8
B_QK_PAD = 2 * LANES
VMEM_LIMIT_BYTES = 56 * 1024 * 1024
MASK_VALUE = -1e30
LOG2E = math.log2(math.e)
SHIFT_MARGIN = 1.02
MAX_SHIFT_SPAN = 100.0
KNORM_ROWS = 16
DV = 128
ONES_ROWS = 16
DV_AUG = DV + ONES_ROWS
assert DV == A_HEAD_DIM == B_V
KV_TILE = 512
B_HEADS_PER_STEP = 2

SEG_UNITS = dict(gates=48, aq=8, az=8, iq=8, bz=8, cq=8, cz=8, bcq=4, ak=2, av=2, bckv=2, ikw=1, bkpe=1)
SEG_START = {}
_u = 0
for _name, _w in SEG_UNITS.items():
    SEG_START[_name] = _u
    _u += _w
PROJ_UNITS = _u
PROJ_WIDTH = PROJ_UNITS * LANES


def _t5_bucket_table():
    d = np.arange(LANES)
    max_exact = REL_BUCKETS // 2
    nf = np.maximum(d, 1).astype(np.float64)
    large = max_exact + (np.log(nf / max_exact) / math.log(REL_MAX_DIST / max_exact)
                         * (REL_BUCKETS - max_exact)).astype(np.int64)
    large = np.minimum(large, REL_BUCKETS - 1)
    table = np.where(d < max_exact, d, large).astype(np.int32)
    far = int(np.min(np.nonzero(table == REL_BUCKETS - 1)[0]))
    assert np.all(table[far:] == REL_BUCKETS - 1)
    return table, far


T5_TABLE, T5_FAR = _t5_bucket_table()


def _cparams(*sem):
    return pltpu.CompilerParams(dimension_semantics=sem, vmem_limit_bytes=VMEM_LIMIT_BYTES)


def _dot_t(a, b):
    return lax.dot_general(a, b, (((1,), (1,)), ((), ())), preferred_element_type=F32)


def _rms_scale(ss, n):
    return lax.rsqrt(ss * (1.0 / n) + EPS)


def _mem_kv_kernel(mem_ref, g_ref, w_ref, ckn_ref, o_ref, h_ref):
    j = pl.program_id(0)

    @pl.when(j == 0)
    def _():
        m = mem_ref[...]
        r = _rms_scale(jnp.sum(m * m, axis=-1, keepdims=True), D_MODEL)
        h_ref[...] = (m * r * g_ref[...]).astype(BF16)

    y = jnp.dot(h_ref[...], w_ref[...].astype(BF16), preferred_element_type=F32)

    @pl.when(j < C_HEADS)
    def _():
        r = _rms_scale(jnp.sum(y * y, axis=-1, keepdims=True), C_HEAD_DIM)
        o_ref[...] = (y * r * ckn_ref[...]).astype(BF16)

    @pl.when(j >= C_HEADS)
    def _():
        o_ref[...] = y.astype(BF16)


def _mem_kv(mem, mem_g, w_mem_kv, ck_g):
    n_mem = mem.shape[0]
    return pl.pallas_call(
        _mem_kv_kernel,
        grid=(2 * C_HEADS,),
        in_specs=[
            pl.BlockSpec((n_mem, D_MODEL), lambda j: (0, 0)),
            pl.BlockSpec((1, D_MODEL), lambda j: (0, 0)),
            pl.BlockSpec((D_MODEL, C_HEAD_DIM), lambda j: (0, j)),
            pl.BlockSpec((1, C_HEAD_DIM), lambda j: (0, 0)),
        ],
        out_specs=pl.BlockSpec((n_mem, C_HEAD_DIM), lambda j: (0, j)),
        out_shape=jax.ShapeDtypeStruct((n_mem, 2 * C_WIDTH), BF16),
        scratch_shapes=[pltpu.VMEM((n_mem, D_MODEL), BF16)],
        compiler_params=_cparams("arbitrary"),
        name="mem_kv",
    )(mem, mem_g, w_mem_kv, ck_g)


def _in_proj_kernel(x_ref, g_ref, w_ref, o_ref, h_ref, *, row_chunk):
    @pl.when(pl.program_id(1) == 0)
    def _():
        g = g_ref[...]
        for c in range(x_ref.shape[0] // row_chunk):
            rows = pl.ds(c * row_chunk, row_chunk)
            x = x_ref[rows, :]
            r = _rms_scale(jnp.sum(x * x, axis=-1, keepdims=True), D_MODEL)
            h_ref[rows, :] = (x * r * g).astype(BF16)

    o_ref[...] = _dot_t(h_ref[...], w_ref[...]).astype(o_ref.dtype)


def _in_proj(x, norm_g, w_cat, *, tm, tn):
    s = x.shape[0]
    return pl.pallas_call(
        functools.partial(_in_proj_kernel, row_chunk=min(tm, 256)),
        grid=(s // tm, PROJ_WIDTH // tn),
        in_specs=[
            pl.BlockSpec((tm, D_MODEL), lambda i, j: (i, 0)),
            pl.BlockSpec((1, D_MODEL), lambda i, j: (0, 0)),
            pl.BlockSpec((tn, D_MODEL), lambda i, j: (j, 0)),
        ],
        out_specs=pl.BlockSpec((tm, tn), lambda i, j: (i, j)),
        out_shape=jax.ShapeDtypeStruct((s, PROJ_WIDTH), BF16),
        scratch_shapes=[pltpu.VMEM((tm, D_MODEL), BF16)],
        compiler_params=_cparams("parallel", "arbitrary"),
        name="in_proj",
    )(x, norm_g, w_cat)


def _rope128(x, cos, sin):
    half = B_ROPE // 2
    lane = lax.broadcasted_iota(I32, x.shape, 1)
    partner = jnp.where(lane < half, pltpu.roll(x, LANES - half, 1), pltpu.roll(x, half, 1))
    return x * cos + partner * sin


def _prep_kernel(aq_ref, ak_ref, av_ref, ikw_ref, bcq_ref, bckv_ref, bkpe_ref, cq_ref,
                 cos_ref, sin_ref, aqg_ref, akg_ref, bqlg_ref, bkvlg_ref, bqg_ref, bkg_ref, cqg_ref,
                 wuq_ref, wuk_ref, wuvt_ref,
                 aqn_ref, akn_ref, kab_ref, wt_ref, avt_ref, qb_ref, kb_ref, vbt_ref, cqn_ref, kn_ref):
    cos = cos_ref[...]
    sin = sin_ref[...]

    @pl.when(pl.program_id(0) == 0)
    def _():
        kn_ref[...] = jnp.zeros(kn_ref.shape, F32)

    def note_key_norm(row, sq_norms):
        top = jnp.max(sq_norms, axis=0, keepdims=True)
        kn_ref[row:row + 1, :] = jnp.maximum(kn_ref[row:row + 1, :], top)

    eye = jnp.where(lax.broadcasted_iota(I32, (LANES, LANES), 0) == lax.broadcasted_iota(I32, (LANES, LANES), 1),
                    1.0, 0.0).astype(BF16)

    aqg = aqg_ref[...] * (A_SCALE * LOG2E)
    for h in range(A_HEADS):
        cols = slice(h * A_HEAD_DIM, (h + 1) * A_HEAD_DIM)
        v = aq_ref[:, cols].astype(F32)
        r = _rms_scale(jnp.sum(v * v, axis=-1, keepdims=True), A_HEAD_DIM)
        aqn_ref[:, cols] = (v * r * aqg).astype(BF16)
    akg = akg_ref[...]
    for h in range(A_KV_HEADS):
        cols = slice(h * A_HEAD_DIM, (h + 1) * A_HEAD_DIM)
        v = ak_ref[:, cols].astype(F32)
        r = _rms_scale(jnp.sum(v * v, axis=-1, keepdims=True), A_HEAD_DIM)
        kn = v * r * akg
        akn_ref[:, cols] = kn.astype(BF16)
        note_key_norm(B_HEADS + h, jnp.sum(kn * kn, axis=-1, keepdims=True))
        avt_ref[h, 0, 0:DV] = _dot_t(eye, av_ref[:, cols]).astype(BF16)
        avt_ref[h, 0, DV:DV_AUG] = jnp.ones((ONES_ROWS, avt_ref.shape[-1]), BF16)

    ikw = ikw_ref[...]
    lane = lax.broadcasted_iota(I32, ikw.shape, 1)
    zero = jnp.zeros_like(ikw)
    kab_ref[:, 0:LANES] = jnp.where(lane < IDX_DIM, ikw, zero)
    kab_ref[:, LANES:2 * LANES] = jnp.where(lane >= IDX_DIM, pltpu.roll(ikw.astype(F32), IDX_DIM, 1).astype(BF16), zero)
    wt_ref[...] = _dot_t(eye, ikw) * IDX_SCALE

    cq_lat = bcq_ref[...].astype(F32)
    r = _rms_scale(jnp.sum(cq_lat * cq_lat, axis=-1, keepdims=True), B_Q_RANK)
    qlat = (cq_lat * r * bqlg_ref[...]).astype(BF16)
    bqg = bqg_ref[...] * (B_SCALE * LOG2E)
    for h in range(B_HEADS):
        cols = slice(h * B_QK_PAD, (h + 1) * B_QK_PAD)
        qh = jnp.dot(qlat, wuq_ref[:, cols], preferred_element_type=F32)
        r = _rms_scale(jnp.sum(qh * qh, axis=-1, keepdims=True), B_QK_DIM)
        qn = qh * r * bqg
        qb_ref[:, h * B_QK_PAD:h * B_QK_PAD + LANES] = qn[:, :LANES].astype(BF16)
        qb_ref[:, h * B_QK_PAD + LANES:(h + 1) * B_QK_PAD] = _rope128(qn[:, LANES:], cos, sin).astype(BF16)

    ckv = bckv_ref[...].astype(F32)
    r = _rms_scale(jnp.sum(ckv * ckv, axis=-1, keepdims=True), B_KV_RANK)
    kvlat = (ckv * r * bkvlg_ref[...]).astype(BF16)
    kpe = bkpe_ref[...].astype(F32)
    ss_pe = jnp.sum(kpe * kpe, axis=-1, keepdims=True)
    bkg = bkg_ref[...]
    kpe_rot = _rope128(kpe * bkg[:, LANES:], cos, sin)
    for h in range(B_HEADS):
        kn = jnp.dot(kvlat, wuk_ref[:, h * B_NOPE:(h + 1) * B_NOPE], preferred_element_type=F32)
        r = _rms_scale(jnp.sum(kn * kn, axis=-1, keepdims=True) + ss_pe, B_QK_DIM)
        k_nope = kn * r * bkg[:, :LANES]
        k_pe = kpe_rot * r
        kb_ref[:, h * B_QK_PAD:h * B_QK_PAD + LANES] = k_nope.astype(BF16)
        kb_ref[:, h * B_QK_PAD + LANES:(h + 1) * B_QK_PAD] = k_pe.astype(BF16)
        note_key_norm(h, jnp.sum(k_nope * k_nope, axis=-1, keepdims=True)
                      + jnp.sum(k_pe * k_pe, axis=-1, keepdims=True))
        vbt_ref[h, 0, 0:DV] = _dot_t(wuvt_ref[h], kvlat).astype(BF16)
        vbt_ref[h, 0, DV:DV_AUG] = jnp.ones((ONES_ROWS, vbt_ref.shape[-1]), BF16)

    cqg = cqg_ref[...] * C_SCALE
    for h in range(C_HEADS):
        cols = slice(h * C_HEAD_DIM, (h + 1) * C_HEAD_DIM)
        v = cq_ref[:, cols].astype(F32)
        r = _rms_scale(jnp.sum(v * v, axis=-1, keepdims=True), C_HEAD_DIM)
        cqn_ref[:, cols] = (v * r * cqg).astype(BF16)


def _seg_spec(tm, name):
    units = SEG_UNITS[name]
    blk = SEG_START[name] // units
    return pl.BlockSpec((tm, units * LANES), lambda i: (i, blk))


def _full_spec(shape):
    return pl.BlockSpec(shape, lambda i: (0,) * len(shape))


def _prep(proj, cos128, sin128, aqg, akg, bqlg, bkvlg, bqg, bkg, cqg, wuq, wuk, wuvt):
    s = proj.shape[0]
    tm = KV_TILE
    nt = s // tm
    row = lambda w: pl.BlockSpec((tm, w), lambda i: (i, 0))
    tiles_t = lambda n: pl.BlockSpec((n, 1, DV_AUG, tm), lambda i: (0, i, 0, 0))
    small = [aqg, akg, bqlg, bkvlg, bqg, bkg, cqg, wuq, wuk, wuvt]
    out_specs = [row(A_WIDTH), row(A_KV_WIDTH), row(2 * LANES),
                 pl.BlockSpec((LANES, tm), lambda i: (0, i)), tiles_t(A_KV_HEADS),
                 row(B_HEADS * B_QK_PAD), row(B_HEADS * B_QK_PAD), tiles_t(B_HEADS), row(C_WIDTH),
                 _full_spec((KNORM_ROWS, LANES))]
    out_shape = [
        jax.ShapeDtypeStruct((s, A_WIDTH), BF16),
        jax.ShapeDtypeStruct((s, A_KV_WIDTH), BF16),
        jax.ShapeDtypeStruct((s, 2 * LANES), BF16),
        jax.ShapeDtypeStruct((LANES, s), F32),
        jax.ShapeDtypeStruct((A_KV_HEADS, nt, DV_AUG, tm), BF16),
        jax.ShapeDtypeStruct((s, B_HEADS * B_QK_PAD), BF16),
        jax.ShapeDtypeStruct((s, B_HEADS * B_QK_PAD), BF16),
        jax.ShapeDtypeStruct((B_HEADS, nt, DV_AUG, tm), BF16),
        jax.ShapeDtypeStruct((s, C_WIDTH), BF16),
        jax.ShapeDtypeStruct((KNORM_ROWS, LANES), F32),
    ]
    return pl.pallas_call(
        _prep_kernel,
        grid=(nt,),
        in_specs=[_seg_spec(tm, n) for n in ("aq", "ak", "av", "ikw", "bcq", "bckv", "bkpe", "cq")]
        + [row(LANES), row(LANES)] + [_full_spec(a.shape) for a in small],
        out_specs=out_specs,
        out_shape=out_shape,
        compiler_params=_cparams("arbitrary"),
        name="prep",
    )(*([proj] * 8), cos128, sin128, *small)


def _reduce_keys(x, op):
    tk, tq = x.shape
    n = tk // SUBLANES
    assert n & (n - 1) == 0
    t = x.reshape(n, SUBLANES, tq)
    while n > 1:
        n //= 2
        t = op(t[:n], t[n:2 * n])
    red = jnp.max if op is jnp.maximum else jnp.sum
    return red(t[0], axis=0, keepdims=True)


def _softmax_step_t(st, vt, m_ref, acc_ref):
    m_prev = m_ref[...]
    m_new = jnp.maximum(m_prev, _reduce_keys(st, jnp.maximum))
    alpha = jnp.exp2(m_prev - m_new)
    p = jnp.exp2(st - m_new).astype(BF16)
    acc_ref[...] = alpha * acc_ref[...] + jnp.dot(vt, p, preferred_element_type=F32)
    m_ref[...] = m_new


def _fixed_shift_step_t(st, vt, shift, acc_ref):
    p = jnp.exp2(st - shift).astype(BF16)
    acc_ref[...] += jnp.dot(vt, p, preferred_element_type=F32)


def _normalised_output(acc):
    return (acc[0:DV] / acc[DV:DV + 1]).T


def _logit_bound(q, kmax2):
    qf = q.astype(F32)
    ones = jnp.ones((SUBLANES, q.shape[1]), BF16)
    qn2 = _dot_t(ones, (qf * qf).astype(BF16))[0:1]
    return jnp.sqrt(qn2 * kmax2) * SHIFT_MARGIN


def _causal_ok_t(i, j, tq, tk):
    key = j * tk + lax.broadcasted_iota(I32, (tk, tq), 0)
    qry = i * tq + lax.broadcasted_iota(I32, (tk, tq), 1)
    return key <= qry


def _init_softmax_state(m_ref, acc_ref):
    m_ref[...] = jnp.full(m_ref.shape, MASK_VALUE, F32)
    acc_ref[...] = jnp.zeros(acc_ref.shape, F32)


def _pingpong_tiles(n_full, logits, consume, buf_a, buf_b):
    logits(0, buf_a)

    def pair(p, c):
        j = 2 * p
        logits(j + 1, buf_b)
        consume(j, buf_a, False)
        logits(j + 2, buf_a)
        consume(j + 1, buf_b, False)
        return c

    lax.fori_loop(0, n_full // 2, pair, 0)
    odd = n_full % 2 == 1

    @pl.when(odd)
    def _():
        logits(n_full, buf_b)
        consume(n_full - 1, buf_a, False)
        consume(n_full, buf_b, True)

    @pl.when(jnp.logical_not(odd))
    def _():
        consume(n_full, buf_a, True)


def _attn_b_kernel(q_ref, k_ref, vt_ref, kn_ref, o_ref, sa_ref, sb_ref, m_ref, acc_ref, *, tq, tk):
    hp = pl.program_id(0)
    i = pl.program_id(1)
    _init_softmax_state(m_ref, acc_ref)
    j_diag = (i * tq) // tk
    heads = range(B_HEADS_PER_STEP)
    shifts = [_logit_bound(q_ref[:, hh * B_QK_PAD:(hh + 1) * B_QK_PAD],
                           kn_ref[pl.ds(hp * B_HEADS_PER_STEP + hh, 1), 0:1]) for hh in heads]
    fixed_ok = jnp.max(2.0 * functools.reduce(jnp.maximum, shifts)) <= MAX_SHIFT_SPAN

    def logits(j, buf):
        rows = pl.ds(pl.multiple_of(j * tk, tk), tk)
        for hh in heads:
            cols = slice(hh * B_QK_PAD, (hh + 1) * B_QK_PAD)
            buf[hh] = _dot_t(k_ref[rows, cols], q_ref[:, cols])

    def consume(j, buf, masked, fixed):
        for hh in heads:
            st = buf[hh]
            if masked:
                st = jnp.where(_causal_ok_t(i, j, tq, tk), st, MASK_VALUE)
            if fixed:
                _fixed_shift_step_t(st, vt_ref[hh, j], shifts[hh], acc_ref.at[hh])
            else:
                _softmax_step_t(st, vt_ref[hh, j], m_ref.at[hh], acc_ref.at[hh])

    @pl.when(fixed_ok)
    def _():
        _pingpong_tiles(j_diag, logits, functools.partial(consume, fixed=True), sa_ref, sb_ref)

    @pl.when(jnp.logical_not(fixed_ok))
    def _():
        _pingpong_tiles(j_diag, logits, functools.partial(consume, fixed=False), sa_ref, sb_ref)

    for hh in heads:
        o_ref[:, hh * B_V:(hh + 1) * B_V] = _normalised_output(acc_ref[hh]).astype(o_ref.dtype)


def _attn_b(qb, kb, vbt, knorm2, *, tq):
    s = qb.shape[0]
    tk = KV_TILE
    hps = B_HEADS_PER_STEP
    assert tk % tq == 0 or tq % tk == 0
    assert tq <= tk
    return pl.pallas_call(
        functools.partial(_attn_b_kernel, tq=tq, tk=tk),
        grid=(B_HEADS // hps, s // tq),
        in_specs=[
            pl.BlockSpec((tq, hps * B_QK_PAD), lambda h, i: (i, h)),
            pl.BlockSpec((s, hps * B_QK_PAD), lambda h, i: (0, h)),
            pl.BlockSpec((hps, s // tk, DV_AUG, tk), lambda h, i: (h, 0, 0, 0)),
            pl.BlockSpec((KNORM_ROWS, LANES), lambda h, i: (0, 0)),
        ],
        out_specs=pl.BlockSpec((tq, hps * B_V), lambda h, i: (i, h)),
        out_shape=jax.ShapeDtypeStruct((s, B_WIDTH), BF16),
        scratch_shapes=[pltpu.VMEM((hps, tk, tq), F32), pltpu.VMEM((hps, tk, tq), F32),
                        pltpu.VMEM((hps, 1, tq), F32), pltpu.VMEM((hps, DV_AUG, tq), F32)],
        compiler_params=_cparams("parallel", "arbitrary"),
        name="attn_b",
    )(qb, kb, vbt, knorm2)


KEY_BITS = 32
COARSE_BITS = 16
BISECT_GROUP = 4
PACKED_SUBLANES = 16
KEY_NEG_INF = (0xFF800000 ^ 0x7FFFFFFF) - (1 << 32)
BRACKET_HALF = (1 << (KEY_BITS - COARSE_BITS - 1)) + 1
BRACKET_STEPS = (KEY_BITS - COARSE_BITS + 1 + BISECT_GROUP) // BISECT_GROUP * BISECT_GROUP
assert COARSE_BITS % BISECT_GROUP == 0 and KEY_BITS % BISECT_GROUP == 0
assert (1 << BRACKET_STEPS) > 2 * BRACKET_HALF + 1


def _key_as_f32(key):
    bits = key ^ (lax.shift_right_arithmetic(key, 31) & 0x7FFFFFFF)
    return lax.bitcast_convert_type(bits, F32)


def _coarse_key_as_bf16(k):
    b = k ^ (lax.shift_right_arithmetic(k, COARSE_BITS - 1) & ((1 << (COARSE_BITS - 1)) - 1))
    return lax.bitcast_convert_type(lax.shift_left(b, COARSE_BITS), F32).astype(BF16)


def _count_hits_packed(hit):
    tk, tq = hit.shape
    n = tk // PACKED_SUBLANES
    assert n <= 256
    t = jnp.where(hit, jnp.ones((), BF16), jnp.zeros((), BF16)).reshape(n, PACKED_SUBLANES, tq)
    while n > 1:
        n //= 2
        t = t[:n] + t[n:2 * n]
    return t[0].astype(F32)


def _count_hits(hit):
    tk, tq = hit.shape
    n = tk // SUBLANES
    t = jnp.where(hit, 1, 0).reshape(n, SUBLANES, tq)
    while n > 1:
        n //= 2
        t = t[:n] + t[n:2 * n]
    return t[0]


def _attn_a_kernel(qmin_ref, kmax_ref,
                   iq_ref, wt_ref, kab_ref, aqn_ref, akn_ref, avt_ref, pq_ref, pk_ref, lut_ref, kn_ref,
                   o_ref,
                   score_ref, coarse_ref, thr_ref, settled_ref, sa_ref, sb_ref, m_ref, acc_ref, *, tq, tk, topk):
    i = pl.program_id(0)
    j_diag = (i * tq) // tk
    n_tiles = j_diag + 1
    int_min = jnp.int32(-2 ** 31)

    def head_pair_dots(j, p):
        rows = pl.ds(pl.multiple_of(j * tk, tk), tk)
        rhs = iq_ref[:, p * LANES:(p + 1) * LANES]
        return _dot_t(kab_ref[rows, 0:LANES], rhs), _dot_t(kab_ref[rows, LANES:2 * LANES], rhs)

    def score_tile(j, masked):
        acc = jnp.zeros((tk, tq), F32)
        n_pairs = IDX_HEADS // 2
        for p in range(n_pairs):
            d_lo, d_hi = (sa_ref[...], sb_ref[...]) if p == 0 else head_pair_dots(j, p)
            if p == n_pairs - 1 and not masked:
                sa_ref[...], sb_ref[...] = head_pair_dots(j + 1, 0)
            w_row = IDX_DIM + 2 * p
            acc = acc + jnp.maximum(d_lo, 0.0) * wt_ref[w_row:w_row + 1, :]
            acc = acc + jnp.maximum(d_hi, 0.0) * wt_ref[w_row + 1:w_row + 2, :]
        if masked:
            acc = jnp.where(_causal_ok_t(i, j, tq, tk), acc, -jnp.inf)
        score_ref[j] = acc
        coarse_ref[j] = acc.astype(BF16)

    def score_body(j, c):
        score_tile(j, False)
        return c

    sa_ref[...], sb_ref[...] = head_pair_dots(0, 0)
    lax.fori_loop(0, j_diag, score_body, 0)
    score_tile(j_diag, True)

    def count_queries(pred_fn):
        def body(j, cnt):
            return cnt + _count_hits(pred_fn(score_ref[j], j))
        cnt = lax.fori_loop(0, n_tiles, body, jnp.zeros((SUBLANES, tq), I32))
        return jnp.sum(cnt, axis=0, keepdims=True)

    def count_at_or_above(cand_key):
        cand = _key_as_f32(cand_key)
        cnt = count_queries(lambda x, j: x >= cand)
        return jnp.where(cand_key <= KEY_NEG_INF, n_tiles * tk, cnt)

    def store_threshold(thr_key, settled):
        thr_ref[...] = jnp.where(thr_key <= KEY_NEG_INF, -jnp.inf, _key_as_f32(thr_key))
        settled_ref[...] = settled

    def bisect_all_bits():
        def cond(c):
            g, _, s = c
            return (g < KEY_BITS // BISECT_GROUP) & (jnp.min(s) == 0)

        def body(c):
            g, t, s = c
            for u in range(BISECT_GROUP):
                bit = KEY_BITS - 1 - (g * BISECT_GROUP + u)
                cand = t + lax.shift_left(jnp.int32(1), bit)
                cnt = count_at_or_above(cand)
                t = jnp.where((s == 0) & (cnt >= topk), cand, t)
                s = jnp.where(cnt == topk, 1, s)
            return g + 1, t, s

        _, t, s = lax.while_loop(cond, body, (jnp.int32(0), jnp.full((1, tq), int_min, I32),
                                              jnp.zeros((1, tq), I32)))
        store_threshold(t, s)

    def bisect_bracket(lo, hi, cnt_lo):
        def cond(c):
            g, lo, hi, _, s = c
            done = (s == 1) | (hi - lo <= 1)
            return (g < BRACKET_STEPS // BISECT_GROUP) & (jnp.min(jnp.where(done, 1, 0)) == 0)

        def body(c):
            g, lo, hi, t, s = c
            for _ in range(BISECT_GROUP):
                mid = lo + lax.shift_right_arithmetic(hi - lo, 1)
                cnt = count_at_or_above(mid)
                t = jnp.where((s == 0) & (cnt == topk), mid, t)
                s = jnp.where(cnt == topk, 1, s)
                lo, hi = jnp.where(cnt >= topk, mid, lo), jnp.where(cnt >= topk, hi, mid)
            return g + 1, lo, hi, t, s

        settled0 = jnp.where(cnt_lo == topk, 1, 0)
        _, lo, _, t, s = lax.while_loop(cond, body, (jnp.int32(0), lo, hi, lo, settled0))
        store_threshold(jnp.where(s == 1, t, lo), s)

    def count_coarse(cand):
        def body(j, cnt):
            return cnt + _count_hits_packed(coarse_ref[j] >= cand)
        cnt = lax.fori_loop(0, n_tiles, body, jnp.zeros((PACKED_SUBLANES, tq), F32))
        return jnp.sum(cnt, axis=0, keepdims=True)

    def coarse_step(it, t):
        cand = t + lax.shift_left(jnp.int32(1), COARSE_BITS - 1 - it)
        return jnp.where(count_coarse(_coarse_key_as_bf16(cand)) >= topk, cand, t)

    t_coarse = lax.fori_loop(0, COARSE_BITS, coarse_step,
                             jnp.full((1, tq), -(1 << (COARSE_BITS - 1)), I32))

    low_bits = KEY_BITS - COARSE_BITS
    centre = lax.shift_left(t_coarse, low_bits) + jnp.where(t_coarse < 0, (1 << low_bits) - 1, 0)
    lo_key, hi_key = centre - BRACKET_HALF, centre + BRACKET_HALF + 1
    cnt_lo = count_at_or_above(lo_key)
    cnt_hi = count_at_or_above(hi_key)
    bracket_ok = (cnt_lo >= topk) & (cnt_hi < topk) & (lo_key < hi_key)
    coarse_ok = jnp.min(jnp.where(bracket_ok, 1, 0)) == 1

    pl.when(coarse_ok)(lambda: bisect_bracket(lo_key, hi_key, cnt_lo))
    pl.when(jnp.logical_not(coarse_ok))(bisect_all_bits)

    thr = thr_ref[...]
    settled = settled_ref[...]

    def drop_excess_ties():
        cnt_gt = count_queries(lambda x, j: x > thr)
        need = topk - cnt_gt
        kpos = lax.broadcasted_iota(I32, (tk, tq), 0)

        def pos_body(it, cut):
            cand = cut + lax.shift_left(jnp.int32(1), 30 - it)
            cnt = count_queries(lambda x, j: (x == thr) & (j * tk + kpos < cand))
            return jnp.where(cnt < need, cand, cut)

        cut = lax.fori_loop(0, 31, pos_body, jnp.zeros((1, tq), I32))

        def drop_body(j, c):
            x = score_ref[j]
            drop = (x == thr) & (j * tk + kpos > cut)
            score_ref[j] = jnp.where(drop, -jnp.inf, x)
            return c

        lax.fori_loop(0, n_tiles, drop_body, 0)

    @pl.when(jnp.min(settled) == 0)
    def _():
        cnt_ge = count_queries(lambda x, j: x >= thr)
        pl.when(jnp.max(cnt_ge) > topk)(drop_excess_ties)

    _init_softmax_state(m_ref, acc_ref)
    pq = pq_ref[...]
    lut = lut_ref[...]
    bias_hi = jnp.max(lut, axis=1, keepdims=True)
    bias_span = bias_hi - jnp.min(lut, axis=1, keepdims=True)
    bounds = [_logit_bound(aqn_ref[:, h * A_HEAD_DIM:(h + 1) * A_HEAD_DIM],
                           kn_ref[B_HEADS + h // A_REP:B_HEADS + h // A_REP + 1, 0:1]) for h in range(A_HEADS)]
    shifts = [bounds[h] + bias_hi[h:h + 1] for h in range(A_HEADS)]
    spans = [2.0 * bounds[h] + bias_span[h:h + 1] for h in range(A_HEADS)]
    fixed_ok = jnp.max(functools.reduce(jnp.maximum, spans)) <= MAX_SHIFT_SPAN

    def head_buf(h):
        return (sa_ref, sb_ref)[h % 2]

    def logits(j, h, buf):
        rows = pl.ds(pl.multiple_of(j * tk, tk), tk)
        g = h // A_REP
        buf[...] = _dot_t(akn_ref[rows, g * A_HEAD_DIM:(g + 1) * A_HEAD_DIM],
                          aqn_ref[:, h * A_HEAD_DIM:(h + 1) * A_HEAD_DIM])

    def attend(j, masked, near, fixed):
        rows = pl.ds(pl.multiple_of(j * tk, tk), tk)
        sel = score_ref[j] >= thr
        if masked:
            sel = sel & _causal_ok_t(i, j, tq, tk)
        mask_bias = jnp.where(sel, 0.0, MASK_VALUE)
        if near:
            dist = jnp.clip(pq - pk_ref[rows, :], 0, LANES - 1)

        logits(j, 0, head_buf(0))
        for h in range(A_HEADS):
            if h + 1 < A_HEADS:
                logits(j, h + 1, head_buf(h + 1))
            st = head_buf(h)[...] + mask_bias
            if near:
                table = jnp.broadcast_to(lut_ref[h:h + 1, :], (tk, LANES))
                st = st + jnp.concatenate(
                    [jnp.take_along_axis(table, dist[:, c * LANES:(c + 1) * LANES], axis=1,
                                         mode="promise_in_bounds")
                     for c in range(tq // LANES)], axis=1)
            if fixed:
                _fixed_shift_step_t(st, avt_ref[h // A_REP, j], shifts[h], acc_ref.at[h])
            else:
                _softmax_step_t(st, avt_ref[h // A_REP, j], m_ref.at[h], acc_ref.at[h])

    def attend_dyn(j, masked, fixed):
        far = qmin_ref[i] - kmax_ref[j] >= T5_FAR

        @pl.when(far)
        def _():
            attend(j, masked, False, fixed)

        @pl.when(jnp.logical_not(far))
        def _():
            attend(j, masked, True, fixed)

    def attend_all(fixed):
        def attend_body(j, c):
            attend_dyn(j, False, fixed)
            return c

        lax.fori_loop(0, j_diag, attend_body, 0)
        attend_dyn(j_diag, True, fixed)

    pl.when(fixed_ok)(functools.partial(attend_all, True))
    pl.when(jnp.logical_not(fixed_ok))(functools.partial(attend_all, False))

    for h in range(A_HEADS):
        q_cols = slice(h * A_HEAD_DIM, (h + 1) * A_HEAD_DIM)
        o_ref[:, q_cols] = _normalised_output(acc_ref[h]).astype(o_ref.dtype)


def _attn_a(proj, wt, kab, aqn, akn, avt, pos_row, pos_col, lut_t, knorm2, qmin, kmax, *, tq, topk):
    s = proj.shape[0]
    tk = KV_TILE
    assert tk % tq == 0
    iq_blk = SEG_START["iq"] // SEG_UNITS["iq"]
    grid_spec = pltpu.PrefetchScalarGridSpec(
        num_scalar_prefetch=2,
        grid=(s // tq,),
        in_specs=[
            pl.BlockSpec((tq, IDX_HEADS * IDX_DIM), lambda i, *_: (i, iq_blk)),
            pl.BlockSpec((LANES, tq), lambda i, *_: (0, i)),
            pl.BlockSpec((s, 2 * LANES), lambda i, *_: (0, 0)),
            pl.BlockSpec((tq, A_WIDTH), lambda i, *_: (i, 0)),
            pl.BlockSpec((s, A_KV_WIDTH), lambda i, *_: (0, 0)),
            pl.BlockSpec((A_KV_HEADS, s // tk, DV_AUG, tk), lambda i, *_: (0, 0, 0, 0)),
            pl.BlockSpec((1, tq), lambda i, *_: (0, i)),
            pl.BlockSpec((s, 1), lambda i, *_: (0, 0)),
            pl.BlockSpec((A_HEADS, LANES), lambda i, *_: (0, 0)),
            pl.BlockSpec((KNORM_ROWS, LANES), lambda i, *_: (0, 0)),
        ],
        out_specs=pl.BlockSpec((tq, A_WIDTH), lambda i, *_: (i, 0)),
        scratch_shapes=[
            pltpu.VMEM((s // tk, tk, tq), F32),
            pltpu.VMEM((s // tk, tk, tq), BF16),
            pltpu.VMEM((1, tq), F32),
            pltpu.VMEM((1, tq), I32),
            pltpu.VMEM((tk, tq), F32),
            pltpu.VMEM((tk, tq), F32),
            pltpu.VMEM((A_HEADS, 1, tq), F32),
            pltpu.VMEM((A_HEADS, DV_AUG, tq), F32),
        ],
    )
    return pl.pallas_call(
        functools.partial(_attn_a_kernel, tq=tq, tk=tk, topk=topk),
        grid_spec=grid_spec,
        out_shape=jax.ShapeDtypeStruct((s, A_WIDTH), BF16),
        compiler_params=_cparams("arbitrary"),
        name="attn_a",
    )(qmin, kmax, proj, wt, kab, aqn, akn, avt, pos_row, pos_col, lut_t, knorm2)


def _attn_c_kernel(q_ref, kv_ref, o_ref):
    for h in range(C_HEADS):
        cols = slice(h * C_HEAD_DIM, (h + 1) * C_HEAD_DIM)
        s = _dot_t(q_ref[:, cols], kv_ref[:, cols])
        p = jnp.exp(s - jnp.max(s, axis=-1, keepdims=True))
        o = jnp.dot(p.astype(BF16), kv_ref[:, C_WIDTH + h * C_HEAD_DIM:C_WIDTH + (h + 1) * C_HEAD_DIM],
                    preferred_element_type=F32)
        o_ref[:, cols] = (o / jnp.sum(p, axis=-1, keepdims=True)).astype(o_ref.dtype)


def _attn_c(cqn, mem_kv, *, tm):
    s = cqn.shape[0]
    return pl.pallas_call(
        _attn_c_kernel,
        grid=(s // tm,),
        in_specs=[pl.BlockSpec((tm, C_WIDTH), lambda i: (i, 0)), _full_spec(mem_kv.shape)],
        out_specs=pl.BlockSpec((tm, C_WIDTH), lambda i: (i, 0)),
        out_shape=jax.ShapeDtypeStruct((s, C_WIDTH), BF16),
        compiler_params=_cparams("parallel"),
        name="attn_c",
    )(cqn, mem_kv)


def _merge_out_kernel(x_ref, oa_ref, ob_ref, oc_ref, az_ref, bz_ref, cz_ref, ga_ref, gb_ref, gc_ref,
                      wbr_ref, wout_ref, o_ref):
    merged = None
    for n, (o_r, z_r, g_r) in enumerate(((oa_ref, az_ref, ga_ref), (ob_ref, bz_ref, gb_ref),
                                         (oc_ref, cz_ref, gc_ref))):
        z = z_r[...].astype(F32)
        u = (o_r[...].astype(F32) * (z * jax.nn.sigmoid(z))).astype(BF16)
        y = jnp.dot(u, wbr_ref[n], preferred_element_type=F32)
        t = jax.nn.sigmoid(g_r[...].astype(F32)) * y
        merged = t if merged is None else merged + t
    o_ref[...] = x_ref[...] + jnp.dot(merged.astype(BF16), wout_ref[...], preferred_element_type=F32)


def _merge_out(x, o_a, o_b, o_c, proj, w_branch, w_out, *, tm):
    s = x.shape[0]
    row = lambda w: pl.BlockSpec((tm, w), lambda i: (i, 0))
    gate_blk = SEG_START["gates"] * LANES // D_MODEL
    gate = lambda n: pl.BlockSpec((tm, D_MODEL), lambda i: (i, gate_blk + n))
    single = pl.Buffered(1)
    return pl.pallas_call(
        _merge_out_kernel,
        grid=(s // tm,),
        in_specs=[row(D_MODEL), row(BRANCH_WIDTH), row(BRANCH_WIDTH), row(BRANCH_WIDTH),
                  _seg_spec(tm, "az"), _seg_spec(tm, "bz"), _seg_spec(tm, "cz"),
                  gate(0), gate(1), gate(2),
                  pl.BlockSpec(w_branch.shape, lambda i: (0, 0, 0), pipeline_mode=single),
                  pl.BlockSpec(w_out.shape, lambda i: (0, 0), pipeline_mode=single)],
        out_specs=row(D_MODEL),
        out_shape=jax.ShapeDtypeStruct((s, D_MODEL), F32),
        compiler_params=_cparams("parallel"),
        name="merge_out",
    )(x, o_a, o_b, o_c, proj, proj, proj, proj, proj, proj, w_branch, w_out)


def _regroup_tables():
    names = ("aq", "ak", "av", "az", "iq", "ik", "iw", "bcq", "bckv", "bkpe", "bz", "cq", "cz", "gates")
    src_off = dict(zip(names, [0] + IN_OFFSETS))
    src_off["ikw"] = src_off["ik"]
    valid_rows = dict(ikw=IDX_DIM + IDX_HEADS, bkpe=B_ROPE)
    start, valid = [], []
    for name, units in SEG_UNITS.items():
        for u in range(units):
            start.append(src_off[name] + u * LANES)
            valid.append(valid_rows.get(name, LANES))
    return np.asarray(start, np.int32), np.asarray(valid, np.int32)


REGROUP_START, REGROUP_VALID = _regroup_tables()
REGROUP_ALIGN = math.gcd(*(int(v) for v in REGROUP_START if v))
assert REGROUP_ALIGN % SUBLANES == 0


REGROUP_UNITS_PER_STEP = 4
assert PROJ_UNITS % REGROUP_UNITS_PER_STEP == 0


def _regroup_kernel(start_ref, valid_ref, *refs):
    w_refs, o_ref = refs[:-1], refs[-1]
    row = lax.broadcasted_iota(I32, (LANES, o_ref.shape[1]), 0)
    for u, w_ref in enumerate(w_refs):
        valid = valid_ref[pl.program_id(0) * REGROUP_UNITS_PER_STEP + u]
        o_ref[u * LANES:(u + 1) * LANES, :] = jnp.where(row < valid, w_ref[...], 0.0).astype(BF16)


def _regroup_w_in_t(w_in_t):
    d = w_in_t.shape[1]
    ups = REGROUP_UNITS_PER_STEP

    def window(u):
        return pl.BlockSpec((pl.Element(LANES), pl.Element(d)),
                            lambda b, start, valid: (start[b * ups + u] * REGROUP_ALIGN, 0))

    grid_spec = pltpu.PrefetchScalarGridSpec(
        num_scalar_prefetch=2,
        grid=(PROJ_UNITS // ups,),
        in_specs=[window(u) for u in range(ups)],
        out_specs=pl.BlockSpec((ups * LANES, d), lambda b, *_: (b, 0)),
    )
    return pl.pallas_call(
        _regroup_kernel,
        grid_spec=grid_spec,
        out_shape=jax.ShapeDtypeStruct((PROJ_WIDTH, d), BF16),
        compiler_params=_cparams("arbitrary"),
        name="regroup_w_in",
    )(jnp.asarray(REGROUP_START // REGROUP_ALIGN), jnp.asarray(REGROUP_VALID), *([w_in_t] * ups))


def _pad_heads(w, n_heads, width, pad_to):
    r = w.shape[0]
    w = w.reshape(r, n_heads, width)
    w = jnp.pad(w, ((0, 0), (0, 0), (0, pad_to - width)))
    return w.reshape(r, n_heads * pad_to)


class _TilePlan(NamedTuple):
    in_proj_rows: int
    in_proj_cols: int
    attn_a_q: int
    attn_b_q: int
    attn_c_rows: int
    merge_rows: int


def _tile_plan(seq):
    plan = _TilePlan(in_proj_rows=min(1024, seq), in_proj_cols=1536, attn_a_q=256, attn_b_q=min(512, seq),
                     attn_c_rows=min(512, seq), merge_rows=min(256, seq))
    assert seq % KV_TILE == 0 and PROJ_WIDTH % plan.in_proj_cols == 0
    assert all(seq % t == 0 for t in (plan.in_proj_rows, plan.attn_a_q, plan.attn_b_q, plan.attn_c_rows,
                                      plan.merge_rows))
    return plan


def kernel(x, mem, positions, rel_bias, norm_g, mem_norm_g, w_in, a_q_norm_g, a_k_norm_g, b_q_lat_norm_g,
           b_kv_lat_norm_g, w_b_uq, w_b_ukv, b_q_norm_g, b_k_norm_g, w_mem_kv, c_q_norm_g, c_k_norm_g,
           w_branch, w_out):
    bsz, seq, d = x.shape
    assert d == D_MODEL and norm_g.shape[0] == 1
    topk = min(TOPK_MAX, seq // 4)
    tiles = _tile_plan(seq)
    tq_a, tq_b, tm_in = tiles.attn_a_q, tiles.attn_b_q, tiles.in_proj_rows

    inv_freq = 1.0 / (ROPE_THETA ** (jnp.arange(0, B_ROPE, 2, dtype=F32) / B_ROPE))
    zeros_half = jnp.zeros((B_ROPE,), F32)
    gq_pad = jnp.concatenate([b_q_norm_g[0], zeros_half])[None, :]
    gk_pad = jnp.concatenate([b_k_norm_g[0], zeros_half])[None, :]
    lut_t = ((rel_bias[jnp.asarray(T5_TABLE)] - rel_bias[REL_BUCKETS - 1]).T * LOG2E).astype(F32)

    w_cat = _regroup_w_in_t(jnp.transpose(w_in[0]))
    wuq = _pad_heads(w_b_uq[0], B_HEADS, B_QK_DIM, B_QK_PAD).astype(BF16)
    wukv = w_b_ukv[0].reshape(B_KV_RANK, B_HEADS, B_NOPE + B_V)
    wuk = wukv[:, :, :B_NOPE].reshape(B_KV_RANK, B_HEADS * B_NOPE).astype(BF16)
    wuvt = jnp.transpose(wukv[:, :, B_NOPE:], (1, 2, 0)).astype(BF16)
    wbr = w_branch[0].astype(BF16)
    wout = w_out[0].astype(BF16)

    outs = []
    for b in range(bsz):
        pos = positions[b]
        ang = pos.astype(F32)[:, None] * inv_freq
        cos, sin = jnp.cos(ang), jnp.sin(ang)
        zpad = jnp.zeros((seq, LANES - B_ROPE), F32)
        cos128 = jnp.concatenate([cos, cos, zpad], axis=1)
        sin128 = jnp.concatenate([-sin, sin, zpad], axis=1)

        mem_kv = _mem_kv(mem[b], mem_norm_g[0][None, :], w_mem_kv[0], c_k_norm_g[0][None, :])
        proj = _in_proj(x[b], norm_g[0][None, :], w_cat, tm=tm_in, tn=tiles.in_proj_cols)
        aqn, akn, kab, wt, avt, qb, kb, vbt, cqn, knorm2 = _prep(
            proj, cos128, sin128, a_q_norm_g[0][None, :], a_k_norm_g[0][None, :],
            b_q_lat_norm_g[0][None, :], b_kv_lat_norm_g[0][None, :], gq_pad, gk_pad,
            c_q_norm_g[0][None, :], wuq, wuk, wuvt)
        o_b = _attn_b(qb, kb, vbt, knorm2, tq=tq_b)
        qmin = jnp.min(pos.reshape(seq // tq_a, tq_a), axis=1)
        kmax = jnp.max(pos.reshape(seq // KV_TILE, KV_TILE), axis=1)
        o_a = _attn_a(proj, wt, kab, aqn, akn, avt, pos[None, :], pos[:, None],
                      lut_t, knorm2, qmin, kmax, tq=tq_a, topk=topk)
        o_c = _attn_c(cqn, mem_kv, tm=tiles.attn_c_rows)
        outs.append(_merge_out(x[b], o_a, o_b, o_c, proj, wbr, wout, tm=tiles.merge_rows))
    return jnp.stack(outs, axis=0)
```

```python
import functools
import math
from typing import NamedTuple

import numpy as np
import jax
import jax.numpy as jnp
from jax import lax
from jax.experimental import pallas as pl
from jax.experimental.pallas import tpu as pltpu

F32 = jnp.float32
BF16 = jnp.bfloat16
I32 = jnp.int32

D_MODEL = 2048
BRANCH_WIDTH = 1024
N_BRANCH = 3
A_HEADS = 8
A_KV_HEADS = 2
A_HEAD_DIM = 128
A_WIDTH = A_HEADS * A_HEAD_DIM
A_KV_WIDTH = A_KV_HEADS * A_HEAD_DIM
A_REP = A_HEADS // A_KV_HEADS
IDX_HEADS = 16
IDX_DIM = 64
TOPK_MAX = 256
IDX_SCALE = (IDX_DIM ** -0.5) * (IDX_HEADS ** -0.5)
A_SCALE = A_HEAD_DIM ** -0.5
B_HEADS = 8
B_Q_RANK = 512
B_KV_RANK = 256
B_NOPE = 128
B_ROPE = 64
B_QK_DIM = B_NOPE + B_ROPE
B_V = 128
B_WIDTH = B_HEADS * B_V
B_SCALE = B_QK_DIM ** -0.5
ROPE_THETA = 10000.0
C_HEADS = 4
C_HEAD_DIM = 256
C_WIDTH = C_HEADS * C_HEAD_DIM
C_SCALE = C_HEAD_DIM ** -0.5
REL_BUCKETS = 32
REL_MAX_DIST = 128
EPS = 1e-6

IN_SIZES = (A_WIDTH, A_KV_WIDTH, A_KV_WIDTH, A_WIDTH, IDX_HEADS * IDX_DIM, IDX_DIM, IDX_HEADS,
            B_Q_RANK, B_KV_RANK, B_ROPE, B_WIDTH, C_WIDTH, C_WIDTH, N_BRANCH * D_MODEL)
IN_OFFSETS = [int(o) for o in np.cumsum(IN_SIZES)[:-1]]

LANES = 128
SUBLANES = 8
B_QK_PAD = 2 * LANES
VMEM_LIMIT_BYTES = 56 * 1024 * 1024
MASK_VALUE = -1e30
LOG2E = math.log2(math.e)
SHIFT_MARGIN = 1.02
MAX_SHIFT_SPAN = 100.0
KNORM_ROWS = 16
DV = 128
ONES_ROWS = 16
DV_AUG = DV + ONES_ROWS
assert DV == A_HEAD_DIM == B_V
KV_TILE = 512
B_HEADS_PER_STEP = 2

SEG_UNITS = dict(gates=48, aq=8, az=8, iq=8, bz=8, cq=8, cz=8, bcq=4, ak=2, av=2, bckv=2, ikw=1, bkpe=1)
SEG_START = {}
_u = 0
for _name, _w in SEG_UNITS.items():
    SEG_START[_name] = _u
    _u += _w
PROJ_UNITS = _u
PROJ_WIDTH = PROJ_UNITS * LANES


def _t5_bucket_table():
    d = np.arange(LANES)
    max_exact = REL_BUCKETS // 2
    nf = np.maximum(d, 1).astype(np.float64)
    large = max_exact + (np.log(nf / max_exact) / math.log(REL_MAX_DIST / max_exact)
                         * (REL_BUCKETS - max_exact)).astype(np.int64)
    large = np.minimum(large, REL_BUCKETS - 1)
    table = np.where(d < max_exact, d, large).astype(np.int32)
    far = int(np.min(np.nonzero(table == REL_BUCKETS - 1)[0]))
    assert np.all(table[far:] == REL_BUCKETS - 1)
    return table, far


T5_TABLE, T5_FAR = _t5_bucket_table()


def _cparams(*sem):
    return pltpu.CompilerParams(dimension_semantics=sem, vmem_limit_bytes=VMEM_LIMIT_BYTES)


def _dot_t(a, b):
    return lax.dot_general(a, b, (((1,), (1,)), ((), ())), preferred_element_type=F32)


def _rms_scale(ss, n):
    return lax.rsqrt(ss * (1.0 / n) + EPS)


def _mem_kv_kernel(mem_ref, g_ref, w_ref, ckn_ref, o_ref, h_ref):
    j = pl.program_id(0)

    @pl.when(j == 0)
    def _():
        m = mem_ref[...]
        r = _rms_scale(jnp.sum(m * m, axis=-1, keepdims=True), D_MODEL)
        h_ref[...] = (m * r * g_ref[...]).astype(BF16)

    y = jnp.dot(h_ref[...], w_ref[...].astype(BF16), preferred_element_type=F32)

    @pl.when(j < C_HEADS)
    def _():
        r = _rms_scale(jnp.sum(y * y, axis=-1, keepdims=True), C_HEAD_DIM)
        o_ref[...] = (y * r * ckn_ref[...]).astype(BF16)

    @pl.when(j >= C_HEADS)
    def _():
        o_ref[...] = y.astype(BF16)


def _mem_kv(mem, mem_g, w_mem_kv, ck_g):
    n_mem = mem.shape[0]
    return pl.pallas_call(
        _mem_kv_kernel,
        grid=(2 * C_HEADS,),
        in_specs=[
            pl.BlockSpec((n_mem, D_MODEL), lambda j: (0, 0)),
            pl.BlockSpec((1, D_MODEL), lambda j: (0, 0)),
            pl.BlockSpec((D_MODEL, C_HEAD_DIM), lambda j: (0, j)),
            pl.BlockSpec((1, C_HEAD_DIM), lambda j: (0, 0)),
        ],
        out_specs=pl.BlockSpec((n_mem, C_HEAD_DIM), lambda j: (0, j)),
        out_shape=jax.ShapeDtypeStruct((n_mem, 2 * C_WIDTH), BF16),
        scratch_shapes=[pltpu.VMEM((n_mem, D_MODEL), BF16)],
        compiler_params=_cparams("arbitrary"),
        name="mem_kv",
    )(mem, mem_g, w_mem_kv, ck_g)


def _in_proj_kernel(x_ref, g_ref, w_ref, o_ref, h_ref, *, row_chunk):
    @pl.when(pl.program_id(1) == 0)
    def _():
        g = g_ref[...]
        for c in range(x_ref.shape[0] // row_chunk):
            rows = pl.ds(c * row_chunk, row_chunk)
            x = x_ref[rows, :]
            r = _rms_scale(jnp.sum(x * x, axis=-1, keepdims=True), D_MODEL)
            h_ref[rows, :] = (x * r * g).astype(BF16)

    o_ref[...] = _dot_t(h_ref[...], w_ref[...]).astype(o_ref.dtype)


def _in_proj(x, norm_g, w_cat, *, tm, tn):
    s = x.shape[0]
    return pl.pallas_call(
        functools.partial(_in_proj_kernel, row_chunk=min(tm, 256)),
        grid=(s // tm, PROJ_WIDTH // tn),
        in_specs=[
            pl.BlockSpec((tm, D_MODEL), lambda i, j: (i, 0)),
            pl.BlockSpec((1, D_MODEL), lambda i, j: (0, 0)),
            pl.BlockSpec((tn, D_MODEL), lambda i, j: (j, 0)),
        ],
        out_specs=pl.BlockSpec((tm, tn), lambda i, j: (i, j)),
        out_shape=jax.ShapeDtypeStruct((s, PROJ_WIDTH), BF16),
        scratch_shapes=[pltpu.VMEM((tm, D_MODEL), BF16)],
        compiler_params=_cparams("parallel", "arbitrary"),
        name="in_proj",
    )(x, norm_g, w_cat)


def _rope128(x, cos, sin):
    half = B_ROPE // 2
    lane = lax.broadcasted_iota(I32, x.shape, 1)
    partner = jnp.where(lane < half, pltpu.roll(x, LANES - half, 1), pltpu.roll(x, half, 1))
    return x * cos + partner * sin


def _prep_kernel(aq_ref, ak_ref, av_ref, ikw_ref, bcq_ref, bckv_ref, bkpe_ref, cq_ref,
                 cos_ref, sin_ref, aqg_ref, akg_ref, bqlg_ref, bkvlg_ref, bqg_ref, bkg_ref, cqg_ref,
                 wuq_ref, wuk_ref, wuvt_ref,
                 aqn_ref, akn_ref, kab_ref, wt_ref, avt_ref, qb_ref, kb_ref, vbt_ref, cqn_ref, kn_ref):
    cos = cos_ref[...]
    sin = sin_ref[...]

    @pl.when(pl.program_id(0) == 0)
    def _():
        kn_ref[...] = jnp.zeros(kn_ref.shape, F32)

    def note_key_norm(row, sq_norms):
        top = jnp.max(sq_norms, axis=0, keepdims=True)
        kn_ref[row:row + 1, :] = jnp.maximum(kn_ref[row:row + 1, :], top)

    eye = jnp.where(lax.broadcasted_iota(I32, (LANES, LANES), 0) == lax.broadcasted_iota(I32, (LANES, LANES), 1),
                    1.0, 0.0).astype(BF16)

    aqg = aqg_ref[...] * (A_SCALE * LOG2E)
    for h in range(A_HEADS):
        cols = slice(h * A_HEAD_DIM, (h + 1) * A_HEAD_DIM)
        v = aq_ref[:, cols].astype(F32)
        r = _rms_scale(jnp.sum(v * v, axis=-1, keepdims=True), A_HEAD_DIM)
        aqn_ref[:, cols] = (v * r * aqg).astype(BF16)
    akg = akg_ref[...]
    for h in range(A_KV_HEADS):
        cols = slice(h * A_HEAD_DIM, (h + 1) * A_HEAD_DIM)
        v = ak_ref[:, cols].astype(F32)
        r = _rms_scale(jnp.sum(v * v, axis=-1, keepdims=True), A_HEAD_DIM)
        kn = v * r * akg
        akn_ref[:, cols] = kn.astype(BF16)
        note_key_norm(B_HEADS + h, jnp.sum(kn * kn, axis=-1, keepdims=True))
        avt_ref[h, 0, 0:DV] = _dot_t(eye, av_ref[:, cols]).astype(BF16)
        avt_ref[h, 0, DV:DV_AUG] = jnp.ones((ONES_ROWS, avt_ref.shape[-1]), BF16)

    ikw = ikw_ref[...]
    lane = lax.broadcasted_iota(I32, ikw.shape, 1)
    zero = jnp.zeros_like(ikw)
    kab_ref[:, 0:LANES] = jnp.where(lane < IDX_DIM, ikw, zero)
    kab_ref[:, LANES:2 * LANES] = jnp.where(lane >= IDX_DIM, pltpu.roll(ikw.astype(F32), IDX_DIM, 1).astype(BF16), zero)
    wt_ref[...] = _dot_t(eye, ikw) * IDX_SCALE

    cq_lat = bcq_ref[...].astype(F32)
    r = _rms_scale(jnp.sum(cq_lat * cq_lat, axis=-1, keepdims=True), B_Q_RANK)
    qlat = (cq_lat * r * bqlg_ref[...]).astype(BF16)
    bqg = bqg_ref[...] * (B_SCALE * LOG2E)
    for h in range(B_HEADS):
        cols = slice(h * B_QK_PAD, (h + 1) * B_QK_PAD)
        qh = jnp.dot(qlat, wuq_ref[:, cols], preferred_element_type=F32)
        r = _rms_scale(jnp.sum(qh * qh, axis=-1, keepdims=True), B_QK_DIM)
        qn = qh * r * bqg
        qb_ref[:, h * B_QK_PAD:h * B_QK_PAD + LANES] = qn[:, :LANES].astype(BF16)
        qb_ref[:, h * B_QK_PAD + LANES:(h + 1) * B_QK_PAD] = _rope128(qn[:, LANES:], cos, sin).astype(BF16)

    ckv = bckv_ref[...].astype(F32)
    r = _rms_scale(jnp.sum(ckv * ckv, axis=-1, keepdims=True), B_KV_RANK)
    kvlat = (ckv * r * bkvlg_ref[...]).astype(BF16)
    kpe = bkpe_ref[...].astype(F32)
    ss_pe = jnp.sum(kpe * kpe, axis=-1, keepdims=True)
    bkg = bkg_ref[...]
    kpe_rot = _rope128(kpe * bkg[:, LANES:], cos, sin)
    for h in range(B_HEADS):
        kn = jnp.dot(kvlat, wuk_ref[:, h * B_NOPE:(h + 1) * B_NOPE], preferred_element_type=F32)
        r = _rms_scale(jnp.sum(kn * kn, axis=-1, keepdims=True) + ss_pe, B_QK_DIM)
        k_nope = kn * r * bkg[:, :LANES]
        k_pe = kpe_rot * r
        kb_ref[:, h * B_QK_PAD:h * B_QK_PAD + LANES] = k_nope.astype(BF16)
        kb_ref[:, h * B_QK_PAD + LANES:(h + 1) * B_QK_PAD] = k_pe.astype(BF16)
        note_key_norm(h, jnp.sum(k_nope * k_nope, axis=-1, keepdims=True)
                      + jnp.sum(k_pe * k_pe, axis=-1, keepdims=True))
        vbt_ref[h, 0, 0:DV] = _dot_t(wuvt_ref[h], kvlat).astype(BF16)
        vbt_ref[h, 0, DV:DV_AUG] = jnp.ones((ONES_ROWS, vbt_ref.shape[-1]), BF16)

    cqg = cqg_ref[...] * C_SCALE
    for h in range(C_HEADS):
        cols = slice(h * C_HEAD_DIM, (h + 1) * C_HEAD_DIM)
        v = cq_ref[:, cols].astype(F32)
        r = _rms_scale(jnp.sum(v * v, axis=-1, keepdims=True), C_HEAD_DIM)
        cqn_ref[:, cols] = (v * r * cqg).astype(BF16)


def _seg_spec(tm, name):
    units = SEG_UNITS[name]
    blk = SEG_START[name] // units
    return pl.BlockSpec((tm, units * LANES), lambda i: (i, blk))


def _full_spec(shape):
    return pl.BlockSpec(shape, lambda i: (0,) * len(shape))


def _prep(proj, cos128, sin128, aqg, akg, bqlg, bkvlg, bqg, bkg, cqg, wuq, wuk, wuvt):
    s = proj.shape[0]
    tm = KV_TILE
    nt = s // tm
    row = lambda w: pl.BlockSpec((tm, w), lambda i: (i, 0))
    tiles_t = lambda n: pl.BlockSpec((n, 1, DV_AUG, tm), lambda i: (0, i, 0, 0))
    small = [aqg, akg, bqlg, bkvlg, bqg, bkg, cqg, wuq, wuk, wuvt]
    out_specs = [row(A_WIDTH), row(A_KV_WIDTH), row(2 * LANES),
                 pl.BlockSpec((LANES, tm), lambda i: (0, i)), tiles_t(A_KV_HEADS),
                 row(B_HEADS * B_QK_PAD), row(B_HEADS * B_QK_PAD), tiles_t(B_HEADS), row(C_WIDTH),
                 _full_spec((KNORM_ROWS, LANES))]
    out_shape = [
        jax.ShapeDtypeStruct((s, A_WIDTH), BF16),
        jax.ShapeDtypeStruct((s, A_KV_WIDTH), BF16),
        jax.ShapeDtypeStruct((s, 2 * LANES), BF16),
        jax.ShapeDtypeStruct((LANES, s), F32),
        jax.ShapeDtypeStruct((A_KV_HEADS, nt, DV_AUG, tm), BF16),
        jax.ShapeDtypeStruct((s, B_HEADS * B_QK_PAD), BF16),
        jax.ShapeDtypeStruct((s, B_HEADS * B_QK_PAD), BF16),
        jax.ShapeDtypeStruct((B_HEADS, nt, DV_AUG, tm), BF16),
        jax.ShapeDtypeStruct((s, C_WIDTH), BF16),
        jax.ShapeDtypeStruct((KNORM_ROWS, LANES), F32),
    ]
    return pl.pallas_call(
        _prep_kernel,
        grid=(nt,),
        in_specs=[_seg_spec(tm, n) for n in ("aq", "ak", "av", "ikw", "bcq", "bckv", "bkpe", "cq")]
        + [row(LANES), row(LANES)] + [_full_spec(a.shape) for a in small],
        out_specs=out_specs,
        out_shape=out_shape,
        compiler_params=_cparams("arbitrary"),
        name="prep",
    )(*([proj] * 8), cos128, sin128, *small)


def _reduce_keys(x, op):
    tk, tq = x.shape
    n = tk // SUBLANES
    assert n & (n - 1) == 0
    t = x.reshape(n, SUBLANES, tq)
    while n > 1:
        n //= 2
        t = op(t[:n], t[n:2 * n])
    red = jnp.max if op is jnp.maximum else jnp.sum
    return red(t[0], axis=0, keepdims=True)


def _softmax_step_t(st, vt, m_ref, acc_ref):
    m_prev = m_ref[...]
    m_new = jnp.maximum(m_prev, _reduce_keys(st, jnp.maximum))
    alpha = jnp.exp2(m_prev - m_new)
    p = jnp.exp2(st - m_new).astype(BF16)
    acc_ref[...] = alpha * acc_ref[...] + jnp.dot(vt, p, preferred_element_type=F32)
    m_ref[...] = m_new


def _fixed_shift_step_t(st, vt, shift, acc_ref):
    p = jnp.exp2(st - shift).astype(BF16)
    acc_ref[...] += jnp.dot(vt, p, preferred_element_type=F32)


def _normalised_output(acc):
    return (acc[0:DV] / acc[DV:DV + 1]).T


def _logit_bound(q, kmax2):
    qf = q.astype(F32)
    ones = jnp.ones((SUBLANES, q.shape[1]), BF16)
    qn2 = _dot_t(ones, (qf * qf).astype(BF16))[0:1]
    return jnp.sqrt(qn2 * kmax2) * SHIFT_MARGIN


def _causal_ok_t(i, j, tq, tk):
    key = j * tk + lax.broadcasted_iota(I32, (tk, tq), 0)
    qry = i * tq + lax.broadcasted_iota(I32, (tk, tq), 1)
    return key <= qry


def _init_softmax_state(m_ref, acc_ref):
    m_ref[...] = jnp.full(m_ref.shape, MASK_VALUE, F32)
    acc_ref[...] = jnp.zeros(acc_ref.shape, F32)


def _pingpong_tiles(n_full, logits, consume, buf_a, buf_b):
    def pair(p, c):
        j = 2 * p
        logits(j + 1, buf_b)
        consume(j, buf_a, False)
        logits(j + 2, buf_a)
        consume(j + 1, buf_b, False)
        return c

    lax.fori_loop(0, n_full // 2, pair, 0)
    odd = n_full % 2 == 1

    @pl.when(odd)
    def _():
        logits(n_full, buf_b)
        consume(n_full - 1, buf_a, False)
        consume(n_full, buf_b, True)

    @pl.when(jnp.logical_not(odd))
    def _():
        consume(n_full, buf_a, True)


def _attn_b_kernel(q_ref, k_ref, vt_ref, kn_ref, o_ref, sa_ref, sb_ref, m_ref, acc_ref, *, tq, tk):
    hp = pl.program_id(0)
    i = pl.program_id(1)
    _init_softmax_state(m_ref, acc_ref)
    j_diag = (i * tq) // tk
    heads = range(B_HEADS_PER_STEP)

    def logits(j, buf):
        rows = pl.ds(pl.multiple_of(j * tk, tk), tk)
        for hh in heads:
            cols = slice(hh * B_QK_PAD, (hh + 1) * B_QK_PAD)
            buf[hh] = _dot_t(k_ref[rows, cols], q_ref[:, cols])

    logits(0, sa_ref)
    shifts = [_logit_bound(q_ref[:, hh * B_QK_PAD:(hh + 1) * B_QK_PAD],
                           kn_ref[pl.ds(hp * B_HEADS_PER_STEP + hh, 1), 0:1]) for hh in heads]
    fixed_ok = jnp.max(2.0 * functools.reduce(jnp.maximum, shifts)) <= MAX_SHIFT_SPAN

    def consume(j, buf, masked, fixed):
        for hh in heads:
            st = buf[hh]
            if masked:
                st = jnp.where(_causal_ok_t(i, j, tq, tk), st, MASK_VALUE)
            if fixed:
                _fixed_shift_step_t(st, vt_ref[hh, j], shifts[hh], acc_ref.at[hh])
            else:
                _softmax_step_t(st, vt_ref[hh, j], m_ref.at[hh], acc_ref.at[hh])

    @pl.when(fixed_ok)
    def _():
        _pingpong_tiles(j_diag, logits, functools.partial(consume, fixed=True), sa_ref, sb_ref)

    @pl.when(jnp.logical_not(fixed_ok))
    def _():
        _pingpong_tiles(j_diag, logits, functools.partial(consume, fixed=False), sa_ref, sb_ref)

    for hh in heads:
        o_ref[:, hh * B_V:(hh + 1) * B_V] = _normalised_output(acc_ref[hh]).astype(o_ref.dtype)


def _attn_b(qb, kb, vbt, knorm2, *, tq):
    s = qb.shape[0]
    tk = KV_TILE
    hps = B_HEADS_PER_STEP
    assert tk % tq == 0 or tq % tk == 0
    assert tq <= tk
    return pl.pallas_call(
        functools.partial(_attn_b_kernel, tq=tq, tk=tk),
        grid=(B_HEADS // hps, s // tq),
        in_specs=[
            pl.BlockSpec((tq, hps * B_QK_PAD), lambda h, i: (i, h)),
            pl.BlockSpec((s, hps * B_QK_PAD), lambda h, i: (0, h)),
            pl.BlockSpec((hps, s // tk, DV_AUG, tk), lambda h, i: (h, 0, 0, 0)),
            pl.BlockSpec((KNORM_ROWS, LANES), lambda h, i: (0, 0)),
        ],
        out_specs=pl.BlockSpec((tq, hps * B_V), lambda h, i: (i, h)),
        out_shape=jax.ShapeDtypeStruct((s, B_WIDTH), BF16),
        scratch_shapes=[pltpu.VMEM((hps, tk, tq), F32), pltpu.VMEM((hps, tk, tq), F32),
                        pltpu.VMEM((hps, 1, tq), F32), pltpu.VMEM((hps, DV_AUG, tq), F32)],
        compiler_params=_cparams("parallel", "arbitrary"),
        name="attn_b",
    )(qb, kb, vbt, knorm2)


KEY_BITS = 32
COARSE_BITS = 16
BISECT_GROUP = 4
BRACKET_GROUP = 2
PACKED_SUBLANES = 16
KEY_NEG_INF = (0xFF800000 ^ 0x7FFFFFFF) - (1 << 32)
BRACKET_HALF = (1 << (KEY_BITS - COARSE_BITS - 1)) + 1
BRACKET_STEPS = (KEY_BITS - COARSE_BITS + 1 + BRACKET_GROUP) // BRACKET_GROUP * BRACKET_GROUP
assert KEY_BITS % BISECT_GROUP == 0
assert (1 << BRACKET_STEPS) > 2 * BRACKET_HALF + 1


def _key_as_f32(key):
    bits = key ^ (lax.shift_right_arithmetic(key, 31) & 0x7FFFFFFF)
    return lax.bitcast_convert_type(bits, F32)


def _coarse_key_as_bf16(k):
    b = k ^ (lax.shift_right_arithmetic(k, COARSE_BITS - 1) & ((1 << (COARSE_BITS - 1)) - 1))
    return lax.bitcast_convert_type(lax.shift_left(b, COARSE_BITS), F32).astype(BF16)


def _count_hits_packed(hit):
    tk, tq = hit.shape
    n = tk // PACKED_SUBLANES
    assert n <= 256
    t = jnp.where(hit, jnp.ones((), BF16), jnp.zeros((), BF16)).reshape(n, PACKED_SUBLANES, tq)
    while n > 1:
        n //= 2
        t = t[:n] + t[n:2 * n]
    return t[0].astype(F32)


def _count_hits(hit):
    tk, tq = hit.shape
    n = tk // SUBLANES
    t = jnp.where(hit, 1, 0).reshape(n, SUBLANES, tq)
    while n > 1:
        n //= 2
        t = t[:n] + t[n:2 * n]
    return t[0]


def _attn_a_kernel(qmin_ref, kmax_ref,
                   iq_ref, wt_ref, kab_ref, aqn_ref, akn_ref, avt_ref, pq_ref, pk_ref, lut_ref, kn_ref,
                   o_ref,
                   score_ref, coarse_ref, thr_ref, settled_ref, sa_ref, sb_ref, m_ref, acc_ref, *, tq, tk, topk):
    i = pl.program_id(0)
    j_diag = (i * tq) // tk
    n_tiles = j_diag + 1
    int_min = jnp.int32(-2 ** 31)

    def head_pair_dots(j, p):
        rows = pl.ds(pl.multiple_of(j * tk, tk), tk)
        rhs = iq_ref[:, p * LANES:(p + 1) * LANES]
        return _dot_t(kab_ref[rows, 0:LANES], rhs), _dot_t(kab_ref[rows, LANES:2 * LANES], rhs)

    def score_tile(j, masked):
        acc = jnp.zeros((tk, tq), F32)
        n_pairs = IDX_HEADS // 2
        for p in range(n_pairs):
            d_lo, d_hi = (sa_ref[...], sb_ref[...]) if p == 0 else head_pair_dots(j, p)
            if p == n_pairs - 1 and not masked:
                sa_ref[...], sb_ref[...] = head_pair_dots(j + 1, 0)
            w_row = IDX_DIM + 2 * p
            acc = acc + jnp.maximum(d_lo, 0.0) * wt_ref[w_row:w_row + 1, :]
            acc = acc + jnp.maximum(d_hi, 0.0) * wt_ref[w_row + 1:w_row + 2, :]
        if masked:
            acc = jnp.where(_causal_ok_t(i, j, tq, tk), acc, -jnp.inf)
        score_ref[j] = acc
        coarse_ref[j] = acc.astype(BF16)

    def score_body(j, c):
        score_tile(j, False)
        return c

    sa_ref[...], sb_ref[...] = head_pair_dots(0, 0)
    lax.fori_loop(0, j_diag, score_body, 0)
    score_tile(j_diag, True)

    def count_queries(pred_fn):
        def body(j, cnt):
            return cnt + _count_hits(pred_fn(score_ref[j], j))
        cnt = lax.fori_loop(0, n_tiles, body, jnp.zeros((SUBLANES, tq), I32))
        return jnp.sum(cnt, axis=0, keepdims=True)

    def count_at_or_above(cand_key):
        cand = _key_as_f32(cand_key)
        cnt = count_queries(lambda x, j: x >= cand)
        return jnp.where(cand_key <= KEY_NEG_INF, n_tiles * tk, cnt)

    def store_threshold(thr_key, settled):
        thr_ref[...] = jnp.where(thr_key <= KEY_NEG_INF, -jnp.inf, _key_as_f32(thr_key))
        settled_ref[...] = settled

    def bisect_all_bits():
        def cond(c):
            g, _, s = c
            return (g < KEY_BITS // BISECT_GROUP) & (jnp.min(s) == 0)

        def body(c):
            g, t, s = c
            for u in range(BISECT_GROUP):
                bit = KEY_BITS - 1 - (g * BISECT_GROUP + u)
                cand = t + lax.shift_left(jnp.int32(1), bit)
                cnt = count_at_or_above(cand)
                t = jnp.where((s == 0) & (cnt >= topk), cand, t)
                s = jnp.where(cnt == topk, 1, s)
            return g + 1, t, s

        _, t, s = lax.while_loop(cond, body, (jnp.int32(0), jnp.full((1, tq), int_min, I32),
                                              jnp.zeros((1, tq), I32)))
        store_threshold(t, s)

    def bisect_bracket(lo, hi, cnt_lo):
        def cond(c):
            g, lo, hi, _, s = c
            done = (s == 1) | (hi - lo <= 1)
            return (g < BRACKET_STEPS // BRACKET_GROUP) & (jnp.min(jnp.where(done, 1, 0)) == 0)

        def body(c):
            g, lo, hi, t, s = c
            for _ in range(BRACKET_GROUP):
                mid = lo + lax.shift_right_arithmetic(hi - lo, 1)
                cnt = count_at_or_above(mid)
                t = jnp.where((s == 0) & (cnt == topk), mid, t)
                s = jnp.where(cnt == topk, 1, s)
                lo, hi = jnp.where(cnt >= topk, mid, lo), jnp.where(cnt >= topk, hi, mid)
            return g + 1, lo, hi, t, s

        settled0 = jnp.where(cnt_lo == topk, 1, 0)
        _, lo, _, t, s = lax.while_loop(cond, body, (jnp.int32(0), lo, hi, lo, settled0))
        store_threshold(jnp.where(s == 1, t, lo), s)

    def count_coarse(cand):
        def body(j, cnt):
            return cnt + _count_hits_packed(coarse_ref[j] >= cand)
        cnt = lax.fori_loop(0, n_tiles, body, jnp.zeros((PACKED_SUBLANES, tq), F32))
        return jnp.sum(cnt, axis=0, keepdims=True)

    def coarse_step(it, t):
        cand = t + lax.shift_left(jnp.int32(1), COARSE_BITS - 1 - it)
        return jnp.where(count_coarse(_coarse_key_as_bf16(cand)) >= topk, cand, t)

    t_coarse = lax.fori_loop(0, COARSE_BITS, coarse_step,
                             jnp.full((1, tq), -(1 << (COARSE_BITS - 1)), I32))

    low_bits = KEY_BITS - COARSE_BITS
    centre = lax.shift_left(t_coarse, low_bits) + jnp.where(t_coarse < 0, (1 << low_bits) - 1, 0)
    lo_key, hi_key = centre - BRACKET_HALF, centre + BRACKET_HALF + 1
    cnt_lo = count_at_or_above(lo_key)
    cnt_hi = count_at_or_above(hi_key)
    bracket_ok = (cnt_lo >= topk) & (cnt_hi < topk) & (lo_key < hi_key)
    coarse_ok = jnp.min(jnp.where(bracket_ok, 1, 0)) == 1

    pl.when(coarse_ok)(lambda: bisect_bracket(lo_key, hi_key, cnt_lo))
    pl.when(jnp.logical_not(coarse_ok))(bisect_all_bits)

    thr = thr_ref[...]
    settled = settled_ref[...]

    def drop_excess_ties():
        cnt_gt = count_queries(lambda x, j: x > thr)
        need = topk - cnt_gt
        kpos = lax.broadcasted_iota(I32, (tk, tq), 0)

        def pos_body(it, cut):
            cand = cut + lax.shift_left(jnp.int32(1), 30 - it)
            cnt = count_queries(lambda x, j: (x == thr) & (j * tk + kpos < cand))
            return jnp.where(cnt < need, cand, cut)

        cut = lax.fori_loop(0, 31, pos_body, jnp.zeros((1, tq), I32))

        def drop_body(j, c):
            x = score_ref[j]
            drop = (x == thr) & (j * tk + kpos > cut)
            score_ref[j] = jnp.where(drop, -jnp.inf, x)
            return c

        lax.fori_loop(0, n_tiles, drop_body, 0)

    @pl.when(jnp.min(settled) == 0)
    def _():
        cnt_ge = count_queries(lambda x, j: x >= thr)
        pl.when(jnp.max(cnt_ge) > topk)(drop_excess_ties)

    _init_softmax_state(m_ref, acc_ref)
    pq = pq_ref[...]
    lut = lut_ref[...]
    bias_hi = jnp.max(lut, axis=1, keepdims=True)
    bias_span = bias_hi - jnp.min(lut, axis=1, keepdims=True)
    bounds = [_logit_bound(aqn_ref[:, h * A_HEAD_DIM:(h + 1) * A_HEAD_DIM],
                           kn_ref[B_HEADS + h // A_REP:B_HEADS + h // A_REP + 1, 0:1]) for h in range(A_HEADS)]
    shifts = [bounds[h] + bias_hi[h:h + 1] for h in range(A_HEADS)]
    spans = [2.0 * bounds[h] + bias_span[h:h + 1] for h in range(A_HEADS)]
    fixed_ok = jnp.max(functools.reduce(jnp.maximum, spans)) <= MAX_SHIFT_SPAN

    def head_buf(h):
        return (sa_ref, sb_ref)[h % 2]

    def logits(j, h, buf):
        rows = pl.ds(pl.multiple_of(j * tk, tk), tk)
        g = h // A_REP
        buf[...] = _dot_t(akn_ref[rows, g * A_HEAD_DIM:(g + 1) * A_HEAD_DIM],
                          aqn_ref[:, h * A_HEAD_DIM:(h + 1) * A_HEAD_DIM])

    def attend(j, masked, near, fixed):
        rows = pl.ds(pl.multiple_of(j * tk, tk), tk)
        sel = score_ref[j] >= thr
        if masked:
            sel = sel & _causal_ok_t(i, j, tq, tk)
        mask_bias = jnp.where(sel, 0.0, MASK_VALUE)
        if near:
            dist = jnp.clip(pq - pk_ref[rows, :], 0, LANES - 1)

        logits(j, 0, head_buf(0))
        for h in range(A_HEADS):
            if h + 1 < A_HEADS:
                logits(j, h + 1, head_buf(h + 1))
            st = head_buf(h)[...] + mask_bias
            if near:
                table = jnp.broadcast_to(lut_ref[h:h + 1, :], (tk, LANES))
                st = st + jnp.concatenate(
                    [jnp.take_along_axis(table, dist[:, c * LANES:(c + 1) * LANES], axis=1,
                                         mode="promise_in_bounds")
                     for c in range(tq // LANES)], axis=1)
            if fixed:
                _fixed_shift_step_t(st, avt_ref[h // A_REP, j], shifts[h], acc_ref.at[h])
            else:
                _softmax_step_t(st, avt_ref[h // A_REP, j], m_ref.at[h], acc_ref.at[h])

    def attend_dyn(j, masked, fixed):
        far = qmin_ref[i] - kmax_ref[j] >= T5_FAR

        @pl.when(far)
        def _():
            attend(j, masked, False, fixed)

        @pl.when(jnp.logical_not(far))
        def _():
            attend(j, masked, True, fixed)

    def attend_all(fixed):
        def attend_body(j, c):
            attend_dyn(j, False, fixed)
            return c

        lax.fori_loop(0, j_diag, attend_body, 0)
        attend_dyn(j_diag, True, fixed)

    pl.when(fixed_ok)(functools.partial(attend_all, True))
    pl.when(jnp.logical_not(fixed_ok))(functools.partial(attend_all, False))

    for h in range(A_HEADS):
        q_cols = slice(h * A_HEAD_DIM, (h + 1) * A_HEAD_DIM)
        o_ref[:, q_cols] = _normalised_output(acc_ref[h]).astype(o_ref.dtype)


def _attn_a(proj, wt, kab, aqn, akn, avt, pos_row, pos_col, lut_t, knorm2, qmin, kmax, *, tq, topk):
    s = proj.shape[0]
    tk = KV_TILE
    assert tk % tq == 0
    iq_blk = SEG_START["iq"] // SEG_UNITS["iq"]
    grid_spec = pltpu.PrefetchScalarGridSpec(
        num_scalar_prefetch=2,
        grid=(s // tq,),
        in_specs=[
            pl.BlockSpec((tq, IDX_HEADS * IDX_DIM), lambda i, *_: (i, iq_blk)),
            pl.BlockSpec((LANES, tq), lambda i, *_: (0, i)),
            pl.BlockSpec((s, 2 * LANES), lambda i, *_: (0, 0)),
            pl.BlockSpec((tq, A_WIDTH), lambda i, *_: (i, 0)),
            pl.BlockSpec((s, A_KV_WIDTH), lambda i, *_: (0, 0)),
            pl.BlockSpec((A_KV_HEADS, s // tk, DV_AUG, tk), lambda i, *_: (0, 0, 0, 0)),
            pl.BlockSpec((1, tq), lambda i, *_: (0, i)),
            pl.BlockSpec((s, 1), lambda i, *_: (0, 0)),
            pl.BlockSpec((A_HEADS, LANES), lambda i, *_: (0, 0)),
            pl.BlockSpec((KNORM_ROWS, LANES), lambda i, *_: (0, 0)),
        ],
        out_specs=pl.BlockSpec((tq, A_WIDTH), lambda i, *_: (i, 0)),
        scratch_shapes=[
            pltpu.VMEM((s // tk, tk, tq), F32),
            pltpu.VMEM((s // tk, tk, tq), BF16),
            pltpu.VMEM((1, tq), F32),
            pltpu.VMEM((1, tq), I32),
            pltpu.VMEM((tk, tq), F32),
            pltpu.VMEM((tk, tq), F32),
            pltpu.VMEM((A_HEADS, 1, tq), F32),
            pltpu.VMEM((A_HEADS, DV_AUG, tq), F32),
        ],
    )
    return pl.pallas_call(
        functools.partial(_attn_a_kernel, tq=tq, tk=tk, topk=topk),
        grid_spec=grid_spec,
        out_shape=jax.ShapeDtypeStruct((s, A_WIDTH), BF16),
        compiler_params=_cparams("arbitrary"),
        name="attn_a",
    )(qmin, kmax, proj, wt, kab, aqn, akn, avt, pos_row, pos_col, lut_t, knorm2)


def _attn_c_kernel(q_ref, kv_ref, o_ref):
    for h in range(C_HEADS):
        cols = slice(h * C_HEAD_DIM, (h + 1) * C_HEAD_DIM)
        s = _dot_t(q_ref[:, cols], kv_ref[:, cols])
        p = jnp.exp(s - jnp.max(s, axis=-1, keepdims=True))
        o = jnp.dot(p.astype(BF16), kv_ref[:, C_WIDTH + h * C_HEAD_DIM:C_WIDTH + (h + 1) * C_HEAD_DIM],
                    preferred_element_type=F32)
        o_ref[:, cols] = (o / jnp.sum(p, axis=-1, keepdims=True)).astype(o_ref.dtype)


def _attn_c(cqn, mem_kv, *, tm):
    s = cqn.shape[0]
    return pl.pallas_call(
        _attn_c_kernel,
        grid=(s // tm,),
        in_specs=[pl.BlockSpec((tm, C_WIDTH), lambda i: (i, 0)), _full_spec(mem_kv.shape)],
        out_specs=pl.BlockSpec((tm, C_WIDTH), lambda i: (i, 0)),
        out_shape=jax.ShapeDtypeStruct((s, C_WIDTH), BF16),
        compiler_params=_cparams("parallel"),
        name="attn_c",
    )(cqn, mem_kv)


def _merge_out_kernel(x_ref, oa_ref, ob_ref, oc_ref, az_ref, bz_ref, cz_ref, ga_ref, gb_ref, gc_ref,
                      wbr_ref, wout_ref, o_ref):
    merged = None
    for n, (o_r, z_r, g_r) in enumerate(((oa_ref, az_ref, ga_ref), (ob_ref, bz_ref, gb_ref),
                                         (oc_ref, cz_ref, gc_ref))):
        z = z_r[...].astype(F32)
        u = (o_r[...].astype(F32) * (z * jax.nn.sigmoid(z))).astype(BF16)
        y = jnp.dot(u, wbr_ref[n], preferred_element_type=F32)
        t = jax.nn.sigmoid(g_r[...].astype(F32)) * y
        merged = t if merged is None else merged + t
    o_ref[...] = x_ref[...] + jnp.dot(merged.astype(BF16), wout_ref[...], preferred_element_type=F32)


def _merge_out(x, o_a, o_b, o_c, proj, w_branch, w_out, *, tm):
    s = x.shape[0]
    row = lambda w: pl.BlockSpec((tm, w), lambda i: (i, 0))
    gate_blk = SEG_START["gates"] * LANES // D_MODEL
    gate = lambda n: pl.BlockSpec((tm, D_MODEL), lambda i: (i, gate_blk + n))
    single = pl.Buffered(1)
    return pl.pallas_call(
        _merge_out_kernel,
        grid=(s // tm,),
        in_specs=[row(D_MODEL), row(BRANCH_WIDTH), row(BRANCH_WIDTH), row(BRANCH_WIDTH),
                  _seg_spec(tm, "az"), _seg_spec(tm, "bz"), _seg_spec(tm, "cz"),
                  gate(0), gate(1), gate(2),
                  pl.BlockSpec(w_branch.shape, lambda i: (0, 0, 0), pipeline_mode=single),
                  pl.BlockSpec(w_out.shape, lambda i: (0, 0), pipeline_mode=single)],
        out_specs=row(D_MODEL),
        out_shape=jax.ShapeDtypeStruct((s, D_MODEL), F32),
        compiler_params=_cparams("parallel"),
        name="merge_out",
    )(x, o_a, o_b, o_c, proj, proj, proj, proj, proj, proj, w_branch, w_out)


def _regroup_tables():
    names = ("aq", "ak", "av", "az", "iq", "ik", "iw", "bcq", "bckv", "bkpe", "bz", "cq", "cz", "gates")
    src_off = dict(zip(names, [0] + IN_OFFSETS))
    src_off["ikw"] = src_off["ik"]
    valid_rows = dict(ikw=IDX_DIM + IDX_HEADS, bkpe=B_ROPE)
    start, valid = [], []
    for name, units in SEG_UNITS.items():
        for u in range(units):
            start.append(src_off[name] + u * LANES)
            valid.append(valid_rows.get(name, LANES))
    return np.asarray(start, np.int32), np.asarray(valid, np.int32)


REGROUP_START, REGROUP_VALID = _regroup_tables()
REGROUP_ALIGN = math.gcd(*(int(v) for v in REGROUP_START if v))
assert REGROUP_ALIGN % SUBLANES == 0


REGROUP_UNITS_PER_STEP = 4
assert PROJ_UNITS % REGROUP_UNITS_PER_STEP == 0


def _regroup_kernel(start_ref, valid_ref, *refs):
    w_refs, o_ref = refs[:-1], refs[-1]
    row = lax.broadcasted_iota(I32, (LANES, o_ref.shape[1]), 0)
    for u, w_ref in enumerate(w_refs):
        valid = valid_ref[pl.program_id(0) * REGROUP_UNITS_PER_STEP + u]
        o_ref[u * LANES:(u + 1) * LANES, :] = jnp.where(row < valid, w_ref[...], 0.0).astype(BF16)


def _regroup_w_in_t(w_in_t):
    d = w_in_t.shape[1]
    ups = REGROUP_UNITS_PER_STEP

    def window(u):
        return pl.BlockSpec((pl.Element(LANES), pl.Element(d)),
                            lambda b, start, valid: (start[b * ups + u] * REGROUP_ALIGN, 0))

    grid_spec = pltpu.PrefetchScalarGridSpec(
        num_scalar_prefetch=2,
        grid=(PROJ_UNITS // ups,),
        in_specs=[window(u) for u in range(ups)],
        out_specs=pl.BlockSpec((ups * LANES, d), lambda b, *_: (b, 0)),
    )
    return pl.pallas_call(
        _regroup_kernel,
        grid_spec=grid_spec,
        out_shape=jax.ShapeDtypeStruct((PROJ_WIDTH, d), BF16),
        compiler_params=_cparams("arbitrary"),
        name="regroup_w_in",
    )(jnp.asarray(REGROUP_START // REGROUP_ALIGN), jnp.asarray(REGROUP_VALID), *([w_in_t] * ups))


def _pad_heads(w, n_heads, width, pad_to):
    r = w.shape[0]
    w = w.reshape(r, n_heads, width)
    w = jnp.pad(w, ((0, 0), (0, 0), (0, pad_to - width)))
    return w.reshape(r, n_heads * pad_to)


class _TilePlan(NamedTuple):
    in_proj_rows: int
    in_proj_cols: int
    attn_a_q: int
    attn_b_q: int
    attn_c_rows: int
    merge_rows: int


def _tile_plan(seq):
    plan = _TilePlan(in_proj_rows=min(1024, seq), in_proj_cols=1536, attn_a_q=256, attn_b_q=min(512, seq),
                     attn_c_rows=min(512, seq), merge_rows=min(256, seq))
    assert seq % KV_TILE == 0 and PROJ_WIDTH % plan.in_proj_cols == 0
    assert all(seq % t == 0 for t in (plan.in_proj_rows, plan.attn_a_q, plan.attn_b_q, plan.attn_c_rows,
                                      plan.merge_rows))
    return plan


def kernel(x, mem, positions, rel_bias, norm_g, mem_norm_g, w_in, a_q_norm_g, a_k_norm_g, b_q_lat_norm_g,
           b_kv_lat_norm_g, w_b_uq, w_b_ukv, b_q_norm_g, b_k_norm_g, w_mem_kv, c_q_norm_g, c_k_norm_g,
           w_branch, w_out):
    bsz, seq, d = x.shape
    assert d == D_MODEL and norm_g.shape[0] == 1
    topk = min(TOPK_MAX, seq // 4)
    tiles = _tile_plan(seq)
    tq_a, tq_b, tm_in = tiles.attn_a_q, tiles.attn_b_q, tiles.in_proj_rows

    inv_freq = 1.0 / (ROPE_THETA ** (jnp.arange(0, B_ROPE, 2, dtype=F32) / B_ROPE))
    zeros_half = jnp.zeros((B_ROPE,), F32)
    gq_pad = jnp.concatenate([b_q_norm_g[0], zeros_half])[None, :]
    gk_pad = jnp.concatenate([b_k_norm_g[0], zeros_half])[None, :]
    lut_t = ((rel_bias[jnp.asarray(T5_TABLE)] - rel_bias[REL_BUCKETS - 1]).T * LOG2E).astype(F32)

    w_cat = _regroup_w_in_t(jnp.transpose(w_in[0]))
    wuq = _pad_heads(w_b_uq[0], B_HEADS, B_QK_DIM, B_QK_PAD).astype(BF16)
    wukv = w_b_ukv[0].reshape(B_KV_RANK, B_HEADS, B_NOPE + B_V)
    wuk = wukv[:, :, :B_NOPE].reshape(B_KV_RANK, B_HEADS * B_NOPE).astype(BF16)
    wuvt = jnp.transpose(wukv[:, :, B_NOPE:], (1, 2, 0)).astype(BF16)
    wbr = w_branch[0].astype(BF16)
    wout = w_out[0].astype(BF16)

    outs = []
    for b in range(bsz):
        pos = positions[b]
        ang = pos.astype(F32)[:, None] * inv_freq
        cos, sin = jnp.cos(ang), jnp.sin(ang)
        zpad = jnp.zeros((seq, LANES - B_ROPE), F32)
        cos128 = jnp.concatenate([cos, cos, zpad], axis=1)
        sin128 = jnp.concatenate([-sin, sin, zpad], axis=1)

        mem_kv = _mem_kv(mem[b], mem_norm_g[0][None, :], w_mem_kv[0], c_k_norm_g[0][None, :])
        proj = _in_proj(x[b], norm_g[0][None, :], w_cat, tm=tm_in, tn=tiles.in_proj_cols)
        aqn, akn, kab, wt, avt, qb, kb, vbt, cqn, knorm2 = _prep(
            proj, cos128, sin128, a_q_norm_g[0][None, :], a_k_norm_g[0][None, :],
            b_q_lat_norm_g[0][None, :], b_kv_lat_norm_g[0][None, :], gq_pad, gk_pad,
            c_q_norm_g[0][None, :], wuq, wuk, wuvt)
        o_b = _attn_b(qb, kb, vbt, knorm2, tq=tq_b)
        qmin = jnp.min(pos.reshape(seq // tq_a, tq_a), axis=1)
        kmax = jnp.max(pos.reshape(seq // KV_TILE, KV_TILE), axis=1)
        o_a = _attn_a(proj, wt, kab, aqn, akn, avt, pos[None, :], pos[:, None],
                      lut_t, knorm2, qmin, kmax, tq=tq_a, topk=topk)
        o_c = _attn_c(cqn, mem_kv, tm=tiles.attn_c_rows)
        outs.append(_merge_out(x[b], o_a, o_b, o_c, proj, wbr, wout, tm=tiles.merge_rows))
    return jnp.stack(outs, axis=0)
```

```python
import functools
import math
from typing import NamedTuple

import numpy as np
import jax
import jax.numpy as jnp
from jax import lax
from jax.experimental import pallas as pl
from jax.experimental.pallas import tpu as pltpu

F32 = jnp.float32
BF16 = jnp.bfloat16
I32 = jnp.int32

D_MODEL = 2048
BRANCH_WIDTH = 1024
N_BRANCH = 3
A_HEADS = 8
A_KV_HEADS = 2
A_HEAD_DIM = 128
A_WIDTH = A_HEADS * A_HEAD_DIM
A_KV_WIDTH = A_KV_HEADS * A_HEAD_DIM
A_REP = A_HEADS // A_KV_HEADS
IDX_HEADS = 16
IDX_DIM = 64
TOPK_MAX = 256
IDX_SCALE = (IDX_DIM ** -0.5) * (IDX_HEADS ** -0.5)
A_SCALE = A_HEAD_DIM ** -0.5
B_HEADS = 8
B_Q_RANK = 512
B_KV_RANK = 256
B_NOPE = 128
B_ROPE = 64
B_QK_DIM = B_NOPE + B_ROPE
B_V = 128
B_WIDTH = B_HEADS * B_V
B_SCALE = B_QK_DIM ** -0.5
ROPE_THETA = 10000.0
C_HEADS = 4
C_HEAD_DIM = 256
C_WIDTH = C_HEADS * C_HEAD_DIM
C_SCALE = C_HEAD_DIM ** -0.5
REL_BUCKETS = 32
REL_MAX_DIST = 128
EPS = 1e-6

IN_SIZES = (A_WIDTH, A_KV_WIDTH, A_KV_WIDTH, A_WIDTH, IDX_HEADS * IDX_DIM, IDX_DIM, IDX_HEADS,
            B_Q_RANK, B_KV_RANK, B_ROPE, B_WIDTH, C_WIDTH, C_WIDTH, N_BRANCH * D_MODEL)
IN_OFFSETS = [int(o) for o in np.cumsum(IN_SIZES)[:-1]]

LANES = 128
SUBLANES = 8
B_QK_PAD = 2 * LANES
VMEM_LIMIT_BYTES = 56 * 1024 * 1024
MASK_VALUE = -1e30
LOG2E = math.log2(math.e)
SHIFT_MARGIN = 1.02
MAX_SHIFT_SPAN = 100.0
KNORM_ROWS = 16
DV = 128
ONES_ROWS = 16
DV_AUG = DV + ONES_ROWS
assert DV == A_HEAD_DIM == B_V
KV_TILE = 512
B_HEADS_PER_STEP = 2

SEG_UNITS = dict(gates=48, aq=8, az=8, iq=8, bz=8, cq=8, cz=8, bcq=4, ak=2, av=2, bckv=2, ikw=1, bkpe=1)
SEG_START = {}
_u = 0
for _name, _w in SEG_UNITS.items():
    SEG_START[_name] = _u
    _u += _w
PROJ_UNITS = _u
PROJ_WIDTH = PROJ_UNITS * LANES


def _t5_bucket_table():
    d = np.arange(LANES)
    max_exact = REL_BUCKETS // 2
    nf = np.maximum(d, 1).astype(np.float64)
    large = max_exact + (np.log(nf / max_exact) / math.log(REL_MAX_DIST / max_exact)
                         * (REL_BUCKETS - max_exact)).astype(np.int64)
    large = np.minimum(large, REL_BUCKETS - 1)
    table = np.where(d < max_exact, d, large).astype(np.int32)
    far = int(np.min(np.nonzero(table == REL_BUCKETS - 1)[0]))
    assert np.all(table[far:] == REL_BUCKETS - 1)
    return table, far


T5_TABLE, T5_FAR = _t5_bucket_table()


def _cparams(*sem):
    return pltpu.CompilerParams(dimension_semantics=sem, vmem_limit_bytes=VMEM_LIMIT_BYTES)


def _dot_t(a, b):
    return lax.dot_general(a, b, (((1,), (1,)), ((), ())), preferred_element_type=F32)


def _rms_scale(ss, n):
    return lax.rsqrt(ss * (1.0 / n) + EPS)


def _mem_kv_kernel(mem_ref, g_ref, w_ref, ckn_ref, o_ref, h_ref):
    j = pl.program_id(0)

    @pl.when(j == 0)
    def _():
        m = mem_ref[...]
        r = _rms_scale(jnp.sum(m * m, axis=-1, keepdims=True), D_MODEL)
        h_ref[...] = (m * r * g_ref[...]).astype(BF16)

    y = jnp.dot(h_ref[...], w_ref[...].astype(BF16), preferred_element_type=F32)

    @pl.when(j < C_HEADS)
    def _():
        r = _rms_scale(jnp.sum(y * y, axis=-1, keepdims=True), C_HEAD_DIM)
        o_ref[...] = (y * r * ckn_ref[...]).astype(BF16)

    @pl.when(j >= C_HEADS)
    def _():
        o_ref[...] = y.astype(BF16)


def _mem_kv(mem, mem_g, w_mem_kv, ck_g):
    n_mem = mem.shape[0]
    return pl.pallas_call(
        _mem_kv_kernel,
        grid=(2 * C_HEADS,),
        in_specs=[
            pl.BlockSpec((n_mem, D_MODEL), lambda j: (0, 0)),
            pl.BlockSpec((1, D_MODEL), lambda j: (0, 0)),
            pl.BlockSpec((D_MODEL, C_HEAD_DIM), lambda j: (0, j)),
            pl.BlockSpec((1, C_HEAD_DIM), lambda j: (0, 0)),
        ],
        out_specs=pl.BlockSpec((n_mem, C_HEAD_DIM), lambda j: (0, j)),
        out_shape=jax.ShapeDtypeStruct((n_mem, 2 * C_WIDTH), BF16),
        scratch_shapes=[pltpu.VMEM((n_mem, D_MODEL), BF16)],
        compiler_params=_cparams("arbitrary"),
        name="mem_kv",
    )(mem, mem_g, w_mem_kv, ck_g)


def _in_proj_kernel(x_ref, g_ref, w_ref, o_ref, h_ref, *, row_chunk):
    @pl.when(pl.program_id(1) == 0)
    def _():
        g = g_ref[...]
        for c in range(x_ref.shape[0] // row_chunk):
            rows = pl.ds(c * row_chunk, row_chunk)
            x = x_ref[rows, :]
            r = _rms_scale(jnp.sum(x * x, axis=-1, keepdims=True), D_MODEL)
            h_ref[rows, :] = (x * r * g).astype(BF16)

    o_ref[...] = _dot_t(h_ref[...], w_ref[...]).astype(o_ref.dtype)


def _in_proj(x, norm_g, w_cat, *, tm, tn):
    s = x.shape[0]
    return pl.pallas_call(
        functools.partial(_in_proj_kernel, row_chunk=min(tm, 256)),
        grid=(s // tm, PROJ_WIDTH // tn),
        in_specs=[
            pl.BlockSpec((tm, D_MODEL), lambda i, j: (i, 0)),
            pl.BlockSpec((1, D_MODEL), lambda i, j: (0, 0)),
            pl.BlockSpec((tn, D_MODEL), lambda i, j: (j, 0)),
        ],
        out_specs=pl.BlockSpec((tm, tn), lambda i, j: (i, j)),
        out_shape=jax.ShapeDtypeStruct((s, PROJ_WIDTH), BF16),
        scratch_shapes=[pltpu.VMEM((tm, D_MODEL), BF16)],
        compiler_params=_cparams("parallel", "arbitrary"),
        name="in_proj",
    )(x, norm_g, w_cat)


def _rope128(x, cos, sin):
    half = B_ROPE // 2
    lane = lax.broadcasted_iota(I32, x.shape, 1)
    partner = jnp.where(lane < half, pltpu.roll(x, LANES - half, 1), pltpu.roll(x, half, 1))
    return x * cos + partner * sin


def _prep_kernel(aq_ref, ak_ref, av_ref, ikw_ref, bcq_ref, bckv_ref, bkpe_ref, cq_ref,
                 cos_ref, sin_ref, aqg_ref, akg_ref, bqlg_ref, bkvlg_ref, bqg_ref, bkg_ref, cqg_ref,
                 wuq_ref, wuk_ref, wuvt_ref,
                 aqn_ref, akn_ref, kab_ref, wt_ref, avt_ref, qb_ref, kb_ref, vbt_ref, cqn_ref, kn_ref):
    cos = cos_ref[...]
    sin = sin_ref[...]

    @pl.when(pl.program_id(0) == 0)
    def _():
        kn_ref[...] = jnp.zeros(kn_ref.shape, F32)

    def note_key_norm(row, sq_norms):
        top = jnp.max(sq_norms, axis=0, keepdims=True)
        kn_ref[row:row + 1, :] = jnp.maximum(kn_ref[row:row + 1, :], top)

    eye = jnp.where(lax.broadcasted_iota(I32, (LANES, LANES), 0) == lax.broadcasted_iota(I32, (LANES, LANES), 1),
                    1.0, 0.0).astype(BF16)

    aqg = aqg_ref[...] * (A_SCALE * LOG2E)
    for h in range(A_HEADS):
        cols = slice(h * A_HEAD_DIM, (h + 1) * A_HEAD_DIM)
        v = aq_ref[:, cols].astype(F32)
        r = _rms_scale(jnp.sum(v * v, axis=-1, keepdims=True), A_HEAD_DIM)
        aqn_ref[:, cols] = (v * r * aqg).astype(BF16)
    akg = akg_ref[...]
    for h in range(A_KV_HEADS):
        cols = slice(h * A_HEAD_DIM, (h + 1) * A_HEAD_DIM)
        v = ak_ref[:, cols].astype(F32)
        r = _rms_scale(jnp.sum(v * v, axis=-1, keepdims=True), A_HEAD_DIM)
        kn = v * r * akg
        akn_ref[:, cols] = kn.astype(BF16)
        note_key_norm(B_HEADS + h, jnp.sum(kn * kn, axis=-1, keepdims=True))
        avt_ref[h, 0, 0:DV] = _dot_t(eye, av_ref[:, cols]).astype(BF16)
        avt_ref[h, 0, DV:DV_AUG] = jnp.ones((ONES_ROWS, avt_ref.shape[-1]), BF16)

    ikw = ikw_ref[...]
    lane = lax.broadcasted_iota(I32, ikw.shape, 1)
    zero = jnp.zeros_like(ikw)
    kab_ref[:, 0:LANES] = jnp.where(lane < IDX_DIM, ikw, zero)
    kab_ref[:, LANES:2 * LANES] = jnp.where(lane >= IDX_DIM, pltpu.roll(ikw.astype(F32), IDX_DIM, 1).astype(BF16), zero)
    wt_ref[...] = _dot_t(eye, ikw) * IDX_SCALE

    cq_lat = bcq_ref[...].astype(F32)
    r = _rms_scale(jnp.sum(cq_lat * cq_lat, axis=-1, keepdims=True), B_Q_RANK)
    qlat = (cq_lat * r * bqlg_ref[...]).astype(BF16)
    bqg = bqg_ref[...] * (B_SCALE * LOG2E)
    for h in range(B_HEADS):
        cols = slice(h * B_QK_PAD, (h + 1) * B_QK_PAD)
        qh = jnp.dot(qlat, wuq_ref[:, cols], preferred_element_type=F32)
        r = _rms_scale(jnp.sum(qh * qh, axis=-1, keepdims=True), B_QK_DIM)
        qn = qh * r * bqg
        qb_ref[:, h * B_QK_PAD:h * B_QK_PAD + LANES] = qn[:, :LANES].astype(BF16)
        qb_ref[:, h * B_QK_PAD + LANES:(h + 1) * B_QK_PAD] = _rope128(qn[:, LANES:], cos, sin).astype(BF16)

    ckv = bckv_ref[...].astype(F32)
    r = _rms_scale(jnp.sum(ckv * ckv, axis=-1, keepdims=True), B_KV_RANK)
    kvlat = (ckv * r * bkvlg_ref[...]).astype(BF16)
    kpe = bkpe_ref[...].astype(F32)
    ss_pe = jnp.sum(kpe * kpe, axis=-1, keepdims=True)
    bkg = bkg_ref[...]
    kpe_rot = _rope128(kpe * bkg[:, LANES:], cos, sin)
    for h in range(B_HEADS):
        kn = jnp.dot(kvlat, wuk_ref[:, h * B_NOPE:(h + 1) * B_NOPE], preferred_element_type=F32)
        r = _rms_scale(jnp.sum(kn * kn, axis=-1, keepdims=True) + ss_pe, B_QK_DIM)
        k_nope = kn * r * bkg[:, :LANES]
        k_pe = kpe_rot * r
        kb_ref[:, h * B_QK_PAD:h * B_QK_PAD + LANES] = k_nope.astype(BF16)
        kb_ref[:, h * B_QK_PAD + LANES:(h + 1) * B_QK_PAD] = k_pe.astype(BF16)
        note_key_norm(h, jnp.sum(k_nope * k_nope, axis=-1, keepdims=True)
                      + jnp.sum(k_pe * k_pe, axis=-1, keepdims=True))
        vbt_ref[h, 0, 0:DV] = _dot_t(wuvt_ref[h], kvlat).astype(BF16)
        vbt_ref[h, 0, DV:DV_AUG] = jnp.ones((ONES_ROWS, vbt_ref.shape[-1]), BF16)

    cqg = cqg_ref[...] * C_SCALE
    for h in range(C_HEADS):
        cols = slice(h * C_HEAD_DIM, (h + 1) * C_HEAD_DIM)
        v = cq_ref[:, cols].astype(F32)
        r = _rms_scale(jnp.sum(v * v, axis=-1, keepdims=True), C_HEAD_DIM)
        cqn_ref[:, cols] = (v * r * cqg).astype(BF16)


def _seg_spec(tm, name):
    units = SEG_UNITS[name]
    blk = SEG_START[name] // units
    return pl.BlockSpec((tm, units * LANES), lambda i: (i, blk))


def _full_spec(shape):
    return pl.BlockSpec(shape, lambda i: (0,) * len(shape))


def _prep(proj, cos128, sin128, aqg, akg, bqlg, bkvlg, bqg, bkg, cqg, wuq, wuk, wuvt):
    s = proj.shape[0]
    tm = KV_TILE
    nt = s // tm
    row = lambda w: pl.BlockSpec((tm, w), lambda i: (i, 0))
    tiles_t = lambda n: pl.BlockSpec((n, 1, DV_AUG, tm), lambda i: (0, i, 0, 0))
    small = [aqg, akg, bqlg, bkvlg, bqg, bkg, cqg, wuq, wuk, wuvt]
    out_specs = [row(A_WIDTH), row(A_KV_WIDTH), row(2 * LANES),
                 pl.BlockSpec((LANES, tm), lambda i: (0, i)), tiles_t(A_KV_HEADS),
                 row(B_HEADS * B_QK_PAD), row(B_HEADS * B_QK_PAD), tiles_t(B_HEADS), row(C_WIDTH),
                 _full_spec((KNORM_ROWS, LANES))]
    out_shape = [
        jax.ShapeDtypeStruct((s, A_WIDTH), BF16),
        jax.ShapeDtypeStruct((s, A_KV_WIDTH), BF16),
        jax.ShapeDtypeStruct((s, 2 * LANES), BF16),
        jax.ShapeDtypeStruct((LANES, s), F32),
        jax.ShapeDtypeStruct((A_KV_HEADS, nt, DV_AUG, tm), BF16),
        jax.ShapeDtypeStruct((s, B_HEADS * B_QK_PAD), BF16),
        jax.ShapeDtypeStruct((s, B_HEADS * B_QK_PAD), BF16),
        jax.ShapeDtypeStruct((B_HEADS, nt, DV_AUG, tm), BF16),
        jax.ShapeDtypeStruct((s, C_WIDTH), BF16),
        jax.ShapeDtypeStruct((KNORM_ROWS, LANES), F32),
    ]
    return pl.pallas_call(
        _prep_kernel,
        grid=(nt,),
        in_specs=[_seg_spec(tm, n) for n in ("aq", "ak", "av", "ikw", "bcq", "bckv", "bkpe", "cq")]
        + [row(LANES), row(LANES)] + [_full_spec(a.shape) for a in small],
        out_specs=out_specs,
        out_shape=out_shape,
        compiler_params=_cparams("arbitrary"),
        name="prep",
    )(*([proj] * 8), cos128, sin128, *small)


def _reduce_keys(x, op):
    tk, tq = x.shape
    n = tk // SUBLANES
    assert n & (n - 1) == 0
    t = x.reshape(n, SUBLANES, tq)
    while n > 1:
        n //= 2
        t = op(t[:n], t[n:2 * n])
    red = jnp.max if op is jnp.maximum else jnp.sum
    return red(t[0], axis=0, keepdims=True)


def _softmax_step_t(st, vt, m_ref, acc_ref):
    m_prev = m_ref[...]
    m_new = jnp.maximum(m_prev, _reduce_keys(st, jnp.maximum))
    alpha = jnp.exp2(m_prev - m_new)
    p = jnp.exp2(st - m_new).astype(BF16)
    acc_ref[...] = alpha * acc_ref[...] + jnp.dot(vt, p, preferred_element_type=F32)
    m_ref[...] = m_new


def _fixed_shift_step_t(st, vt, shift, acc_ref):
    p = jnp.exp2(st - shift).astype(BF16)
    acc_ref[...] += jnp.dot(vt, p, preferred_element_type=F32)


def _normalised_output(acc):
    return (acc[0:DV] / acc[DV:DV + 1]).T


def _logit_bound(q, kmax2):
    qf = q.astype(F32)
    ones = jnp.ones((SUBLANES, q.shape[1]), BF16)
    qn2 = _dot_t(ones, (qf * qf).astype(BF16))[0:1]
    return jnp.sqrt(qn2 * kmax2) * SHIFT_MARGIN


def _causal_ok_t(i, j, tq, tk):
    key = j * tk + lax.broadcasted_iota(I32, (tk, tq), 0)
    qry = i * tq + lax.broadcasted_iota(I32, (tk, tq), 1)
    return key <= qry


def _init_softmax_state(m_ref, acc_ref):
    m_ref[...] = jnp.full(m_ref.shape, MASK_VALUE, F32)
    acc_ref[...] = jnp.zeros(acc_ref.shape, F32)


def _pingpong_tiles(n_full, logits, consume, buf_a, buf_b):
    def pair(p, c):
        j = 2 * p
        logits(j + 1, buf_b)
        consume(j, buf_a, False)
        logits(j + 2, buf_a)
        consume(j + 1, buf_b, False)
        return c

    lax.fori_loop(0, n_full // 2, pair, 0)
    odd = n_full % 2 == 1

    @pl.when(odd)
    def _():
        logits(n_full, buf_b)
        consume(n_full - 1, buf_a, False)
        consume(n_full, buf_b, True)

    @pl.when(jnp.logical_not(odd))
    def _():
        consume(n_full, buf_a, True)


def _attn_b_kernel(q_ref, k_ref, vt_ref, kn_ref, o_ref, sa_ref, sb_ref, m_ref, acc_ref, *, tq, tk):
    hp = pl.program_id(0)
    i = pl.program_id(1)
    _init_softmax_state(m_ref, acc_ref)
    j_diag = (i * tq) // tk
    heads = range(B_HEADS_PER_STEP)

    def logits(j, buf):
        rows = pl.ds(pl.multiple_of(j * tk, tk), tk)
        for hh in heads:
            cols = slice(hh * B_QK_PAD, (hh + 1) * B_QK_PAD)
            buf[hh] = _dot_t(k_ref[rows, cols], q_ref[:, cols])

    logits(0, sa_ref)
    shifts = [_logit_bound(q_ref[:, hh * B_QK_PAD:(hh + 1) * B_QK_PAD],
                           kn_ref[pl.ds(hp * B_HEADS_PER_STEP + hh, 1), 0:1]) for hh in heads]
    fixed_ok = jnp.max(2.0 * functools.reduce(jnp.maximum, shifts)) <= MAX_SHIFT_SPAN

    def consume(j, buf, masked, fixed):
        for hh in heads:
            st = buf[hh]
            if masked:
                st = jnp.where(_causal_ok_t(i, j, tq, tk), st, MASK_VALUE)
            if fixed:
                _fixed_shift_step_t(st, vt_ref[hh, j], shifts[hh], acc_ref.at[hh])
            else:
                _softmax_step_t(st, vt_ref[hh, j], m_ref.at[hh], acc_ref.at[hh])

    @pl.when(fixed_ok)
    def _():
        _pingpong_tiles(j_diag, logits, functools.partial(consume, fixed=True), sa_ref, sb_ref)

    @pl.when(jnp.logical_not(fixed_ok))
    def _():
        _pingpong_tiles(j_diag, logits, functools.partial(consume, fixed=False), sa_ref, sb_ref)

    for hh in heads:
        o_ref[:, hh * B_V:(hh + 1) * B_V] = _normalised_output(acc_ref[hh]).astype(o_ref.dtype)


def _attn_b(qb, kb, vbt, knorm2, *, tq):
    s = qb.shape[0]
    tk = KV_TILE
    hps = B_HEADS_PER_STEP
    assert tk % tq == 0 or tq % tk == 0
    assert tq <= tk
    return pl.pallas_call(
        functools.partial(_attn_b_kernel, tq=tq, tk=tk),
        grid=(B_HEADS // hps, s // tq),
        in_specs=[
            pl.BlockSpec((tq, hps * B_QK_PAD), lambda h, i: (i, h)),
            pl.BlockSpec((s, hps * B_QK_PAD), lambda h, i: (0, h)),
            pl.BlockSpec((hps, s // tk, DV_AUG, tk), lambda h, i: (h, 0, 0, 0)),
            pl.BlockSpec((KNORM_ROWS, LANES), lambda h, i: (0, 0)),
        ],
        out_specs=pl.BlockSpec((tq, hps * B_V), lambda h, i: (i, h)),
        out_shape=jax.ShapeDtypeStruct((s, B_WIDTH), BF16),
        scratch_shapes=[pltpu.VMEM((hps, tk, tq), F32), pltpu.VMEM((hps, tk, tq), F32),
                        pltpu.VMEM((hps, 1, tq), F32), pltpu.VMEM((hps, DV_AUG, tq), F32)],
        compiler_params=_cparams("parallel", "arbitrary"),
        name="attn_b",
    )(qb, kb, vbt, knorm2)


KEY_BITS = 32
COARSE_BITS = 16
BISECT_GROUP = 4
BRACKET_GROUP = 2
PACKED_SUBLANES = 16
KEY_NEG_INF = (0xFF800000 ^ 0x7FFFFFFF) - (1 << 32)
BRACKET_HALF = (1 << (KEY_BITS - COARSE_BITS - 1)) + 1
BRACKET_STEPS = (KEY_BITS - COARSE_BITS + 1 + BRACKET_GROUP) // BRACKET_GROUP * BRACKET_GROUP
assert KEY_BITS % BISECT_GROUP == 0
assert (1 << BRACKET_STEPS) > 2 * BRACKET_HALF + 1


def _key_as_f32(key):
    bits = key ^ (lax.shift_right_arithmetic(key, 31) & 0x7FFFFFFF)
    return lax.bitcast_convert_type(bits, F32)


def _coarse_key_as_bf16(k):
    b = k ^ (lax.shift_right_arithmetic(k, COARSE_BITS - 1) & ((1 << (COARSE_BITS - 1)) - 1))
    return lax.bitcast_convert_type(lax.shift_left(b, COARSE_BITS), F32).astype(BF16)


def _count_hits_packed(hit):
    tk, tq = hit.shape
    n = tk // PACKED_SUBLANES
    assert n <= 256
    t = jnp.where(hit, jnp.ones((), BF16), jnp.zeros((), BF16)).reshape(n, PACKED_SUBLANES, tq)
    while n > 1:
        n //= 2
        t = t[:n] + t[n:2 * n]
    return t[0].astype(F32)


def _count_hits(hit):
    tk, tq = hit.shape
    n = tk // SUBLANES
    t = jnp.where(hit, 1, 0).reshape(n, SUBLANES, tq)
    while n > 1:
        n //= 2
        t = t[:n] + t[n:2 * n]
    return t[0]


def _attn_a_kernel(qmin_ref, kmax_ref,
                   iq_ref, wt_ref, kab_ref, aqn_ref, akn_ref, avt_ref, pq_ref, pk_ref, lut_ref, kn_ref,
                   o_ref,
                   score_ref, coarse_ref, thr_ref, settled_ref, sa_ref, sb_ref, pa_ref, pb_ref, m_ref, acc_ref,
                   *, tq, tk, topk):
    i = pl.program_id(0)
    j_diag = (i * tq) // tk
    n_tiles = j_diag + 1
    int_min = jnp.int32(-2 ** 31)

    def head_pair_dots(j, p):
        rows = pl.ds(pl.multiple_of(j * tk, tk), tk)
        rhs = iq_ref[:, p * LANES:(p + 1) * LANES]
        return _dot_t(kab_ref[rows, 0:LANES], rhs), _dot_t(kab_ref[rows, LANES:2 * LANES], rhs)

    def score_tile(j, masked):
        acc = jnp.zeros((tk, tq), F32)
        n_pairs = IDX_HEADS // 2
        for p in range(n_pairs):
            d_lo, d_hi = (sa_ref[...], sb_ref[...]) if p == 0 else head_pair_dots(j, p)
            if p == n_pairs - 1 and not masked:
                sa_ref[...], sb_ref[...] = head_pair_dots(j + 1, 0)
            w_row = IDX_DIM + 2 * p
            acc = acc + jnp.maximum(d_lo, 0.0) * wt_ref[w_row:w_row + 1, :]
            acc = acc + jnp.maximum(d_hi, 0.0) * wt_ref[w_row + 1:w_row + 2, :]
        if masked:
            acc = jnp.where(_causal_ok_t(i, j, tq, tk), acc, -jnp.inf)
        score_ref[j] = acc
        coarse_ref[j] = acc.astype(BF16)

    def score_body(j, c):
        score_tile(j, False)
        return c

    sa_ref[...], sb_ref[...] = head_pair_dots(0, 0)
    lax.fori_loop(0, j_diag, score_body, 0)
    score_tile(j_diag, True)

    def count_queries(pred_fn):
        def body(j, cnt):
            return cnt + _count_hits(pred_fn(score_ref[j], j))
        cnt = lax.fori_loop(0, n_tiles, body, jnp.zeros((SUBLANES, tq), I32))
        return jnp.sum(cnt, axis=0, keepdims=True)

    def count_at_or_above(cand_key):
        cand = _key_as_f32(cand_key)
        cnt = count_queries(lambda x, j: x >= cand)
        return jnp.where(cand_key <= KEY_NEG_INF, n_tiles * tk, cnt)

    def store_threshold(thr_key, settled):
        thr_ref[...] = jnp.where(thr_key <= KEY_NEG_INF, -jnp.inf, _key_as_f32(thr_key))
        settled_ref[...] = settled

    def bisect_all_bits():
        def cond(c):
            g, _, s = c
            return (g < KEY_BITS // BISECT_GROUP) & (jnp.min(s) == 0)

        def body(c):
            g, t, s = c
            for u in range(BISECT_GROUP):
                bit = KEY_BITS - 1 - (g * BISECT_GROUP + u)
                cand = t + lax.shift_left(jnp.int32(1), bit)
                cnt = count_at_or_above(cand)
                t = jnp.where((s == 0) & (cnt >= topk), cand, t)
                s = jnp.where(cnt == topk, 1, s)
            return g + 1, t, s

        _, t, s = lax.while_loop(cond, body, (jnp.int32(0), jnp.full((1, tq), int_min, I32),
                                              jnp.zeros((1, tq), I32)))
        store_threshold(t, s)

    def bisect_bracket(lo, hi, cnt_lo):
        def cond(c):
            g, lo, hi, _, s = c
            done = (s == 1) | (hi - lo <= 1)
            return (g < BRACKET_STEPS // BRACKET_GROUP) & (jnp.min(jnp.where(done, 1, 0)) == 0)

        def body(c):
            g, lo, hi, t, s = c
            for _ in range(BRACKET_GROUP):
                mid = lo + lax.shift_right_arithmetic(hi - lo, 1)
                cnt = count_at_or_above(mid)
                t = jnp.where((s == 0) & (cnt == topk), mid, t)
                s = jnp.where(cnt == topk, 1, s)
                lo, hi = jnp.where(cnt >= topk, mid, lo), jnp.where(cnt >= topk, hi, mid)
            return g + 1, lo, hi, t, s

        settled0 = jnp.where(cnt_lo == topk, 1, 0)
        _, lo, _, t, s = lax.while_loop(cond, body, (jnp.int32(0), lo, hi, lo, settled0))
        store_threshold(jnp.where(s == 1, t, lo), s)

    def count_coarse(cand):
        def body(j, cnt):
            return cnt + _count_hits_packed(coarse_ref[j] >= cand)
        cnt = lax.fori_loop(0, n_tiles, body, jnp.zeros((PACKED_SUBLANES, tq), F32))
        return jnp.sum(cnt, axis=0, keepdims=True)

    def coarse_step(it, t):
        cand = t + lax.shift_left(jnp.int32(1), COARSE_BITS - 1 - it)
        return jnp.where(count_coarse(_coarse_key_as_bf16(cand)) >= topk, cand, t)

    t_coarse = lax.fori_loop(0, COARSE_BITS, coarse_step,
                             jnp.full((1, tq), -(1 << (COARSE_BITS - 1)), I32))

    low_bits = KEY_BITS - COARSE_BITS
    centre = lax.shift_left(t_coarse, low_bits) + jnp.where(t_coarse < 0, (1 << low_bits) - 1, 0)
    lo_key, hi_key = centre - BRACKET_HALF, centre + BRACKET_HALF + 1
    cnt_lo = count_at_or_above(lo_key)
    cnt_hi = count_at_or_above(hi_key)
    bracket_ok = (cnt_lo >= topk) & (cnt_hi < topk) & (lo_key < hi_key)
    coarse_ok = jnp.min(jnp.where(bracket_ok, 1, 0)) == 1

    pl.when(coarse_ok)(lambda: bisect_bracket(lo_key, hi_key, cnt_lo))
    pl.when(jnp.logical_not(coarse_ok))(bisect_all_bits)

    thr = thr_ref[...]
    settled = settled_ref[...]

    def drop_excess_ties():
        cnt_gt = count_queries(lambda x, j: x > thr)
        need = topk - cnt_gt
        kpos = lax.broadcasted_iota(I32, (tk, tq), 0)

        def pos_body(it, cut):
            cand = cut + lax.shift_left(jnp.int32(1), 30 - it)
            cnt = count_queries(lambda x, j: (x == thr) & (j * tk + kpos < cand))
            return jnp.where(cnt < need, cand, cut)

        cut = lax.fori_loop(0, 31, pos_body, jnp.zeros((1, tq), I32))

        def drop_body(j, c):
            x = score_ref[j]
            drop = (x == thr) & (j * tk + kpos > cut)
            score_ref[j] = jnp.where(drop, -jnp.inf, x)
            return c

        lax.fori_loop(0, n_tiles, drop_body, 0)

    @pl.when(jnp.min(settled) == 0)
    def _():
        cnt_ge = count_queries(lambda x, j: x >= thr)
        pl.when(jnp.max(cnt_ge) > topk)(drop_excess_ties)

    _init_softmax_state(m_ref, acc_ref)
    pq = pq_ref[...]
    lut = lut_ref[...]
    bias_hi = jnp.max(lut, axis=1, keepdims=True)
    bias_span = bias_hi - jnp.min(lut, axis=1, keepdims=True)
    bounds = [_logit_bound(aqn_ref[:, h * A_HEAD_DIM:(h + 1) * A_HEAD_DIM],
                           kn_ref[B_HEADS + h // A_REP:B_HEADS + h // A_REP + 1, 0:1]) for h in range(A_HEADS)]
    shifts = [bounds[h] + bias_hi[h:h + 1] for h in range(A_HEADS)]
    spans = [2.0 * bounds[h] + bias_span[h:h + 1] for h in range(A_HEADS)]
    fixed_ok = jnp.max(functools.reduce(jnp.maximum, spans)) <= MAX_SHIFT_SPAN

    def head_buf(h):
        return (sa_ref, sb_ref)[h % 2]

    def logits(j, h, buf):
        rows = pl.ds(pl.multiple_of(j * tk, tk), tk)
        g = h // A_REP
        buf[...] = _dot_t(akn_ref[rows, g * A_HEAD_DIM:(g + 1) * A_HEAD_DIM],
                          aqn_ref[:, h * A_HEAD_DIM:(h + 1) * A_HEAD_DIM])

    def attend(j, masked, near, fixed):
        rows = pl.ds(pl.multiple_of(j * tk, tk), tk)
        sel = score_ref[j] >= thr
        if masked:
            sel = sel & _causal_ok_t(i, j, tq, tk)
        mask_bias = jnp.where(sel, 0.0, MASK_VALUE)
        if near:
            dist = jnp.clip(pq - pk_ref[rows, :], 0, LANES - 1)

        logits(j, 0, head_buf(0))
        for h in range(A_HEADS):
            if h + 1 < A_HEADS:
                logits(j, h + 1, head_buf(h + 1))
            st = head_buf(h)[...] + mask_bias
            if near:
                table = jnp.broadcast_to(lut_ref[h:h + 1, :], (tk, LANES))
                st = st + jnp.concatenate(
                    [jnp.take_along_axis(table, dist[:, c * LANES:(c + 1) * LANES], axis=1,
                                         mode="promise_in_bounds")
                     for c in range(tq // LANES)], axis=1)
            if fixed:
                _fixed_shift_step_t(st, avt_ref[h // A_REP, j], shifts[h], acc_ref.at[h])
            else:
                _softmax_step_t(st, avt_ref[h // A_REP, j], m_ref.at[h], acc_ref.at[h])

    def is_far(j):
        return qmin_ref[i] - kmax_ref[j] >= T5_FAR

    def attend_dyn(j, masked, fixed):
        far = is_far(j)

        @pl.when(far)
        def _():
            attend(j, masked, False, fixed)

        @pl.when(jnp.logical_not(far))
        def _():
            attend(j, masked, True, fixed)

    def attend_far_pair(j):
        rows = pl.ds(pl.multiple_of(j * tk, tk), 2 * tk)
        mask_bias = jnp.concatenate([jnp.where(score_ref[j] >= thr, 0.0, MASK_VALUE),
                                     jnp.where(score_ref[j + 1] >= thr, 0.0, MASK_VALUE)], axis=0)
        pair_bufs = (pa_ref, pb_ref)

        def pair_logits(h):
            g = h // A_REP
            pair_bufs[h % 2][...] = _dot_t(akn_ref[rows, g * A_HEAD_DIM:(g + 1) * A_HEAD_DIM],
                                           aqn_ref[:, h * A_HEAD_DIM:(h + 1) * A_HEAD_DIM])

        pair_logits(0)
        for h in range(A_HEADS):
            if h + 1 < A_HEADS:
                pair_logits(h + 1)
            p = jnp.exp2(pair_bufs[h % 2][...] + mask_bias - shifts[h]).astype(BF16)
            g = h // A_REP
            acc_ref[h] += (jnp.dot(avt_ref[g, j], p[:tk], preferred_element_type=F32)
                           + jnp.dot(avt_ref[g, j + 1], p[tk:], preferred_element_type=F32))

    def attend_all(fixed):
        def attend_body(j, c):
            attend_dyn(j, False, fixed)
            return c

        if fixed:
            def pair_body(p, c):
                j = 2 * p
                both_far = is_far(j) & is_far(j + 1)

                @pl.when(both_far)
                def _():
                    attend_far_pair(j)

                @pl.when(jnp.logical_not(both_far))
                def _():
                    lax.fori_loop(j, j + 2, attend_body, 0)

                return c

            n_pairs = j_diag // 2
            lax.fori_loop(0, n_pairs, pair_body, 0)
            lax.fori_loop(2 * n_pairs, j_diag, attend_body, 0)
        else:
            lax.fori_loop(0, j_diag, attend_body, 0)
        attend_dyn(j_diag, True, fixed)

    pl.when(fixed_ok)(functools.partial(attend_all, True))
    pl.when(jnp.logical_not(fixed_ok))(functools.partial(attend_all, False))

    for h in range(A_HEADS):
        q_cols = slice(h * A_HEAD_DIM, (h + 1) * A_HEAD_DIM)
        o_ref[:, q_cols] = _normalised_output(acc_ref[h]).astype(o_ref.dtype)


def _attn_a(proj, wt, kab, aqn, akn, avt, pos_row, pos_col, lut_t, knorm2, qmin, kmax, *, tq, topk):
    s = proj.shape[0]
    tk = KV_TILE
    assert tk % tq == 0
    iq_blk = SEG_START["iq"] // SEG_UNITS["iq"]
    grid_spec = pltpu.PrefetchScalarGridSpec(
        num_scalar_prefetch=2,
        grid=(s // tq,),
        in_specs=[
            pl.BlockSpec((tq, IDX_HEADS * IDX_DIM), lambda i, *_: (i, iq_blk)),
            pl.BlockSpec((LANES, tq), lambda i, *_: (0, i)),
            pl.BlockSpec((s, 2 * LANES), lambda i, *_: (0, 0)),
            pl.BlockSpec((tq, A_WIDTH), lambda i, *_: (i, 0)),
            pl.BlockSpec((s, A_KV_WIDTH), lambda i, *_: (0, 0)),
            pl.BlockSpec((A_KV_HEADS, s // tk, DV_AUG, tk), lambda i, *_: (0, 0, 0, 0)),
            pl.BlockSpec((1, tq), lambda i, *_: (0, i)),
            pl.BlockSpec((s, 1), lambda i, *_: (0, 0)),
            pl.BlockSpec((A_HEADS, LANES), lambda i, *_: (0, 0)),
            pl.BlockSpec((KNORM_ROWS, LANES), lambda i, *_: (0, 0)),
        ],
        out_specs=pl.BlockSpec((tq, A_WIDTH), lambda i, *_: (i, 0)),
        scratch_shapes=[
            pltpu.VMEM((s // tk, tk, tq), F32),
            pltpu.VMEM((s // tk, tk, tq), BF16),
            pltpu.VMEM((1, tq), F32),
            pltpu.VMEM((1, tq), I32),
            pltpu.VMEM((tk, tq), F32),
            pltpu.VMEM((tk, tq), F32),
            pltpu.VMEM((2 * tk, tq), F32),
            pltpu.VMEM((2 * tk, tq), F32),
            pltpu.VMEM((A_HEADS, 1, tq), F32),
            pltpu.VMEM((A_HEADS, DV_AUG, tq), F32),
        ],
    )
    return pl.pallas_call(
        functools.partial(_attn_a_kernel, tq=tq, tk=tk, topk=topk),
        grid_spec=grid_spec,
        out_shape=jax.ShapeDtypeStruct((s, A_WIDTH), BF16),
        compiler_params=_cparams("arbitrary"),
        name="attn_a",
    )(qmin, kmax, proj, wt, kab, aqn, akn, avt, pos_row, pos_col, lut_t, knorm2)


def _attn_c_kernel(q_ref, kv_ref, o_ref):
    for h in range(C_HEADS):
        cols = slice(h * C_HEAD_DIM, (h + 1) * C_HEAD_DIM)
        s = _dot_t(q_ref[:, cols], kv_ref[:, cols])
        p = jnp.exp(s - jnp.max(s, axis=-1, keepdims=True))
        o = jnp.dot(p.astype(BF16), kv_ref[:, C_WIDTH + h * C_HEAD_DIM:C_WIDTH + (h + 1) * C_HEAD_DIM],
                    preferred_element_type=F32)
        o_ref[:, cols] = (o / jnp.sum(p, axis=-1, keepdims=True)).astype(o_ref.dtype)


def _attn_c(cqn, mem_kv, *, tm):
    s = cqn.shape[0]
    return pl.pallas_call(
        _attn_c_kernel,
        grid=(s // tm,),
        in_specs=[pl.BlockSpec((tm, C_WIDTH), lambda i: (i, 0)), _full_spec(mem_kv.shape)],
        out_specs=pl.BlockSpec((tm, C_WIDTH), lambda i: (i, 0)),
        out_shape=jax.ShapeDtypeStruct((s, C_WIDTH), BF16),
        compiler_params=_cparams("parallel"),
        name="attn_c",
    )(cqn, mem_kv)


def _merge_out_kernel(x_ref, oa_ref, ob_ref, oc_ref, az_ref, bz_ref, cz_ref, ga_ref, gb_ref, gc_ref,
                      wbr_ref, wout_ref, o_ref):
    merged = None
    for n, (o_r, z_r, g_r) in enumerate(((oa_ref, az_ref, ga_ref), (ob_ref, bz_ref, gb_ref),
                                         (oc_ref, cz_ref, gc_ref))):
        z = z_r[...].astype(F32)
        u = (o_r[...].astype(F32) * (z * jax.nn.sigmoid(z))).astype(BF16)
        y = jnp.dot(u, wbr_ref[n], preferred_element_type=F32)
        t = jax.nn.sigmoid(g_r[...].astype(F32)) * y
        merged = t if merged is None else merged + t
    o_ref[...] = x_ref[...] + jnp.dot(merged.astype(BF16), wout_ref[...], preferred_element_type=F32)


def _merge_out(x, o_a, o_b, o_c, proj, w_branch, w_out, *, tm):
    s = x.shape[0]
    row = lambda w: pl.BlockSpec((tm, w), lambda i: (i, 0))
    gate_blk = SEG_START["gates"] * LANES // D_MODEL
    gate = lambda n: pl.BlockSpec((tm, D_MODEL), lambda i: (i, gate_blk + n))
    single = pl.Buffered(1)
    return pl.pallas_call(
        _merge_out_kernel,
        grid=(s // tm,),
        in_specs=[row(D_MODEL), row(BRANCH_WIDTH), row(BRANCH_WIDTH), row(BRANCH_WIDTH),
                  _seg_spec(tm, "az"), _seg_spec(tm, "bz"), _seg_spec(tm, "cz"),
                  gate(0), gate(1), gate(2),
                  pl.BlockSpec(w_branch.shape, lambda i: (0, 0, 0), pipeline_mode=single),
                  pl.BlockSpec(w_out.shape, lambda i: (0, 0), pipeline_mode=single)],
        out_specs=row(D_MODEL),
        out_shape=jax.ShapeDtypeStruct((s, D_MODEL), F32),
        compiler_params=_cparams("parallel"),
        name="merge_out",
    )(x, o_a, o_b, o_c, proj, proj, proj, proj, proj, proj, w_branch, w_out)


def _regroup_tables():
    names = ("aq", "ak", "av", "az", "iq", "ik", "iw", "bcq", "bckv", "bkpe", "bz", "cq", "cz", "gates")
    src_off = dict(zip(names, [0] + IN_OFFSETS))
    src_off["ikw"] = src_off["ik"]
    valid_rows = dict(ikw=IDX_DIM + IDX_HEADS, bkpe=B_ROPE)
    start, valid = [], []
    for name, units in SEG_UNITS.items():
        for u in range(units):
            start.append(src_off[name] + u * LANES)
            valid.append(valid_rows.get(name, LANES))
    return np.asarray(start, np.int32), np.asarray(valid, np.int32)


REGROUP_START, REGROUP_VALID = _regroup_tables()
REGROUP_ALIGN = math.gcd(*(int(v) for v in REGROUP_START if v))
assert REGROUP_ALIGN % SUBLANES == 0


REGROUP_UNITS_PER_STEP = 4
assert PROJ_UNITS % REGROUP_UNITS_PER_STEP == 0


def _regroup_kernel(start_ref, valid_ref, *refs):
    w_refs, o_ref = refs[:-1], refs[-1]
    row = lax.broadcasted_iota(I32, (LANES, o_ref.shape[1]), 0)
    for u, w_ref in enumerate(w_refs):
        valid = valid_ref[pl.program_id(0) * REGROUP_UNITS_PER_STEP + u]
        o_ref[u * LANES:(u + 1) * LANES, :] = jnp.where(row < valid, w_ref[...], 0.0).astype(BF16)


def _regroup_w_in_t(w_in_t):
    d = w_in_t.shape[1]
    ups = REGROUP_UNITS_PER_STEP

    def window(u):
        return pl.BlockSpec((pl.Element(LANES), pl.Element(d)),
                            lambda b, start, valid: (start[b * ups + u] * REGROUP_ALIGN, 0))

    grid_spec = pltpu.PrefetchScalarGridSpec(
        num_scalar_prefetch=2,
        grid=(PROJ_UNITS // ups,),
        in_specs=[window(u) for u in range(ups)],
        out_specs=pl.BlockSpec((ups * LANES, d), lambda b, *_: (b, 0)),
    )
    return pl.pallas_call(
        _regroup_kernel,
        grid_spec=grid_spec,
        out_shape=jax.ShapeDtypeStruct((PROJ_WIDTH, d), BF16),
        compiler_params=_cparams("arbitrary"),
        name="regroup_w_in",
    )(jnp.asarray(REGROUP_START // REGROUP_ALIGN), jnp.asarray(REGROUP_VALID), *([w_in_t] * ups))


def _pad_heads(w, n_heads, width, pad_to):
    r = w.shape[0]
    w = w.reshape(r, n_heads, width)
    w = jnp.pad(w, ((0, 0), (0, 0), (0, pad_to - width)))
    return w.reshape(r, n_heads * pad_to)


class _TilePlan(NamedTuple):
    in_proj_rows: int
    in_proj_cols: int
    attn_a_q: int
    attn_b_q: int
    attn_c_rows: int
    merge_rows: int


def _tile_plan(seq):
    plan = _TilePlan(in_proj_rows=min(1024, seq), in_proj_cols=1536, attn_a_q=256, attn_b_q=min(512, seq),
                     attn_c_rows=min(512, seq), merge_rows=min(256, seq))
    assert seq % KV_TILE == 0 and PROJ_WIDTH % plan.in_proj_cols == 0
    assert all(seq % t == 0 for t in (plan.in_proj_rows, plan.attn_a_q, plan.attn_b_q, plan.attn_c_rows,
                                      plan.merge_rows))
    return plan


def kernel(x, mem, positions, rel_bias, norm_g, mem_norm_g, w_in, a_q_norm_g, a_k_norm_g, b_q_lat_norm_g,
           b_kv_lat_norm_g, w_b_uq, w_b_ukv, b_q_norm_g, b_k_norm_g, w_mem_kv, c_q_norm_g, c_k_norm_g,
           w_branch, w_out):
    bsz, seq, d = x.shape
    assert d == D_MODEL and norm_g.shape[0] == 1
    topk = min(TOPK_MAX, seq // 4)
    tiles = _tile_plan(seq)
    tq_a, tq_b, tm_in = tiles.attn_a_q, tiles.attn_b_q, tiles.in_proj_rows

    inv_freq = 1.0 / (ROPE_THETA ** (jnp.arange(0, B_ROPE, 2, dtype=F32) / B_ROPE))
    zeros_half = jnp.zeros((B_ROPE,), F32)
    gq_pad = jnp.concatenate([b_q_norm_g[0], zeros_half])[None, :]
    gk_pad = jnp.concatenate([b_k_norm_g[0], zeros_half])[None, :]
    lut_t = ((rel_bias[jnp.asarray(T5_TABLE)] - rel_bias[REL_BUCKETS - 1]).T * LOG2E).astype(F32)

    w_cat = _regroup_w_in_t(jnp.transpose(w_in[0]))
    wuq = _pad_heads(w_b_uq[0], B_HEADS, B_QK_DIM, B_QK_PAD).astype(BF16)
    wukv = w_b_ukv[0].reshape(B_KV_RANK, B_HEADS, B_NOPE + B_V)
    wuk = wukv[:, :, :B_NOPE].reshape(B_KV_RANK, B_HEADS * B_NOPE).astype(BF16)
    wuvt = jnp.transpose(wukv[:, :, B_NOPE:], (1, 2, 0)).astype(BF16)
    wbr = w_branch[0].astype(BF16)
    wout = w_out[0].astype(BF16)

    outs = []
    for b in range(bsz):
        pos = positions[b]
        ang = pos.astype(F32)[:, None] * inv_freq
        cos, sin = jnp.cos(ang), jnp.sin(ang)
        zpad = jnp.zeros((seq, LANES - B_ROPE), F32)
        cos128 = jnp.concatenate([cos, cos, zpad], axis=1)
        sin128 = jnp.concatenate([-sin, sin, zpad], axis=1)

        mem_kv = _mem_kv(mem[b], mem_norm_g[0][None, :], w_mem_kv[0], c_k_norm_g[0][None, :])
        proj = _in_proj(x[b], norm_g[0][None, :], w_cat, tm=tm_in, tn=tiles.in_proj_cols)
        aqn, akn, kab, wt, avt, qb, kb, vbt, cqn, knorm2 = _prep(
            proj, cos128, sin128, a_q_norm_g[0][None, :], a_k_norm_g[0][None, :],
            b_q_lat_norm_g[0][None, :], b_kv_lat_norm_g[0][None, :], gq_pad, gk_pad,
            c_q_norm_g[0][None, :], wuq, wuk, wuvt)
        o_b = _attn_b(qb, kb, vbt, knorm2, tq=tq_b)
        qmin = jnp.min(pos.reshape(seq // tq_a, tq_a), axis=1)
        kmax = jnp.max(pos.reshape(seq // KV_TILE, KV_TILE), axis=1)
        o_a = _attn_a(proj, wt, kab, aqn, akn, avt, pos[None, :], pos[:, None],
                      lut_t, knorm2, qmin, kmax, tq=tq_a, topk=topk)
        o_c = _attn_c(cqn, mem_kv, tm=tiles.attn_c_rows)
        outs.append(_merge_out(x[b], o_a, o_b, o_c, proj, wbr, wout, tm=tiles.merge_rows))
    return jnp.stack(outs, axis=0)
```

```python
import functools
import math
from typing import NamedTuple

import numpy as np
import jax
import jax.numpy as jnp
from jax import lax
from jax.experimental import pallas as pl
from jax.experimental.pallas import tpu as pltpu

F32 = jnp.float32
BF16 = jnp.bfloat16
I32 = jnp.int32

D_MODEL = 2048
BRANCH_WIDTH = 1024
N_BRANCH = 3
A_HEADS = 8
A_KV_HEADS = 2
A_HEAD_DIM = 128
A_WIDTH = A_HEADS * A_HEAD_DIM
A_KV_WIDTH = A_KV_HEADS * A_HEAD_DIM
A_REP = A_HEADS // A_KV_HEADS
IDX_HEADS = 16
IDX_DIM = 64
TOPK_MAX = 256
IDX_SCALE = (IDX_DIM ** -0.5) * (IDX_HEADS ** -0.5)
A_SCALE = A_HEAD_DIM ** -0.5
B_HEADS = 8
B_Q_RANK = 512
B_KV_RANK = 256
B_NOPE = 128
B_ROPE = 64
B_QK_DIM = B_NOPE + B_ROPE
B_V = 128
B_WIDTH = B_HEADS * B_V
B_SCALE = B_QK_DIM ** -0.5
ROPE_THETA = 10000.0
C_HEADS = 4
C_HEAD_DIM = 256
C_WIDTH = C_HEADS * C_HEAD_DIM
C_SCALE = C_HEAD_DIM ** -0.5
REL_BUCKETS = 32
REL_MAX_DIST = 128
EPS = 1e-6

IN_SIZES = (A_WIDTH, A_KV_WIDTH, A_KV_WIDTH, A_WIDTH, IDX_HEADS * IDX_DIM, IDX_DIM, IDX_HEADS,
            B_Q_RANK, B_KV_RANK, B_ROPE, B_WIDTH, C_WIDTH, C_WIDTH, N_BRANCH * D_MODEL)
IN_OFFSETS = [int(o) for o in np.cumsum(IN_SIZES)[:-1]]

LANES = 128
SUBLANES = 8
B_QK_PAD = 2 * LANES
VMEM_LIMIT_BYTES = 56 * 1024 * 1024
MASK_VALUE = -1e30
LOG2E = math.log2(math.e)
SHIFT_MARGIN = 1.02
MAX_SHIFT_SPAN = 100.0
KNORM_ROWS = 16
DV = 128
ONES_ROWS = 16
DV_AUG = DV + ONES_ROWS
assert DV == A_HEAD_DIM == B_V
KV_TILE = 512
B_HEADS_PER_STEP = 2

SEG_UNITS = dict(gates=48, aq=8, az=8, iq=8, bz=8, cq=8, cz=8, bcq=4, ak=2, av=2, bckv=2, ikw=1, bkpe=1)
SEG_START = {}
_u = 0
for _name, _w in SEG_UNITS.items():
    SEG_START[_name] = _u
    _u += _w
PROJ_UNITS = _u
PROJ_WIDTH = PROJ_UNITS * LANES


def _t5_bucket_table():
    d = np.arange(LANES)
    max_exact = REL_BUCKETS // 2
    nf = np.maximum(d, 1).astype(np.float64)
    large = max_exact + (np.log(nf / max_exact) / math.log(REL_MAX_DIST / max_exact)
                         * (REL_BUCKETS - max_exact)).astype(np.int64)
    large = np.minimum(large, REL_BUCKETS - 1)
    table = np.where(d < max_exact, d, large).astype(np.int32)
    far = int(np.min(np.nonzero(table == REL_BUCKETS - 1)[0]))
    assert np.all(table[far:] == REL_BUCKETS - 1)
    return table, far


T5_TABLE, T5_FAR = _t5_bucket_table()


def _cparams(*sem):
    return pltpu.CompilerParams(dimension_semantics=sem, vmem_limit_bytes=VMEM_LIMIT_BYTES)


def _dot_t(a, b):
    return lax.dot_general(a, b, (((1,), (1,)), ((), ())), preferred_element_type=F32)


def _rms_scale(ss, n):
    return lax.rsqrt(ss * (1.0 / n) + EPS)


def _mem_kv_kernel(mem_ref, g_ref, w_ref, ckn_ref, o_ref, h_ref):
    j = pl.program_id(0)

    @pl.when(j == 0)
    def _():
        m = mem_ref[...]
        r = _rms_scale(jnp.sum(m * m, axis=-1, keepdims=True), D_MODEL)
        h_ref[...] = (m * r * g_ref[...]).astype(BF16)

    y = jnp.dot(h_ref[...], w_ref[...].astype(BF16), preferred_element_type=F32)

    @pl.when(j < C_HEADS)
    def _():
        r = _rms_scale(jnp.sum(y * y, axis=-1, keepdims=True), C_HEAD_DIM)
        o_ref[...] = (y * r * ckn_ref[...]).astype(BF16)

    @pl.when(j >= C_HEADS)
    def _():
        o_ref[...] = y.astype(BF16)


def _mem_kv(mem, mem_g, w_mem_kv, ck_g):
    n_mem = mem.shape[0]
    return pl.pallas_call(
        _mem_kv_kernel,
        grid=(2 * C_HEADS,),
        in_specs=[
            pl.BlockSpec((n_mem, D_MODEL), lambda j: (0, 0)),
            pl.BlockSpec((1, D_MODEL), lambda j: (0, 0)),
            pl.BlockSpec((D_MODEL, C_HEAD_DIM), lambda j: (0, j)),
            pl.BlockSpec((1, C_HEAD_DIM), lambda j: (0, 0)),
        ],
        out_specs=pl.BlockSpec((n_mem, C_HEAD_DIM), lambda j: (0, j)),
        out_shape=jax.ShapeDtypeStruct((n_mem, 2 * C_WIDTH), BF16),
        scratch_shapes=[pltpu.VMEM((n_mem, D_MODEL), BF16)],
        compiler_params=_cparams("arbitrary"),
        name="mem_kv",
    )(mem, mem_g, w_mem_kv, ck_g)


def _in_proj_kernel(x_ref, g_ref, w_ref, o_ref, h_ref, *, row_chunk):
    @pl.when(pl.program_id(1) == 0)
    def _():
        g = g_ref[...]
        for c in range(x_ref.shape[0] // row_chunk):
            rows = pl.ds(c * row_chunk, row_chunk)
            x = x_ref[rows, :]
            r = _rms_scale(jnp.sum(x * x, axis=-1, keepdims=True), D_MODEL)
            h_ref[rows, :] = (x * r * g).astype(BF16)

    o_ref[...] = _dot_t(h_ref[...], w_ref[...]).astype(o_ref.dtype)


def _in_proj(x, norm_g, w_cat, *, tm, tn):
    s = x.shape[0]
    return pl.pallas_call(
        functools.partial(_in_proj_kernel, row_chunk=min(tm, 256)),
        grid=(s // tm, PROJ_WIDTH // tn),
        in_specs=[
            pl.BlockSpec((tm, D_MODEL), lambda i, j: (i, 0)),
            pl.BlockSpec((1, D_MODEL), lambda i, j: (0, 0)),
            pl.BlockSpec((tn, D_MODEL), lambda i, j: (j, 0)),
        ],
        out_specs=pl.BlockSpec((tm, tn), lambda i, j: (i, j)),
        out_shape=jax.ShapeDtypeStruct((s, PROJ_WIDTH), BF16),
        scratch_shapes=[pltpu.VMEM((tm, D_MODEL), BF16)],
        compiler_params=_cparams("parallel", "arbitrary"),
        name="in_proj",
    )(x, norm_g, w_cat)


def _rope128(x, cos, sin):
    half = B_ROPE // 2
    lane = lax.broadcasted_iota(I32, x.shape, 1)
    partner = jnp.where(lane < half, pltpu.roll(x, LANES - half, 1), pltpu.roll(x, half, 1))
    return x * cos + partner * sin


def _prep_kernel(aq_ref, ak_ref, av_ref, ikw_ref, bcq_ref, bckv_ref, bkpe_ref, cq_ref,
                 cos_ref, sin_ref, aqg_ref, akg_ref, bqlg_ref, bkvlg_ref, bqg_ref, bkg_ref, cqg_ref,
                 wuq_ref, wuk_ref, wuvt_ref,
                 aqn_ref, akn_ref, kab_ref, wt_ref, avt_ref, qb_ref, kb_ref, vbt_ref, cqn_ref, kn_ref):
    cos = cos_ref[...]
    sin = sin_ref[...]

    @pl.when(pl.program_id(0) == 0)
    def _():
        kn_ref[...] = jnp.zeros(kn_ref.shape, F32)

    def note_key_norm(row, sq_norms):
        top = jnp.max(sq_norms, axis=0, keepdims=True)
        kn_ref[row:row + 1, :] = jnp.maximum(kn_ref[row:row + 1, :], top)

    eye = jnp.where(lax.broadcasted_iota(I32, (LANES, LANES), 0) == lax.broadcasted_iota(I32, (LANES, LANES), 1),
                    1.0, 0.0).astype(BF16)

    aqg = aqg_ref[...] * (A_SCALE * LOG2E)
    for h in range(A_HEADS):
        cols = slice(h * A_HEAD_DIM, (h + 1) * A_HEAD_DIM)
        v = aq_ref[:, cols].astype(F32)
        r = _rms_scale(jnp.sum(v * v, axis=-1, keepdims=True), A_HEAD_DIM)
        aqn_ref[:, cols] = (v * r * aqg).astype(BF16)
    akg = akg_ref[...]
    for h in range(A_KV_HEADS):
        cols = slice(h * A_HEAD_DIM, (h + 1) * A_HEAD_DIM)
        v = ak_ref[:, cols].astype(F32)
        r = _rms_scale(jnp.sum(v * v, axis=-1, keepdims=True), A_HEAD_DIM)
        kn = v * r * akg
        akn_ref[:, cols] = kn.astype(BF16)
        note_key_norm(B_HEADS + h, jnp.sum(kn * kn, axis=-1, keepdims=True))
        avt_ref[h, 0, 0:DV] = _dot_t(eye, av_ref[:, cols]).astype(BF16)
        avt_ref[h, 0, DV:DV_AUG] = jnp.ones((ONES_ROWS, avt_ref.shape[-1]), BF16)

    ikw = ikw_ref[...]
    lane = lax.broadcasted_iota(I32, ikw.shape, 1)
    zero = jnp.zeros_like(ikw)
    kab_ref[:, 0:LANES] = jnp.where(lane < IDX_DIM, ikw, zero)
    kab_ref[:, LANES:2 * LANES] = jnp.where(lane >= IDX_DIM, pltpu.roll(ikw.astype(F32), IDX_DIM, 1).astype(BF16), zero)
    wt_ref[...] = _dot_t(eye, ikw) * IDX_SCALE

    cq_lat = bcq_ref[...].astype(F32)
    r = _rms_scale(jnp.sum(cq_lat * cq_lat, axis=-1, keepdims=True), B_Q_RANK)
    qlat = (cq_lat * r * bqlg_ref[...]).astype(BF16)
    bqg = bqg_ref[...] * (B_SCALE * LOG2E)
    for h in range(B_HEADS):
        cols = slice(h * B_QK_PAD, (h + 1) * B_QK_PAD)
        qh = jnp.dot(qlat, wuq_ref[:, cols], preferred_element_type=F32)
        r = _rms_scale(jnp.sum(qh * qh, axis=-1, keepdims=True), B_QK_DIM)
        qn = qh * r * bqg
        qb_ref[:, h * B_QK_PAD:h * B_QK_PAD + LANES] = qn[:, :LANES].astype(BF16)
        qb_ref[:, h * B_QK_PAD + LANES:(h + 1) * B_QK_PAD] = _rope128(qn[:, LANES:], cos, sin).astype(BF16)

    ckv = bckv_ref[...].astype(F32)
    r = _rms_scale(jnp.sum(ckv * ckv, axis=-1, keepdims=True), B_KV_RANK)
    kvlat = (ckv * r * bkvlg_ref[...]).astype(BF16)
    kpe = bkpe_ref[...].astype(F32)
    ss_pe = jnp.sum(kpe * kpe, axis=-1, keepdims=True)
    bkg = bkg_ref[...]
    kpe_rot = _rope128(kpe * bkg[:, LANES:], cos, sin)
    for h in range(B_HEADS):
        kn = jnp.dot(kvlat, wuk_ref[:, h * B_NOPE:(h + 1) * B_NOPE], preferred_element_type=F32)
        r = _rms_scale(jnp.sum(kn * kn, axis=-1, keepdims=True) + ss_pe, B_QK_DIM)
        k_nope = kn * r * bkg[:, :LANES]
        k_pe = kpe_rot * r
        kb_ref[:, h * B_QK_PAD:h * B_QK_PAD + LANES] = k_nope.astype(BF16)
        kb_ref[:, h * B_QK_PAD + LANES:(h + 1) * B_QK_PAD] = k_pe.astype(BF16)
        note_key_norm(h, jnp.sum(k_nope * k_nope, axis=-1, keepdims=True)
                      + jnp.sum(k_pe * k_pe, axis=-1, keepdims=True))
        vbt_ref[h, 0, 0:DV] = _dot_t(wuvt_ref[h], kvlat).astype(BF16)
        vbt_ref[h, 0, DV:DV_AUG] = jnp.ones((ONES_ROWS, vbt_ref.shape[-1]), BF16)

    cqg = cqg_ref[...] * C_SCALE
    for h in range(C_HEADS):
        cols = slice(h * C_HEAD_DIM, (h + 1) * C_HEAD_DIM)
        v = cq_ref[:, cols].astype(F32)
        r = _rms_scale(jnp.sum(v * v, axis=-1, keepdims=True), C_HEAD_DIM)
        cqn_ref[:, cols] = (v * r * cqg).astype(BF16)


def _seg_spec(tm, name):
    units = SEG_UNITS[name]
    blk = SEG_START[name] // units
    return pl.BlockSpec((tm, units * LANES), lambda i: (i, blk))


def _full_spec(shape):
    return pl.BlockSpec(shape, lambda i: (0,) * len(shape))


def _prep(proj, cos128, sin128, aqg, akg, bqlg, bkvlg, bqg, bkg, cqg, wuq, wuk, wuvt):
    s = proj.shape[0]
    tm = KV_TILE
    nt = s // tm
    row = lambda w: pl.BlockSpec((tm, w), lambda i: (i, 0))
    tiles_t = lambda n: pl.BlockSpec((n, 1, DV_AUG, tm), lambda i: (0, i, 0, 0))
    small = [aqg, akg, bqlg, bkvlg, bqg, bkg, cqg, wuq, wuk, wuvt]
    out_specs = [row(A_WIDTH), row(A_KV_WIDTH), row(2 * LANES),
                 pl.BlockSpec((LANES, tm), lambda i: (0, i)), tiles_t(A_KV_HEADS),
                 row(B_HEADS * B_QK_PAD), row(B_HEADS * B_QK_PAD), tiles_t(B_HEADS), row(C_WIDTH),
                 _full_spec((KNORM_ROWS, LANES))]
    out_shape = [
        jax.ShapeDtypeStruct((s, A_WIDTH), BF16),
        jax.ShapeDtypeStruct((s, A_KV_WIDTH), BF16),
        jax.ShapeDtypeStruct((s, 2 * LANES), BF16),
        jax.ShapeDtypeStruct((LANES, s), F32),
        jax.ShapeDtypeStruct((A_KV_HEADS, nt, DV_AUG, tm), BF16),
        jax.ShapeDtypeStruct((s, B_HEADS * B_QK_PAD), BF16),
        jax.ShapeDtypeStruct((s, B_HEADS * B_QK_PAD), BF16),
        jax.ShapeDtypeStruct((B_HEADS, nt, DV_AUG, tm), BF16),
        jax.ShapeDtypeStruct((s, C_WIDTH), BF16),
        jax.ShapeDtypeStruct((KNORM_ROWS, LANES), F32),
    ]
    return pl.pallas_call(
        _prep_kernel,
        grid=(nt,),
        in_specs=[_seg_spec(tm, n) for n in ("aq", "ak", "av", "ikw", "bcq", "bckv", "bkpe", "cq")]
        + [row(LANES), row(LANES)] + [_full_spec(a.shape) for a in small],
        out_specs=out_specs,
        out_shape=out_shape,
        compiler_params=_cparams("arbitrary"),
        name="prep",
    )(*([proj] * 8), cos128, sin128, *small)


def _reduce_keys(x, op):
    tk, tq = x.shape
    n = tk // SUBLANES
    assert n & (n - 1) == 0
    t = x.reshape(n, SUBLANES, tq)
    while n > 1:
        n //= 2
        t = op(t[:n], t[n:2 * n])
    red = jnp.max if op is jnp.maximum else jnp.sum
    return red(t[0], axis=0, keepdims=True)


def _softmax_step_t(st, vt, m_ref, acc_ref):
    m_prev = m_ref[...]
    m_new = jnp.maximum(m_prev, _reduce_keys(st, jnp.maximum))
    alpha = jnp.exp2(m_prev - m_new)
    p = jnp.exp2(st - m_new).astype(BF16)
    acc_ref[...] = alpha * acc_ref[...] + jnp.dot(vt, p, preferred_element_type=F32)
    m_ref[...] = m_new


def _fixed_shift_step_t(st, vt, shift, acc_ref):
    p = jnp.exp2(st - shift).astype(BF16)
    acc_ref[...] += jnp.dot(vt, p, preferred_element_type=F32)


def _normalised_output(acc):
    return (acc[0:DV] / acc[DV:DV + 1]).T


def _logit_bound(q, kmax2):
    qf = q.astype(F32)
    ones = jnp.ones((SUBLANES, q.shape[1]), BF16)
    qn2 = _dot_t(ones, (qf * qf).astype(BF16))[0:1]
    return jnp.sqrt(qn2 * kmax2) * SHIFT_MARGIN


def _causal_ok_t(i, j, tq, tk):
    key = j * tk + lax.broadcasted_iota(I32, (tk, tq), 0)
    qry = i * tq + lax.broadcasted_iota(I32, (tk, tq), 1)
    return key <= qry


def _init_softmax_state(m_ref, acc_ref):
    m_ref[...] = jnp.full(m_ref.shape, MASK_VALUE, F32)
    acc_ref[...] = jnp.zeros(acc_ref.shape, F32)


def _pingpong_tiles(n_full, logits, consume, buf_a, buf_b):
    def pair(p, c):
        j = 2 * p
        logits(j + 1, buf_b)
        consume(j, buf_a, False)
        logits(j + 2, buf_a)
        consume(j + 1, buf_b, False)
        return c

    lax.fori_loop(0, n_full // 2, pair, 0)
    odd = n_full % 2 == 1

    @pl.when(odd)
    def _():
        logits(n_full, buf_b)
        consume(n_full - 1, buf_a, False)
        consume(n_full, buf_b, True)

    @pl.when(jnp.logical_not(odd))
    def _():
        consume(n_full, buf_a, True)


def _attn_b_kernel(q_ref, k_ref, vt_ref, kn_ref, o_ref, sa_ref, sb_ref, m_ref, acc_ref, *, tq, tk):
    hp = pl.program_id(0)
    i = pl.program_id(1)
    _init_softmax_state(m_ref, acc_ref)
    j_diag = (i * tq) // tk
    heads = range(B_HEADS_PER_STEP)

    def logits(j, buf):
        rows = pl.ds(pl.multiple_of(j * tk, tk), tk)
        for hh in heads:
            cols = slice(hh * B_QK_PAD, (hh + 1) * B_QK_PAD)
            buf[hh] = _dot_t(k_ref[rows, cols], q_ref[:, cols])

    logits(0, sa_ref)
    shifts = [_logit_bound(q_ref[:, hh * B_QK_PAD:(hh + 1) * B_QK_PAD],
                           kn_ref[pl.ds(hp * B_HEADS_PER_STEP + hh, 1), 0:1]) for hh in heads]
    fixed_ok = jnp.max(2.0 * functools.reduce(jnp.maximum, shifts)) <= MAX_SHIFT_SPAN

    def consume(j, buf, masked, fixed):
        for hh in heads:
            st = buf[hh]
            if masked:
                st = jnp.where(_causal_ok_t(i, j, tq, tk), st, MASK_VALUE)
            if fixed:
                _fixed_shift_step_t(st, vt_ref[hh, j], shifts[hh], acc_ref.at[hh])
            else:
                _softmax_step_t(st, vt_ref[hh, j], m_ref.at[hh], acc_ref.at[hh])

    @pl.when(fixed_ok)
    def _():
        _pingpong_tiles(j_diag, logits, functools.partial(consume, fixed=True), sa_ref, sb_ref)

    @pl.when(jnp.logical_not(fixed_ok))
    def _():
        _pingpong_tiles(j_diag, logits, functools.partial(consume, fixed=False), sa_ref, sb_ref)

    for hh in heads:
        o_ref[:, hh * B_V:(hh + 1) * B_V] = _normalised_output(acc_ref[hh]).astype(o_ref.dtype)


def _attn_b(qb, kb, vbt, knorm2, *, tq):
    s = qb.shape[0]
    tk = KV_TILE
    hps = B_HEADS_PER_STEP
    assert tk % tq == 0 or tq % tk == 0
    assert tq <= tk
    return pl.pallas_call(
        functools.partial(_attn_b_kernel, tq=tq, tk=tk),
        grid=(B_HEADS // hps, s // tq),
        in_specs=[
            pl.BlockSpec((tq, hps * B_QK_PAD), lambda h, i: (i, h)),
            pl.BlockSpec((s, hps * B_QK_PAD), lambda h, i: (0, h)),
            pl.BlockSpec((hps, s // tk, DV_AUG, tk), lambda h, i: (h, 0, 0, 0)),
            pl.BlockSpec((KNORM_ROWS, LANES), lambda h, i: (0, 0)),
        ],
        out_specs=pl.BlockSpec((tq, hps * B_V), lambda h, i: (i, h)),
        out_shape=jax.ShapeDtypeStruct((s, B_WIDTH), BF16),
        scratch_shapes=[pltpu.VMEM((hps, tk, tq), F32), pltpu.VMEM((hps, tk, tq), F32),
                        pltpu.VMEM((hps, 1, tq), F32), pltpu.VMEM((hps, DV_AUG, tq), F32)],
        compiler_params=_cparams("parallel", "arbitrary"),
        name="attn_b",
    )(qb, kb, vbt, knorm2)


KEY_BITS = 32
COARSE_BITS = 16
BISECT_GROUP = 4
BRACKET_GROUP = 2
PACKED_SUBLANES = 16
KEY_NEG_INF = (0xFF800000 ^ 0x7FFFFFFF) - (1 << 32)
BRACKET_HALF = (1 << (KEY_BITS - COARSE_BITS - 1)) + 1
BRACKET_STEPS = (KEY_BITS - COARSE_BITS + 1 + BRACKET_GROUP) // BRACKET_GROUP * BRACKET_GROUP
assert KEY_BITS % BISECT_GROUP == 0
assert (1 << BRACKET_STEPS) > 2 * BRACKET_HALF + 1


def _key_as_f32(key):
    bits = key ^ (lax.shift_right_arithmetic(key, 31) & 0x7FFFFFFF)
    return lax.bitcast_convert_type(bits, F32)


def _coarse_key_as_bf16(k):
    b = k ^ (lax.shift_right_arithmetic(k, COARSE_BITS - 1) & ((1 << (COARSE_BITS - 1)) - 1))
    return lax.bitcast_convert_type(lax.shift_left(b, COARSE_BITS), F32).astype(BF16)


def _count_hits_packed(hit):
    tk, tq = hit.shape
    n = tk // PACKED_SUBLANES
    assert n <= 256
    t = jnp.where(hit, jnp.ones((), BF16), jnp.zeros((), BF16)).reshape(n, PACKED_SUBLANES, tq)
    while n > 1:
        n //= 2
        t = t[:n] + t[n:2 * n]
    return t[0].astype(F32)


def _count_hits(hit):
    tk, tq = hit.shape
    n = tk // SUBLANES
    t = jnp.where(hit, 1, 0).reshape(n, SUBLANES, tq)
    while n > 1:
        n //= 2
        t = t[:n] + t[n:2 * n]
    return t[0]


def _attn_a_kernel(qmin_ref, kmax_ref,
                   iq_ref, wt_ref, kab_ref, aqn_ref, akn_ref, avt_ref, pq_ref, pk_ref, lut_ref, kn_ref,
                   o_ref,
                   score_ref, coarse_ref, thr_ref, settled_ref, sa_ref, sb_ref, pa_ref, pb_ref, m_ref, acc_ref,
                   *, tq, tk, topk):
    i = pl.program_id(0)
    j_diag = (i * tq) // tk
    n_tiles = j_diag + 1
    int_min = jnp.int32(-2 ** 31)

    def head_pair_dots(j, p):
        rows = pl.ds(pl.multiple_of(j * tk, tk), tk)
        rhs = iq_ref[:, p * LANES:(p + 1) * LANES]
        return _dot_t(kab_ref[rows, 0:LANES], rhs), _dot_t(kab_ref[rows, LANES:2 * LANES], rhs)

    def score_tile(j, masked):
        acc = jnp.zeros((tk, tq), F32)
        n_pairs = IDX_HEADS // 2
        for p in range(n_pairs):
            d_lo, d_hi = (sa_ref[...], sb_ref[...]) if p == 0 else head_pair_dots(j, p)
            if p == n_pairs - 1 and not masked:
                sa_ref[...], sb_ref[...] = head_pair_dots(j + 1, 0)
            w_row = IDX_DIM + 2 * p
            acc = acc + jnp.maximum(d_lo, 0.0) * wt_ref[w_row:w_row + 1, :]
            acc = acc + jnp.maximum(d_hi, 0.0) * wt_ref[w_row + 1:w_row + 2, :]
        if masked:
            acc = jnp.where(_causal_ok_t(i, j, tq, tk), acc, -jnp.inf)
        score_ref[j] = acc
        coarse_ref[j] = acc.astype(BF16)

    def score_body(j, c):
        score_tile(j, False)
        return c

    sa_ref[...], sb_ref[...] = head_pair_dots(0, 0)
    lax.fori_loop(0, j_diag, score_body, 0)
    score_tile(j_diag, True)

    def over_tiles(tile_count, init):
        def single(j, cnt):
            return cnt + tile_count(j)

        def double(p, cnt):
            return cnt + (tile_count(2 * p) + tile_count(2 * p + 1))

        cnt = lax.fori_loop(0, n_tiles // 2, double, init)
        cnt = lax.fori_loop(2 * (n_tiles // 2), n_tiles, single, cnt)
        return jnp.sum(cnt, axis=0, keepdims=True)

    def count_queries(pred_fn):
        return over_tiles(lambda j: _count_hits(pred_fn(score_ref[j], j)), jnp.zeros((SUBLANES, tq), I32))

    def count_at_or_above(cand_key):
        cand = _key_as_f32(cand_key)
        cnt = count_queries(lambda x, j: x >= cand)
        return jnp.where(cand_key <= KEY_NEG_INF, n_tiles * tk, cnt)

    def store_threshold(thr_key, settled):
        thr_ref[...] = jnp.where(thr_key <= KEY_NEG_INF, -jnp.inf, _key_as_f32(thr_key))
        settled_ref[...] = settled

    def bisect_all_bits():
        def cond(c):
            g, _, s = c
            return (g < KEY_BITS // BISECT_GROUP) & (jnp.min(s) == 0)

        def body(c):
            g, t, s = c
            for u in range(BISECT_GROUP):
                bit = KEY_BITS - 1 - (g * BISECT_GROUP + u)
                cand = t + lax.shift_left(jnp.int32(1), bit)
                cnt = count_at_or_above(cand)
                t = jnp.where((s == 0) & (cnt >= topk), cand, t)
                s = jnp.where(cnt == topk, 1, s)
            return g + 1, t, s

        _, t, s = lax.while_loop(cond, body, (jnp.int32(0), jnp.full((1, tq), int_min, I32),
                                              jnp.zeros((1, tq), I32)))
        store_threshold(t, s)

    def bisect_bracket(lo, hi, cnt_lo):
        def cond(c):
            g, lo, hi, _, s = c
            done = (s == 1) | (hi - lo <= 1)
            return (g < BRACKET_STEPS // BRACKET_GROUP) & (jnp.min(jnp.where(done, 1, 0)) == 0)

        def body(c):
            g, lo, hi, t, s = c
            for _ in range(BRACKET_GROUP):
                mid = lo + lax.shift_right_arithmetic(hi - lo, 1)
                cnt = count_at_or_above(mid)
                t = jnp.where((s == 0) & (cnt == topk), mid, t)
                s = jnp.where(cnt == topk, 1, s)
                lo, hi = jnp.where(cnt >= topk, mid, lo), jnp.where(cnt >= topk, hi, mid)
            return g + 1, lo, hi, t, s

        settled0 = jnp.where(cnt_lo == topk, 1, 0)
        _, lo, _, t, s = lax.while_loop(cond, body, (jnp.int32(0), lo, hi, lo, settled0))
        store_threshold(jnp.where(s == 1, t, lo), s)

    def count_coarse(cand):
        return over_tiles(lambda j: _count_hits_packed(coarse_ref[j] >= cand),
                          jnp.zeros((PACKED_SUBLANES, tq), F32))

    def coarse_step(it, t):
        cand = t + lax.shift_left(jnp.int32(1), COARSE_BITS - 1 - it)
        return jnp.where(count_coarse(_coarse_key_as_bf16(cand)) >= topk, cand, t)

    t_coarse = lax.fori_loop(0, COARSE_BITS, coarse_step,
                             jnp.full((1, tq), -(1 << (COARSE_BITS - 1)), I32))

    low_bits = KEY_BITS - COARSE_BITS
    centre = lax.shift_left(t_coarse, low_bits) + jnp.where(t_coarse < 0, (1 << low_bits) - 1, 0)
    lo_key, hi_key = centre - BRACKET_HALF, centre + BRACKET_HALF + 1
    cnt_lo = count_at_or_above(lo_key)
    cnt_hi = count_at_or_above(hi_key)
    bracket_ok = (cnt_lo >= topk) & (cnt_hi < topk) & (lo_key < hi_key)
    coarse_ok = jnp.min(jnp.where(bracket_ok, 1, 0)) == 1

    pl.when(coarse_ok)(lambda: bisect_bracket(lo_key, hi_key, cnt_lo))
    pl.when(jnp.logical_not(coarse_ok))(bisect_all_bits)

    thr = thr_ref[...]
    settled = settled_ref[...]

    def drop_excess_ties():
        cnt_gt = count_queries(lambda x, j: x > thr)
        need = topk - cnt_gt
        kpos = lax.broadcasted_iota(I32, (tk, tq), 0)

        def pos_body(it, cut):
            cand = cut + lax.shift_left(jnp.int32(1), 30 - it)
            cnt = count_queries(lambda x, j: (x == thr) & (j * tk + kpos < cand))
            return jnp.where(cnt < need, cand, cut)

        cut = lax.fori_loop(0, 31, pos_body, jnp.zeros((1, tq), I32))

        def drop_body(j, c):
            x = score_ref[j]
            drop = (x == thr) & (j * tk + kpos > cut)
            score_ref[j] = jnp.where(drop, -jnp.inf, x)
            return c

        lax.fori_loop(0, n_tiles, drop_body, 0)

    @pl.when(jnp.min(settled) == 0)
    def _():
        cnt_ge = count_queries(lambda x, j: x >= thr)
        pl.when(jnp.max(cnt_ge) > topk)(drop_excess_ties)

    _init_softmax_state(m_ref, acc_ref)
    pq = pq_ref[...]
    lut = lut_ref[...]
    bias_hi = jnp.max(lut, axis=1, keepdims=True)
    bias_span = bias_hi - jnp.min(lut, axis=1, keepdims=True)
    bounds = [_logit_bound(aqn_ref[:, h * A_HEAD_DIM:(h + 1) * A_HEAD_DIM],
                           kn_ref[B_HEADS + h // A_REP:B_HEADS + h // A_REP + 1, 0:1]) for h in range(A_HEADS)]
    shifts = [bounds[h] + bias_hi[h:h + 1] for h in range(A_HEADS)]
    spans = [2.0 * bounds[h] + bias_span[h:h + 1] for h in range(A_HEADS)]
    fixed_ok = jnp.max(functools.reduce(jnp.maximum, spans)) <= MAX_SHIFT_SPAN

    def head_buf(h):
        return (sa_ref, sb_ref)[h % 2]

    def logits(j, h, buf):
        rows = pl.ds(pl.multiple_of(j * tk, tk), tk)
        g = h // A_REP
        buf[...] = _dot_t(akn_ref[rows, g * A_HEAD_DIM:(g + 1) * A_HEAD_DIM],
                          aqn_ref[:, h * A_HEAD_DIM:(h + 1) * A_HEAD_DIM])

    def attend(j, masked, near, fixed):
        rows = pl.ds(pl.multiple_of(j * tk, tk), tk)
        sel = score_ref[j] >= thr
        if masked:
            sel = sel & _causal_ok_t(i, j, tq, tk)
        mask_bias = jnp.where(sel, 0.0, MASK_VALUE)
        if near:
            dist = jnp.clip(pq - pk_ref[rows, :], 0, LANES - 1)

        logits(j, 0, head_buf(0))
        for h in range(A_HEADS):
            if h + 1 < A_HEADS:
                logits(j, h + 1, head_buf(h + 1))
            st = head_buf(h)[...] + mask_bias
            if near:
                table = jnp.broadcast_to(lut_ref[h:h + 1, :], (tk, LANES))
                st = st + jnp.concatenate(
                    [jnp.take_along_axis(table, dist[:, c * LANES:(c + 1) * LANES], axis=1,
                                         mode="promise_in_bounds")
                     for c in range(tq // LANES)], axis=1)
            if fixed:
                _fixed_shift_step_t(st, avt_ref[h // A_REP, j], shifts[h], acc_ref.at[h])
            else:
                _softmax_step_t(st, avt_ref[h // A_REP, j], m_ref.at[h], acc_ref.at[h])

    def is_far(j):
        return qmin_ref[i] - kmax_ref[j] >= T5_FAR

    def attend_dyn(j, masked, fixed):
        far = is_far(j)

        @pl.when(far)
        def _():
            attend(j, masked, False, fixed)

        @pl.when(jnp.logical_not(far))
        def _():
            attend(j, masked, True, fixed)

    def attend_far_pair(j):
        rows = pl.ds(pl.multiple_of(j * tk, tk), 2 * tk)
        mask_bias = jnp.concatenate([jnp.where(score_ref[j] >= thr, 0.0, MASK_VALUE),
                                     jnp.where(score_ref[j + 1] >= thr, 0.0, MASK_VALUE)], axis=0)
        pair_bufs = (pa_ref, pb_ref)

        def pair_logits(h):
            g = h // A_REP
            pair_bufs[h % 2][...] = _dot_t(akn_ref[rows, g * A_HEAD_DIM:(g + 1) * A_HEAD_DIM],
                                           aqn_ref[:, h * A_HEAD_DIM:(h + 1) * A_HEAD_DIM])

        pair_logits(0)
        for h in range(A_HEADS):
            if h + 1 < A_HEADS:
                pair_logits(h + 1)
            p = jnp.exp2(pair_bufs[h % 2][...] + mask_bias - shifts[h]).astype(BF16)
            g = h // A_REP
            acc_ref[h] += (jnp.dot(avt_ref[g, j], p[:tk], preferred_element_type=F32)
                           + jnp.dot(avt_ref[g, j + 1], p[tk:], preferred_element_type=F32))

    def attend_all(fixed):
        def attend_body(j, c):
            attend_dyn(j, False, fixed)
            return c

        if fixed:
            def pair_body(p, c):
                j = 2 * p
                both_far = is_far(j) & is_far(j + 1)

                @pl.when(both_far)
                def _():
                    attend_far_pair(j)

                @pl.when(jnp.logical_not(both_far))
                def _():
                    lax.fori_loop(j, j + 2, attend_body, 0)

                return c

            n_pairs = j_diag // 2
            lax.fori_loop(0, n_pairs, pair_body, 0)
            lax.fori_loop(2 * n_pairs, j_diag, attend_body, 0)
        else:
            lax.fori_loop(0, j_diag, attend_body, 0)
        attend_dyn(j_diag, True, fixed)

    pl.when(fixed_ok)(functools.partial(attend_all, True))
    pl.when(jnp.logical_not(fixed_ok))(functools.partial(attend_all, False))

    for h in range(A_HEADS):
        q_cols = slice(h * A_HEAD_DIM, (h + 1) * A_HEAD_DIM)
        o_ref[:, q_cols] = _normalised_output(acc_ref[h]).astype(o_ref.dtype)


def _attn_a(proj, wt, kab, aqn, akn, avt, pos_row, pos_col, lut_t, knorm2, qmin, kmax, *, tq, topk):
    s = proj.shape[0]
    tk = KV_TILE
    assert tk % tq == 0
    iq_blk = SEG_START["iq"] // SEG_UNITS["iq"]
    grid_spec = pltpu.PrefetchScalarGridSpec(
        num_scalar_prefetch=2,
        grid=(s // tq,),
        in_specs=[
            pl.BlockSpec((tq, IDX_HEADS * IDX_DIM), lambda i, *_: (i, iq_blk)),
            pl.BlockSpec((LANES, tq), lambda i, *_: (0, i)),
            pl.BlockSpec((s, 2 * LANES), lambda i, *_: (0, 0)),
            pl.BlockSpec((tq, A_WIDTH), lambda i, *_: (i, 0)),
            pl.BlockSpec((s, A_KV_WIDTH), lambda i, *_: (0, 0)),
            pl.BlockSpec((A_KV_HEADS, s // tk, DV_AUG, tk), lambda i, *_: (0, 0, 0, 0)),
            pl.BlockSpec((1, tq), lambda i, *_: (0, i)),
            pl.BlockSpec((s, 1), lambda i, *_: (0, 0)),
            pl.BlockSpec((A_HEADS, LANES), lambda i, *_: (0, 0)),
            pl.BlockSpec((KNORM_ROWS, LANES), lambda i, *_: (0, 0)),
        ],
        out_specs=pl.BlockSpec((tq, A_WIDTH), lambda i, *_: (i, 0)),
        scratch_shapes=[
            pltpu.VMEM((s // tk, tk, tq), F32),
            pltpu.VMEM((s // tk, tk, tq), BF16),
            pltpu.VMEM((1, tq), F32),
            pltpu.VMEM((1, tq), I32),
            pltpu.VMEM((tk, tq), F32),
            pltpu.VMEM((tk, tq), F32),
            pltpu.VMEM((2 * tk, tq), F32),
            pltpu.VMEM((2 * tk, tq), F32),
            pltpu.VMEM((A_HEADS, 1, tq), F32),
            pltpu.VMEM((A_HEADS, DV_AUG, tq), F32),
        ],
    )
    return pl.pallas_call(
        functools.partial(_attn_a_kernel, tq=tq, tk=tk, topk=topk),
        grid_spec=grid_spec,
        out_shape=jax.ShapeDtypeStruct((s, A_WIDTH), BF16),
        compiler_params=_cparams("arbitrary"),
        name="attn_a",
    )(qmin, kmax, proj, wt, kab, aqn, akn, avt, pos_row, pos_col, lut_t, knorm2)


def _attn_c_kernel(q_ref, kv_ref, o_ref):
    for h in range(C_HEADS):
        cols = slice(h * C_HEAD_DIM, (h + 1) * C_HEAD_DIM)
        s = _dot_t(q_ref[:, cols], kv_ref[:, cols])
        p = jnp.exp(s - jnp.max(s, axis=-1, keepdims=True))
        o = jnp.dot(p.astype(BF16), kv_ref[:, C_WIDTH + h * C_HEAD_DIM:C_WIDTH + (h + 1) * C_HEAD_DIM],
                    preferred_element_type=F32)
        o_ref[:, cols] = (o / jnp.sum(p, axis=-1, keepdims=True)).astype(o_ref.dtype)


def _attn_c(cqn, mem_kv, *, tm):
    s = cqn.shape[0]
    return pl.pallas_call(
        _attn_c_kernel,
        grid=(s // tm,),
        in_specs=[pl.BlockSpec((tm, C_WIDTH), lambda i: (i, 0)), _full_spec(mem_kv.shape)],
        out_specs=pl.BlockSpec((tm, C_WIDTH), lambda i: (i, 0)),
        out_shape=jax.ShapeDtypeStruct((s, C_WIDTH), BF16),
        compiler_params=_cparams("parallel"),
        name="attn_c",
    )(cqn, mem_kv)


def _merge_out_kernel(x_ref, oa_ref, ob_ref, oc_ref, az_ref, bz_ref, cz_ref, ga_ref, gb_ref, gc_ref,
                      wbr_ref, wout_ref, o_ref):
    merged = None
    for n, (o_r, z_r, g_r) in enumerate(((oa_ref, az_ref, ga_ref), (ob_ref, bz_ref, gb_ref),
                                         (oc_ref, cz_ref, gc_ref))):
        z = z_r[...].astype(F32)
        u = (o_r[...].astype(F32) * (z * jax.nn.sigmoid(z))).astype(BF16)
        y = jnp.dot(u, wbr_ref[n], preferred_element_type=F32)
        t = jax.nn.sigmoid(g_r[...].astype(F32)) * y
        merged = t if merged is None else merged + t
    o_ref[...] = x_ref[...] + jnp.dot(merged.astype(BF16), wout_ref[...], preferred_element_type=F32)


def _merge_out(x, o_a, o_b, o_c, proj, w_branch, w_out, *, tm):
    s = x.shape[0]
    row = lambda w: pl.BlockSpec((tm, w), lambda i: (i, 0))
    gate_blk = SEG_START["gates"] * LANES // D_MODEL
    gate = lambda n: pl.BlockSpec((tm, D_MODEL), lambda i: (i, gate_blk + n))
    single = pl.Buffered(1)
    return pl.pallas_call(
        _merge_out_kernel,
        grid=(s // tm,),
        in_specs=[row(D_MODEL), row(BRANCH_WIDTH), row(BRANCH_WIDTH), row(BRANCH_WIDTH),
                  _seg_spec(tm, "az"), _seg_spec(tm, "bz"), _seg_spec(tm, "cz"),
                  gate(0), gate(1), gate(2),
                  pl.BlockSpec(w_branch.shape, lambda i: (0, 0, 0), pipeline_mode=single),
                  pl.BlockSpec(w_out.shape, lambda i: (0, 0), pipeline_mode=single)],
        out_specs=row(D_MODEL),
        out_shape=jax.ShapeDtypeStruct((s, D_MODEL), F32),
        compiler_params=_cparams("parallel"),
        name="merge_out",
    )(x, o_a, o_b, o_c, proj, proj, proj, proj, proj, proj, w_branch, w_out)


def _regroup_tables():
    names = ("aq", "ak", "av", "az", "iq", "ik", "iw", "bcq", "bckv", "bkpe", "bz", "cq", "cz", "gates")
    src_off = dict(zip(names, [0] + IN_OFFSETS))
    src_off["ikw"] = src_off["ik"]
    valid_rows = dict(ikw=IDX_DIM + IDX_HEADS, bkpe=B_ROPE)
    start, valid = [], []
    for name, units in SEG_UNITS.items():
        for u in range(units):
            start.append(src_off[name] + u * LANES)
            valid.append(valid_rows.get(name, LANES))
    return np.asarray(start, np.int32), np.asarray(valid, np.int32)


REGROUP_START, REGROUP_VALID = _regroup_tables()
REGROUP_ALIGN = math.gcd(*(int(v) for v in REGROUP_START if v))
assert REGROUP_ALIGN % SUBLANES == 0


REGROUP_UNITS_PER_STEP = 4
assert PROJ_UNITS % REGROUP_UNITS_PER_STEP == 0


def _regroup_kernel(start_ref, valid_ref, *refs):
    w_refs, o_ref = refs[:-1], refs[-1]
    row = lax.broadcasted_iota(I32, (LANES, o_ref.shape[1]), 0)
    for u, w_ref in enumerate(w_refs):
        valid = valid_ref[pl.program_id(0) * REGROUP_UNITS_PER_STEP + u]
        o_ref[u * LANES:(u + 1) * LANES, :] = jnp.where(row < valid, w_ref[...], 0.0).astype(BF16)


def _regroup_w_in_t(w_in_t):
    d = w_in_t.shape[1]
    ups = REGROUP_UNITS_PER_STEP

    def window(u):
        return pl.BlockSpec((pl.Element(LANES), pl.Element(d)),
                            lambda b, start, valid: (start[b * ups + u] * REGROUP_ALIGN, 0))

    grid_spec = pltpu.PrefetchScalarGridSpec(
        num_scalar_prefetch=2,
        grid=(PROJ_UNITS // ups,),
        in_specs=[window(u) for u in range(ups)],
        out_specs=pl.BlockSpec((ups * LANES, d), lambda b, *_: (b, 0)),
    )
    return pl.pallas_call(
        _regroup_kernel,
        grid_spec=grid_spec,
        out_shape=jax.ShapeDtypeStruct((PROJ_WIDTH, d), BF16),
        compiler_params=_cparams("arbitrary"),
        name="regroup_w_in",
    )(jnp.asarray(REGROUP_START // REGROUP_ALIGN), jnp.asarray(REGROUP_VALID), *([w_in_t] * ups))


def _pad_heads(w, n_heads, width, pad_to):
    r = w.shape[0]
    w = w.reshape(r, n_heads, width)
    w = jnp.pad(w, ((0, 0), (0, 0), (0, pad_to - width)))
    return w.reshape(r, n_heads * pad_to)


class _TilePlan(NamedTuple):
    in_proj_rows: int
    in_proj_cols: int
    attn_a_q: int
    attn_b_q: int
    attn_c_rows: int
    merge_rows: int


def _tile_plan(seq):
    plan = _TilePlan(in_proj_rows=min(1024, seq), in_proj_cols=2304, attn_a_q=256, attn_b_q=min(512, seq),
                     attn_c_rows=min(512, seq), merge_rows=min(256, seq))
    assert seq % KV_TILE == 0 and PROJ_WIDTH % plan.in_proj_cols == 0
    assert all(seq % t == 0 for t in (plan.in_proj_rows, plan.attn_a_q, plan.attn_b_q, plan.attn_c_rows,
                                      plan.merge_rows))
    return plan


def kernel(x, mem, positions, rel_bias, norm_g, mem_norm_g, w_in, a_q_norm_g, a_k_norm_g, b_q_lat_norm_g,
           b_kv_lat_norm_g, w_b_uq, w_b_ukv, b_q_norm_g, b_k_norm_g, w_mem_kv, c_q_norm_g, c_k_norm_g,
           w_branch, w_out):
    bsz, seq, d = x.shape
    assert d == D_MODEL and norm_g.shape[0] == 1
    topk = min(TOPK_MAX, seq // 4)
    tiles = _tile_plan(seq)
    tq_a, tq_b, tm_in = tiles.attn_a_q, tiles.attn_b_q, tiles.in_proj_rows

    inv_freq = 1.0 / (ROPE_THETA ** (jnp.arange(0, B_ROPE, 2, dtype=F32) / B_ROPE))
    zeros_half = jnp.zeros((B_ROPE,), F32)
    gq_pad = jnp.concatenate([b_q_norm_g[0], zeros_half])[None, :]
    gk_pad = jnp.concatenate([b_k_norm_g[0], zeros_half])[None, :]
    lut_t = ((rel_bias[jnp.asarray(T5_TABLE)] - rel_bias[REL_BUCKETS - 1]).T * LOG2E).astype(F32)

    w_cat = _regroup_w_in_t(jnp.transpose(w_in[0]))
    wuq = _pad_heads(w_b_uq[0], B_HEADS, B_QK_DIM, B_QK_PAD).astype(BF16)
    wukv = w_b_ukv[0].reshape(B_KV_RANK, B_HEADS, B_NOPE + B_V)
    wuk = wukv[:, :, :B_NOPE].reshape(B_KV_RANK, B_HEADS * B_NOPE).astype(BF16)
    wuvt = jnp.transpose(wukv[:, :, B_NOPE:], (1, 2, 0)).astype(BF16)
    wbr = w_branch[0].astype(BF16)
    wout = w_out[0].astype(BF16)

    outs = []
    for b in range(bsz):
        pos = positions[b]
        ang = pos.astype(F32)[:, None] * inv_freq
        cos, sin = jnp.cos(ang), jnp.sin(ang)
        zpad = jnp.zeros((seq, LANES - B_ROPE), F32)
        cos128 = jnp.concatenate([cos, cos, zpad], axis=1)
        sin128 = jnp.concatenate([-sin, sin, zpad], axis=1)

        mem_kv = _mem_kv(mem[b], mem_norm_g[0][None, :], w_mem_kv[0], c_k_norm_g[0][None, :])
        proj = _in_proj(x[b], norm_g[0][None, :], w_cat, tm=tm_in, tn=tiles.in_proj_cols)
        aqn, akn, kab, wt, avt, qb, kb, vbt, cqn, knorm2 = _prep(
            proj, cos128, sin128, a_q_norm_g[0][None, :], a_k_norm_g[0][None, :],
            b_q_lat_norm_g[0][None, :], b_kv_lat_norm_g[0][None, :], gq_pad, gk_pad,
            c_q_norm_g[0][None, :], wuq, wuk, wuvt)
        o_b = _attn_b(qb, kb, vbt, knorm2, tq=tq_b)
        qmin = jnp.min(pos.reshape(seq // tq_a, tq_a), axis=1)
        kmax = jnp.max(pos.reshape(seq // KV_TILE, KV_TILE), axis=1)
        o_a = _attn_a(proj, wt, kab, aqn, akn, avt, pos[None, :], pos[:, None],
                      lut_t, knorm2, qmin, kmax, tq=tq_a, topk=topk)
        o_c = _attn_c(cqn, mem_kv, tm=tiles.attn_c_rows)
        outs.append(_merge_out(x[b], o_a, o_b, o_c, proj, wbr, wout, tm=tiles.merge_rows))
    return jnp.stack(outs, axis=0)
```

```python
import functools
import math
from typing import NamedTuple

import numpy as np
import jax
import jax.numpy as jnp
from jax import lax
from jax.experimental import pallas as pl
from jax.experimental.pallas import tpu as pltpu

F32 = jnp.float32
BF16 = jnp.bfloat16
I32 = jnp.int32

D_MODEL = 2048
BRANCH_WIDTH = 1024
N_BRANCH = 3
A_HEADS = 8
A_KV_HEADS = 2
A_HEAD_DIM = 128
A_WIDTH = A_HEADS * A_HEAD_DIM
A_KV_WIDTH = A_KV_HEADS * A_HEAD_DIM
A_REP = A_HEADS // A_KV_HEADS
IDX_HEADS = 16
IDX_DIM = 64
TOPK_MAX = 256
IDX_SCALE = (IDX_DIM ** -0.5) * (IDX_HEADS ** -0.5)
A_SCALE = A_HEAD_DIM ** -0.5
B_HEADS = 8
B_Q_RANK = 512
B_KV_RANK = 256
B_NOPE = 128
B_ROPE = 64
B_QK_DIM = B_NOPE + B_ROPE
B_V = 128
B_WIDTH = B_HEADS * B_V
B_SCALE = B_QK_DIM ** -0.5
ROPE_THETA = 10000.0
C_HEADS = 4
C_HEAD_DIM = 256
C_WIDTH = C_HEADS * C_HEAD_DIM
C_SCALE = C_HEAD_DIM ** -0.5
REL_BUCKETS = 32
REL_MAX_DIST = 128
EPS = 1e-6

IN_SIZES = (A_WIDTH, A_KV_WIDTH, A_KV_WIDTH, A_WIDTH, IDX_HEADS * IDX_DIM, IDX_DIM, IDX_HEADS,
            B_Q_RANK, B_KV_RANK, B_ROPE, B_WIDTH, C_WIDTH, C_WIDTH, N_BRANCH * D_MODEL)
IN_OFFSETS = [int(o) for o in np.cumsum(IN_SIZES)[:-1]]

LANES = 128
SUBLANES = 8
B_QK_PAD = 2 * LANES
VMEM_LIMIT_BYTES = 56 * 1024 * 1024
MASK_VALUE = -1e30
LOG2E = math.log2(math.e)
SHIFT_MARGIN = 1.02
MAX_SHIFT_SPAN = 100.0
KNORM_ROWS = 16
DV = 128
ONES_ROWS = 16
DV_AUG = DV + ONES_ROWS
assert DV == A_HEAD_DIM == B_V
KV_TILE = 512
B_HEADS_PER_STEP = 2

SEG_UNITS = dict(gates=48, aq=8, az=8, iq=8, bz=8, cq=8, cz=8, bcq=4, ak=2, av=2, bckv=2, ikw=1, bkpe=1)
SEG_START = {}
_u = 0
for _name, _w in SEG_UNITS.items():
    SEG_START[_name] = _u
    _u += _w
PROJ_UNITS = _u
PROJ_WIDTH = PROJ_UNITS * LANES


def _t5_bucket_table():
    d = np.arange(LANES)
    max_exact = REL_BUCKETS // 2
    nf = np.maximum(d, 1).astype(np.float64)
    large = max_exact + (np.log(nf / max_exact) / math.log(REL_MAX_DIST / max_exact)
                         * (REL_BUCKETS - max_exact)).astype(np.int64)
    large = np.minimum(large, REL_BUCKETS - 1)
    table = np.where(d < max_exact, d, large).astype(np.int32)
    far = int(np.min(np.nonzero(table == REL_BUCKETS - 1)[0]))
    assert np.all(table[far:] == REL_BUCKETS - 1)
    return table, far


T5_TABLE, T5_FAR = _t5_bucket_table()


def _cparams(*sem):
    return pltpu.CompilerParams(dimension_semantics=sem, vmem_limit_bytes=VMEM_LIMIT_BYTES)


def _dot_t(a, b):
    return lax.dot_general(a, b, (((1,), (1,)), ((), ())), preferred_element_type=F32)


def _rms_scale(ss, n):
    return lax.rsqrt(ss * (1.0 / n) + EPS)


def _mem_kv_kernel(mem_ref, g_ref, w_ref, ckn_ref, o_ref, h_ref):
    j = pl.program_id(0)

    @pl.when(j == 0)
    def _():
        m = mem_ref[...]
        r = _rms_scale(jnp.sum(m * m, axis=-1, keepdims=True), D_MODEL)
        h_ref[...] = (m * r * g_ref[...]).astype(BF16)

    y = jnp.dot(h_ref[...], w_ref[...].astype(BF16), preferred_element_type=F32)

    @pl.when(j < C_HEADS)
    def _():
        r = _rms_scale(jnp.sum(y * y, axis=-1, keepdims=True), C_HEAD_DIM)
        o_ref[...] = (y * r * ckn_ref[...]).astype(BF16)

    @pl.when(j >= C_HEADS)
    def _():
        o_ref[...] = y.astype(BF16)


def _mem_kv(mem, mem_g, w_mem_kv, ck_g):
    n_mem = mem.shape[0]
    return pl.pallas_call(
        _mem_kv_kernel,
        grid=(2 * C_HEADS,),
        in_specs=[
            pl.BlockSpec((n_mem, D_MODEL), lambda j: (0, 0)),
            pl.BlockSpec((1, D_MODEL), lambda j: (0, 0)),
            pl.BlockSpec((D_MODEL, C_HEAD_DIM), lambda j: (0, j)),
            pl.BlockSpec((1, C_HEAD_DIM), lambda j: (0, 0)),
        ],
        out_specs=pl.BlockSpec((n_mem, C_HEAD_DIM), lambda j: (0, j)),
        out_shape=jax.ShapeDtypeStruct((n_mem, 2 * C_WIDTH), BF16),
        scratch_shapes=[pltpu.VMEM((n_mem, D_MODEL), BF16)],
        compiler_params=_cparams("arbitrary"),
        name="mem_kv",
    )(mem, mem_g, w_mem_kv, ck_g)


def _in_proj_kernel(x_ref, g_ref, w_ref, o_ref, h_ref, *, row_chunk):
    @pl.when(pl.program_id(1) == 0)
    def _():
        g = g_ref[...]
        for c in range(x_ref.shape[0] // row_chunk):
            rows = pl.ds(c * row_chunk, row_chunk)
            x = x_ref[rows, :]
            r = _rms_scale(jnp.sum(x * x, axis=-1, keepdims=True), D_MODEL)
            h_ref[rows, :] = (x * r * g).astype(BF16)

    o_ref[...] = _dot_t(h_ref[...], w_ref[...]).astype(o_ref.dtype)


def _in_proj(x, norm_g, w_cat, *, tm, tn):
    s = x.shape[0]
    return pl.pallas_call(
        functools.partial(_in_proj_kernel, row_chunk=min(tm, 256)),
        grid=(s // tm, PROJ_WIDTH // tn),
        in_specs=[
            pl.BlockSpec((tm, D_MODEL), lambda i, j: (i, 0)),
            pl.BlockSpec((1, D_MODEL), lambda i, j: (0, 0)),
            pl.BlockSpec((tn, D_MODEL), lambda i, j: (j, 0)),
        ],
        out_specs=pl.BlockSpec((tm, tn), lambda i, j: (i, j)),
        out_shape=jax.ShapeDtypeStruct((s, PROJ_WIDTH), BF16),
        scratch_shapes=[pltpu.VMEM((tm, D_MODEL), BF16)],
        compiler_params=_cparams("parallel", "arbitrary"),
        name="in_proj",
    )(x, norm_g, w_cat)


def _rope128(x, cos, sin):
    half = B_ROPE // 2
    lane = lax.broadcasted_iota(I32, x.shape, 1)
    partner = jnp.where(lane < half, pltpu.roll(x, LANES - half, 1), pltpu.roll(x, half, 1))
    return x * cos + partner * sin


def _prep_kernel(aq_ref, ak_ref, av_ref, ikw_ref, bcq_ref, bckv_ref, bkpe_ref, cq_ref,
                 cos_ref, sin_ref, aqg_ref, akg_ref, bqlg_ref, bkvlg_ref, bqg_ref, bkg_ref, cqg_ref,
                 wuq_ref, wuk_ref, wuvt_ref,
                 aqn_ref, akn_ref, kab_ref, wt_ref, avt_ref, qb_ref, kb_ref, vbt_ref, cqn_ref, kn_ref):
    cos = cos_ref[...]
    sin = sin_ref[...]

    @pl.when(pl.program_id(0) == 0)
    def _():
        kn_ref[...] = jnp.zeros(kn_ref.shape, F32)

    def note_key_norm(row, sq_norms):
        top = jnp.max(sq_norms, axis=0, keepdims=True)
        kn_ref[row:row + 1, :] = jnp.maximum(kn_ref[row:row + 1, :], top)

    eye = jnp.where(lax.broadcasted_iota(I32, (LANES, LANES), 0) == lax.broadcasted_iota(I32, (LANES, LANES), 1),
                    1.0, 0.0).astype(BF16)

    aqg = aqg_ref[...] * (A_SCALE * LOG2E)
    for h in range(A_HEADS):
        cols = slice(h * A_HEAD_DIM, (h + 1) * A_HEAD_DIM)
        v = aq_ref[:, cols].astype(F32)
        r = _rms_scale(jnp.sum(v * v, axis=-1, keepdims=True), A_HEAD_DIM)
        aqn_ref[:, cols] = (v * r * aqg).astype(BF16)
    akg = akg_ref[...]
    for h in range(A_KV_HEADS):
        cols = slice(h * A_HEAD_DIM, (h + 1) * A_HEAD_DIM)
        v = ak_ref[:, cols].astype(F32)
        r = _rms_scale(jnp.sum(v * v, axis=-1, keepdims=True), A_HEAD_DIM)
        kn = v * r * akg
        akn_ref[:, cols] = kn.astype(BF16)
        note_key_norm(B_HEADS + h, jnp.sum(kn * kn, axis=-1, keepdims=True))
        avt_ref[h, 0, 0:DV] = _dot_t(eye, av_ref[:, cols]).astype(BF16)
        avt_ref[h, 0, DV:DV_AUG] = jnp.ones((ONES_ROWS, avt_ref.shape[-1]), BF16)

    ikw = ikw_ref[...]
    lane = lax.broadcasted_iota(I32, ikw.shape, 1)
    zero = jnp.zeros_like(ikw)
    kab_ref[:, 0:LANES] = jnp.where(lane < IDX_DIM, ikw, zero)
    kab_ref[:, LANES:2 * LANES] = jnp.where(lane >= IDX_DIM, pltpu.roll(ikw.astype(F32), IDX_DIM, 1).astype(BF16), zero)
    wt_ref[...] = _dot_t(eye, ikw) * IDX_SCALE

    cq_lat = bcq_ref[...].astype(F32)
    r = _rms_scale(jnp.sum(cq_lat * cq_lat, axis=-1, keepdims=True), B_Q_RANK)
    qlat = (cq_lat * r * bqlg_ref[...]).astype(BF16)
    bqg = bqg_ref[...] * (B_SCALE * LOG2E)
    for h in range(B_HEADS):
        cols = slice(h * B_QK_PAD, (h + 1) * B_QK_PAD)
        qh = jnp.dot(qlat, wuq_ref[:, cols], preferred_element_type=F32)
        r = _rms_scale(jnp.sum(qh * qh, axis=-1, keepdims=True), B_QK_DIM)
        qn = qh * r * bqg
        qb_ref[:, h * B_QK_PAD:h * B_QK_PAD + LANES] = qn[:, :LANES].astype(BF16)
        qb_ref[:, h * B_QK_PAD + LANES:(h + 1) * B_QK_PAD] = _rope128(qn[:, LANES:], cos, sin).astype(BF16)

    ckv = bckv_ref[...].astype(F32)
    r = _rms_scale(jnp.sum(ckv * ckv, axis=-1, keepdims=True), B_KV_RANK)
    kvlat = (ckv * r * bkvlg_ref[...]).astype(BF16)
    kpe = bkpe_ref[...].astype(F32)
    ss_pe = jnp.sum(kpe * kpe, axis=-1, keepdims=True)
    bkg = bkg_ref[...]
    kpe_rot = _rope128(kpe * bkg[:, LANES:], cos, sin)
    for h in range(B_HEADS):
        kn = jnp.dot(kvlat, wuk_ref[:, h * B_NOPE:(h + 1) * B_NOPE], preferred_element_type=F32)
        r = _rms_scale(jnp.sum(kn * kn, axis=-1, keepdims=True) + ss_pe, B_QK_DIM)
        k_nope = kn * r * bkg[:, :LANES]
        k_pe = kpe_rot * r
        kb_ref[:, h * B_QK_PAD:h * B_QK_PAD + LANES] = k_nope.astype(BF16)
        kb_ref[:, h * B_QK_PAD + LANES:(h + 1) * B_QK_PAD] = k_pe.astype(BF16)
        note_key_norm(h, jnp.sum(k_nope * k_nope, axis=-1, keepdims=True)
                      + jnp.sum(k_pe * k_pe, axis=-1, keepdims=True))
        vbt_ref[h, 0, 0:DV] = _dot_t(wuvt_ref[h], kvlat).astype(BF16)
        vbt_ref[h, 0, DV:DV_AUG] = jnp.ones((ONES_ROWS, vbt_ref.shape[-1]), BF16)

    cqg = cqg_ref[...] * C_SCALE
    for h in range(C_HEADS):
        cols = slice(h * C_HEAD_DIM, (h + 1) * C_HEAD_DIM)
        v = cq_ref[:, cols].astype(F32)
        r = _rms_scale(jnp.sum(v * v, axis=-1, keepdims=True), C_HEAD_DIM)
        cqn_ref[:, cols] = (v * r * cqg).astype(BF16)


def _seg_spec(tm, name):
    units = SEG_UNITS[name]
    blk = SEG_START[name] // units
    return pl.BlockSpec((tm, units * LANES), lambda i: (i, blk))


def _full_spec(shape):
    return pl.BlockSpec(shape, lambda i: (0,) * len(shape))


def _prep(proj, cos128, sin128, aqg, akg, bqlg, bkvlg, bqg, bkg, cqg, wuq, wuk, wuvt):
    s = proj.shape[0]
    tm = KV_TILE
    nt = s // tm
    row = lambda w: pl.BlockSpec((tm, w), lambda i: (i, 0))
    tiles_t = lambda n: pl.BlockSpec((n, 1, DV_AUG, tm), lambda i: (0, i, 0, 0))
    small = [aqg, akg, bqlg, bkvlg, bqg, bkg, cqg, wuq, wuk, wuvt]
    out_specs = [row(A_WIDTH), row(A_KV_WIDTH), row(2 * LANES),
                 pl.BlockSpec((LANES, tm), lambda i: (0, i)), tiles_t(A_KV_HEADS),
                 row(B_HEADS * B_QK_PAD), row(B_HEADS * B_QK_PAD), tiles_t(B_HEADS), row(C_WIDTH),
                 _full_spec((KNORM_ROWS, LANES))]
    out_shape = [
        jax.ShapeDtypeStruct((s, A_WIDTH), BF16),
        jax.ShapeDtypeStruct((s, A_KV_WIDTH), BF16),
        jax.ShapeDtypeStruct((s, 2 * LANES), BF16),
        jax.ShapeDtypeStruct((LANES, s), F32),
        jax.ShapeDtypeStruct((A_KV_HEADS, nt, DV_AUG, tm), BF16),
        jax.ShapeDtypeStruct((s, B_HEADS * B_QK_PAD), BF16),
        jax.ShapeDtypeStruct((s, B_HEADS * B_QK_PAD), BF16),
        jax.ShapeDtypeStruct((B_HEADS, nt, DV_AUG, tm), BF16),
        jax.ShapeDtypeStruct((s, C_WIDTH), BF16),
        jax.ShapeDtypeStruct((KNORM_ROWS, LANES), F32),
    ]
    return pl.pallas_call(
        _prep_kernel,
        grid=(nt,),
        in_specs=[_seg_spec(tm, n) for n in ("aq", "ak", "av", "ikw", "bcq", "bckv", "bkpe", "cq")]
        + [row(LANES), row(LANES)] + [_full_spec(a.shape) for a in small],
        out_specs=out_specs,
        out_shape=out_shape,
        compiler_params=_cparams("arbitrary"),
        name="prep",
    )(*([proj] * 8), cos128, sin128, *small)


def _reduce_keys(x, op):
    tk, tq = x.shape
    n = tk // SUBLANES
    assert n & (n - 1) == 0
    t = x.reshape(n, SUBLANES, tq)
    while n > 1:
        n //= 2
        t = op(t[:n], t[n:2 * n])
    red = jnp.max if op is jnp.maximum else jnp.sum
    return red(t[0], axis=0, keepdims=True)


def _softmax_step_t(st, vt, m_ref, acc_ref):
    m_prev = m_ref[...]
    m_new = jnp.maximum(m_prev, _reduce_keys(st, jnp.maximum))
    alpha = jnp.exp2(m_prev - m_new)
    p = jnp.exp2(st - m_new).astype(BF16)
    acc_ref[...] = alpha * acc_ref[...] + jnp.dot(vt, p, preferred_element_type=F32)
    m_ref[...] = m_new


def _fixed_shift_step_t(st, vt, shift, acc_ref):
    p = jnp.exp2(st - shift).astype(BF16)
    acc_ref[...] += jnp.dot(vt, p, preferred_element_type=F32)


def _normalised_output(acc):
    return (acc[0:DV] / acc[DV:DV + 1]).T


def _logit_bound(q, kmax2):
    qf = q.astype(F32)
    ones = jnp.ones((SUBLANES, q.shape[1]), BF16)
    qn2 = _dot_t(ones, (qf * qf).astype(BF16))[0:1]
    return jnp.sqrt(qn2 * kmax2) * SHIFT_MARGIN


def _causal_ok_t(i, j, tq, tk):
    key = j * tk + lax.broadcasted_iota(I32, (tk, tq), 0)
    qry = i * tq + lax.broadcasted_iota(I32, (tk, tq), 1)
    return key <= qry


def _init_softmax_state(m_ref, acc_ref):
    m_ref[...] = jnp.full(m_ref.shape, MASK_VALUE, F32)
    acc_ref[...] = jnp.zeros(acc_ref.shape, F32)


def _pingpong_tiles(n_full, logits, consume, buf_a, buf_b):
    def pair(p, c):
        j = 2 * p
        logits(j + 1, buf_b)
        consume(j, buf_a, False)
        logits(j + 2, buf_a)
        consume(j + 1, buf_b, False)
        return c

    lax.fori_loop(0, n_full // 2, pair, 0)
    odd = n_full % 2 == 1

    @pl.when(odd)
    def _():
        logits(n_full, buf_b)
        consume(n_full - 1, buf_a, False)
        consume(n_full, buf_b, True)

    @pl.when(jnp.logical_not(odd))
    def _():
        consume(n_full, buf_a, True)


def _attn_b_kernel(q_ref, k_ref, vt_ref, kn_ref, o_ref, sa_ref, sb_ref, m_ref, acc_ref, *, tq, tk):
    hp = pl.program_id(0)
    i = pl.program_id(1)
    _init_softmax_state(m_ref, acc_ref)
    j_diag = (i * tq) // tk
    heads = range(B_HEADS_PER_STEP)

    def logits(j, buf):
        rows = pl.ds(pl.multiple_of(j * tk, tk), tk)
        for hh in heads:
            cols = slice(hh * B_QK_PAD, (hh + 1) * B_QK_PAD)
            buf[hh] = _dot_t(k_ref[rows, cols], q_ref[:, cols])

    logits(0, sa_ref)
    shifts = [_logit_bound(q_ref[:, hh * B_QK_PAD:(hh + 1) * B_QK_PAD],
                           kn_ref[pl.ds(hp * B_HEADS_PER_STEP + hh, 1), 0:1]) for hh in heads]
    fixed_ok = jnp.max(2.0 * functools.reduce(jnp.maximum, shifts)) <= MAX_SHIFT_SPAN

    def consume(j, buf, masked, fixed):
        for hh in heads:
            st = buf[hh]
            if masked:
                st = jnp.where(_causal_ok_t(i, j, tq, tk), st, MASK_VALUE)
            if fixed:
                _fixed_shift_step_t(st, vt_ref[hh, j], shifts[hh], acc_ref.at[hh])
            else:
                _softmax_step_t(st, vt_ref[hh, j], m_ref.at[hh], acc_ref.at[hh])

    @pl.when(fixed_ok)
    def _():
        _pingpong_tiles(j_diag, logits, functools.partial(consume, fixed=True), sa_ref, sb_ref)

    @pl.when(jnp.logical_not(fixed_ok))
    def _():
        _pingpong_tiles(j_diag, logits, functools.partial(consume, fixed=False), sa_ref, sb_ref)

    for hh in heads:
        o_ref[:, hh * B_V:(hh + 1) * B_V] = _normalised_output(acc_ref[hh]).astype(o_ref.dtype)


def _attn_b(qb, kb, vbt, knorm2, *, tq):
    s = qb.shape[0]
    tk = KV_TILE
    hps = B_HEADS_PER_STEP
    assert tk % tq == 0 or tq % tk == 0
    assert tq <= tk
    return pl.pallas_call(
        functools.partial(_attn_b_kernel, tq=tq, tk=tk),
        grid=(B_HEADS // hps, s // tq),
        in_specs=[
            pl.BlockSpec((tq, hps * B_QK_PAD), lambda h, i: (i, h)),
            pl.BlockSpec((s, hps * B_QK_PAD), lambda h, i: (0, h)),
            pl.BlockSpec((hps, s // tk, DV_AUG, tk), lambda h, i: (h, 0, 0, 0)),
            pl.BlockSpec((KNORM_ROWS, LANES), lambda h, i: (0, 0)),
        ],
        out_specs=pl.BlockSpec((tq, hps * B_V), lambda h, i: (i, h)),
        out_shape=jax.ShapeDtypeStruct((s, B_WIDTH), BF16),
        scratch_shapes=[pltpu.VMEM((hps, tk, tq), F32), pltpu.VMEM((hps, tk, tq), F32),
                        pltpu.VMEM((hps, 1, tq), F32), pltpu.VMEM((hps, DV_AUG, tq), F32)],
        compiler_params=_cparams("parallel", "arbitrary"),
        name="attn_b",
    )(qb, kb, vbt, knorm2)


KEY_BITS = 32
COARSE_BITS = 16
BISECT_GROUP = 4
BRACKET_GROUP = 2
PACKED_SUBLANES = 16
KEY_NEG_INF = (0xFF800000 ^ 0x7FFFFFFF) - (1 << 32)
BRACKET_HALF = (1 << (KEY_BITS - COARSE_BITS - 1)) + 1
BRACKET_STEPS = (KEY_BITS - COARSE_BITS + 1 + BRACKET_GROUP) // BRACKET_GROUP * BRACKET_GROUP
assert KEY_BITS % BISECT_GROUP == 0
assert (1 << BRACKET_STEPS) > 2 * BRACKET_HALF + 1


def _key_as_f32(key):
    bits = key ^ (lax.shift_right_arithmetic(key, 31) & 0x7FFFFFFF)
    return lax.bitcast_convert_type(bits, F32)


def _coarse_key_as_bf16(k):
    b = k ^ (lax.shift_right_arithmetic(k, COARSE_BITS - 1) & ((1 << (COARSE_BITS - 1)) - 1))
    return lax.bitcast_convert_type(lax.shift_left(b, COARSE_BITS), F32).astype(BF16)


def _count_hits_packed(hit):
    tk, tq = hit.shape
    n = tk // PACKED_SUBLANES
    assert n <= 256
    t = jnp.where(hit, jnp.ones((), BF16), jnp.zeros((), BF16)).reshape(n, PACKED_SUBLANES, tq)
    while n > 1:
        n //= 2
        t = t[:n] + t[n:2 * n]
    return t[0].astype(F32)


def _count_hits(hit):
    tk, tq = hit.shape
    n = tk // SUBLANES
    t = jnp.where(hit, 1, 0).reshape(n, SUBLANES, tq)
    while n > 1:
        n //= 2
        t = t[:n] + t[n:2 * n]
    return t[0]


def _attn_a_kernel(qmin_ref, kmax_ref,
                   iq_ref, wt_ref, kab_ref, aqn_ref, akn_ref, avt_ref, pq_ref, pk_ref, lut_ref, kn_ref,
                   o_ref,
                   score_ref, coarse_ref, thr_ref, settled_ref, sa_ref, sb_ref, pa_ref, pb_ref, m_ref, acc_ref,
                   *, tq, tk, topk):
    i = pl.program_id(0)
    j_diag = (i * tq) // tk
    n_tiles = j_diag + 1
    int_min = jnp.int32(-2 ** 31)

    def head_pair_dots(j, p):
        rows = pl.ds(pl.multiple_of(j * tk, tk), tk)
        rhs = iq_ref[:, p * LANES:(p + 1) * LANES]
        return _dot_t(kab_ref[rows, 0:LANES], rhs), _dot_t(kab_ref[rows, LANES:2 * LANES], rhs)

    def score_tile(j, masked):
        acc = jnp.zeros((tk, tq), F32)
        n_pairs = IDX_HEADS // 2
        for p in range(n_pairs):
            d_lo, d_hi = (sa_ref[...], sb_ref[...]) if p == 0 else head_pair_dots(j, p)
            if p == n_pairs - 1 and not masked:
                sa_ref[...], sb_ref[...] = head_pair_dots(j + 1, 0)
            w_row = IDX_DIM + 2 * p
            acc = acc + jnp.maximum(d_lo, 0.0) * wt_ref[w_row:w_row + 1, :]
            acc = acc + jnp.maximum(d_hi, 0.0) * wt_ref[w_row + 1:w_row + 2, :]
        if masked:
            acc = jnp.where(_causal_ok_t(i, j, tq, tk), acc, -jnp.inf)
        score_ref[j] = acc
        coarse_ref[j] = acc.astype(BF16)

    def score_body(j, c):
        score_tile(j, False)
        return c

    sa_ref[...], sb_ref[...] = head_pair_dots(0, 0)
    lax.fori_loop(0, j_diag, score_body, 0)
    score_tile(j_diag, True)

    def over_tiles(tile_count, init):
        def single(j, cnt):
            return cnt + tile_count(j)

        def double(p, cnt):
            return cnt + (tile_count(2 * p) + tile_count(2 * p + 1))

        cnt = lax.fori_loop(0, n_tiles // 2, double, init)
        cnt = lax.fori_loop(2 * (n_tiles // 2), n_tiles, single, cnt)
        return jnp.sum(cnt, axis=0, keepdims=True)

    def count_queries(pred_fn):
        return over_tiles(lambda j: _count_hits(pred_fn(score_ref[j], j)), jnp.zeros((SUBLANES, tq), I32))

    def count_at_or_above(cand_key):
        cand = _key_as_f32(cand_key)
        cnt = count_queries(lambda x, j: x >= cand)
        return jnp.where(cand_key <= KEY_NEG_INF, n_tiles * tk, cnt)

    def store_threshold(thr_key, settled):
        thr_ref[...] = jnp.where(thr_key <= KEY_NEG_INF, -jnp.inf, _key_as_f32(thr_key))
        settled_ref[...] = settled

    def bisect_all_bits():
        def cond(c):
            g, _, s = c
            return (g < KEY_BITS // BISECT_GROUP) & (jnp.min(s) == 0)

        def body(c):
            g, t, s = c
            for u in range(BISECT_GROUP):
                bit = KEY_BITS - 1 - (g * BISECT_GROUP + u)
                cand = t + lax.shift_left(jnp.int32(1), bit)
                cnt = count_at_or_above(cand)
                t = jnp.where((s == 0) & (cnt >= topk), cand, t)
                s = jnp.where(cnt == topk, 1, s)
            return g + 1, t, s

        _, t, s = lax.while_loop(cond, body, (jnp.int32(0), jnp.full((1, tq), int_min, I32),
                                              jnp.zeros((1, tq), I32)))
        store_threshold(t, s)

    def bisect_bracket(lo, hi, cnt_lo):
        def cond(c):
            g, lo, hi, _, s = c
            done = (s == 1) | (hi - lo <= 1)
            return (g < BRACKET_STEPS // BRACKET_GROUP) & (jnp.min(jnp.where(done, 1, 0)) == 0)

        def body(c):
            g, lo, hi, t, s = c
            for _ in range(BRACKET_GROUP):
                mid = lo + lax.shift_right_arithmetic(hi - lo, 1)
                cnt = count_at_or_above(mid)
                t = jnp.where((s == 0) & (cnt == topk), mid, t)
                s = jnp.where(cnt == topk, 1, s)
                lo, hi = jnp.where(cnt >= topk, mid, lo), jnp.where(cnt >= topk, hi, mid)
            return g + 1, lo, hi, t, s

        settled0 = jnp.where(cnt_lo == topk, 1, 0)
        _, lo, _, t, s = lax.while_loop(cond, body, (jnp.int32(0), lo, hi, lo, settled0))
        store_threshold(jnp.where(s == 1, t, lo), s)

    def count_coarse(cand):
        return over_tiles(lambda j: _count_hits_packed(coarse_ref[j] >= cand),
                          jnp.zeros((PACKED_SUBLANES, tq), F32))

    def coarse_step(it, t):
        cand = t + lax.shift_left(jnp.int32(1), COARSE_BITS - 1 - it)
        return jnp.where(count_coarse(_coarse_key_as_bf16(cand)) >= topk, cand, t)

    t_coarse = lax.fori_loop(0, COARSE_BITS, coarse_step,
                             jnp.full((1, tq), -(1 << (COARSE_BITS - 1)), I32))

    low_bits = KEY_BITS - COARSE_BITS
    centre = lax.shift_left(t_coarse, low_bits) + jnp.where(t_coarse < 0, (1 << low_bits) - 1, 0)
    lo_key, hi_key = centre - BRACKET_HALF, centre + BRACKET_HALF + 1
    cnt_lo = count_at_or_above(lo_key)
    cnt_hi = count_at_or_above(hi_key)
    bracket_ok = (cnt_lo >= topk) & (cnt_hi < topk) & (lo_key < hi_key)
    coarse_ok = jnp.min(jnp.where(bracket_ok, 1, 0)) == 1

    pl.when(coarse_ok)(lambda: bisect_bracket(lo_key, hi_key, cnt_lo))
    pl.when(jnp.logical_not(coarse_ok))(bisect_all_bits)

    thr = thr_ref[...]
    settled = settled_ref[...]

    def drop_excess_ties():
        cnt_gt = count_queries(lambda x, j: x > thr)
        need = topk - cnt_gt
        kpos = lax.broadcasted_iota(I32, (tk, tq), 0)

        def pos_body(it, cut):
            cand = cut + lax.shift_left(jnp.int32(1), 30 - it)
            cnt = count_queries(lambda x, j: (x == thr) & (j * tk + kpos < cand))
            return jnp.where(cnt < need, cand, cut)

        cut = lax.fori_loop(0, 31, pos_body, jnp.zeros((1, tq), I32))

        def drop_body(j, c):
            x = score_ref[j]
            drop = (x == thr) & (j * tk + kpos > cut)
            score_ref[j] = jnp.where(drop, -jnp.inf, x)
            return c

        lax.fori_loop(0, n_tiles, drop_body, 0)

    @pl.when(jnp.min(settled) == 0)
    def _():
        cnt_ge = count_queries(lambda x, j: x >= thr)
        pl.when(jnp.max(cnt_ge) > topk)(drop_excess_ties)

    _init_softmax_state(m_ref, acc_ref)
    pq = pq_ref[...]
    lut = lut_ref[...]
    bias_hi = jnp.max(lut, axis=1, keepdims=True)
    bias_span = bias_hi - jnp.min(lut, axis=1, keepdims=True)
    bounds = [_logit_bound(aqn_ref[:, h * A_HEAD_DIM:(h + 1) * A_HEAD_DIM],
                           kn_ref[B_HEADS + h // A_REP:B_HEADS + h // A_REP + 1, 0:1]) for h in range(A_HEADS)]
    shifts = [bounds[h] + bias_hi[h:h + 1] for h in range(A_HEADS)]
    spans = [2.0 * bounds[h] + bias_span[h:h + 1] for h in range(A_HEADS)]
    fixed_ok = jnp.max(functools.reduce(jnp.maximum, spans)) <= MAX_SHIFT_SPAN

    def head_buf(h):
        return (sa_ref, sb_ref)[h % 2]

    def logits(j, h, buf):
        rows = pl.ds(pl.multiple_of(j * tk, tk), tk)
        g = h // A_REP
        buf[...] = _dot_t(akn_ref[rows, g * A_HEAD_DIM:(g + 1) * A_HEAD_DIM],
                          aqn_ref[:, h * A_HEAD_DIM:(h + 1) * A_HEAD_DIM])

    def attend(j, masked, near, fixed):
        rows = pl.ds(pl.multiple_of(j * tk, tk), tk)
        sel = score_ref[j] >= thr
        if masked:
            sel = sel & _causal_ok_t(i, j, tq, tk)
        mask_bias = jnp.where(sel, 0.0, MASK_VALUE)
        if near:
            dist = jnp.clip(pq - pk_ref[rows, :], 0, LANES - 1)

        logits(j, 0, head_buf(0))
        for h in range(A_HEADS):
            if h + 1 < A_HEADS:
                logits(j, h + 1, head_buf(h + 1))
            st = head_buf(h)[...] + mask_bias
            if near:
                table = jnp.broadcast_to(lut_ref[h:h + 1, :], (tk, LANES))
                st = st + jnp.concatenate(
                    [jnp.take_along_axis(table, dist[:, c * LANES:(c + 1) * LANES], axis=1,
                                         mode="promise_in_bounds")
                     for c in range(tq // LANES)], axis=1)
            if fixed:
                _fixed_shift_step_t(st, avt_ref[h // A_REP, j], shifts[h], acc_ref.at[h])
            else:
                _softmax_step_t(st, avt_ref[h // A_REP, j], m_ref.at[h], acc_ref.at[h])

    def is_far(j):
        return qmin_ref[i] - kmax_ref[j] >= T5_FAR

    def attend_dyn(j, masked, fixed):
        far = is_far(j)

        @pl.when(far)
        def _():
            attend(j, masked, False, fixed)

        @pl.when(jnp.logical_not(far))
        def _():
            attend(j, masked, True, fixed)

    def attend_far_pair(j):
        rows = pl.ds(pl.multiple_of(j * tk, tk), 2 * tk)
        mask_bias = jnp.concatenate([jnp.where(score_ref[j] >= thr, 0.0, MASK_VALUE),
                                     jnp.where(score_ref[j + 1] >= thr, 0.0, MASK_VALUE)], axis=0)
        pair_bufs = (pa_ref, pb_ref)

        def pair_logits(h):
            g = h // A_REP
            pair_bufs[h % 2][...] = _dot_t(akn_ref[rows, g * A_HEAD_DIM:(g + 1) * A_HEAD_DIM],
                                           aqn_ref[:, h * A_HEAD_DIM:(h + 1) * A_HEAD_DIM])

        pair_logits(0)
        for h in range(A_HEADS):
            if h + 1 < A_HEADS:
                pair_logits(h + 1)
            p = jnp.exp2(pair_bufs[h % 2][...] + mask_bias - shifts[h]).astype(BF16)
            g = h // A_REP
            acc_ref[h] += (jnp.dot(avt_ref[g, j], p[:tk], preferred_element_type=F32)
                           + jnp.dot(avt_ref[g, j + 1], p[tk:], preferred_element_type=F32))

    def attend_all(fixed):
        def attend_body(j, c):
            attend_dyn(j, False, fixed)
            return c

        if fixed:
            def pair_body(p, c):
                j = 2 * p
                both_far = is_far(j) & is_far(j + 1)

                @pl.when(both_far)
                def _():
                    attend_far_pair(j)

                @pl.when(jnp.logical_not(both_far))
                def _():
                    lax.fori_loop(j, j + 2, attend_body, 0)

                return c

            n_pairs = j_diag // 2
            lax.fori_loop(0, n_pairs, pair_body, 0)
            lax.fori_loop(2 * n_pairs, j_diag, attend_body, 0)
        else:
            lax.fori_loop(0, j_diag, attend_body, 0)
        attend_dyn(j_diag, True, fixed)

    pl.when(fixed_ok)(functools.partial(attend_all, True))
    pl.when(jnp.logical_not(fixed_ok))(functools.partial(attend_all, False))

    for h in range(A_HEADS):
        q_cols = slice(h * A_HEAD_DIM, (h + 1) * A_HEAD_DIM)
        o_ref[:, q_cols] = _normalised_output(acc_ref[h]).astype(o_ref.dtype)


def _attn_a(proj, wt, kab, aqn, akn, avt, pos_row, pos_col, lut_t, knorm2, qmin, kmax, *, tq, topk):
    s = proj.shape[0]
    tk = KV_TILE
    assert tk % tq == 0
    iq_blk = SEG_START["iq"] // SEG_UNITS["iq"]
    grid_spec = pltpu.PrefetchScalarGridSpec(
        num_scalar_prefetch=2,
        grid=(s // tq,),
        in_specs=[
            pl.BlockSpec((tq, IDX_HEADS * IDX_DIM), lambda i, *_: (i, iq_blk)),
            pl.BlockSpec((LANES, tq), lambda i, *_: (0, i)),
            pl.BlockSpec((s, 2 * LANES), lambda i, *_: (0, 0)),
            pl.BlockSpec((tq, A_WIDTH), lambda i, *_: (i, 0)),
            pl.BlockSpec((s, A_KV_WIDTH), lambda i, *_: (0, 0)),
            pl.BlockSpec((A_KV_HEADS, s // tk, DV_AUG, tk), lambda i, *_: (0, 0, 0, 0)),
            pl.BlockSpec((1, tq), lambda i, *_: (0, i)),
            pl.BlockSpec((s, 1), lambda i, *_: (0, 0)),
            pl.BlockSpec((A_HEADS, LANES), lambda i, *_: (0, 0)),
            pl.BlockSpec((KNORM_ROWS, LANES), lambda i, *_: (0, 0)),
        ],
        out_specs=pl.BlockSpec((tq, A_WIDTH), lambda i, *_: (i, 0)),
        scratch_shapes=[
            pltpu.VMEM((s // tk, tk, tq), F32),
            pltpu.VMEM((s // tk, tk, tq), BF16),
            pltpu.VMEM((1, tq), F32),
            pltpu.VMEM((1, tq), I32),
            pltpu.VMEM((tk, tq), F32),
            pltpu.VMEM((tk, tq), F32),
            pltpu.VMEM((2 * tk, tq), F32),
            pltpu.VMEM((2 * tk, tq), F32),
            pltpu.VMEM((A_HEADS, 1, tq), F32),
            pltpu.VMEM((A_HEADS, DV_AUG, tq), F32),
        ],
    )
    return pl.pallas_call(
        functools.partial(_attn_a_kernel, tq=tq, tk=tk, topk=topk),
        grid_spec=grid_spec,
        out_shape=jax.ShapeDtypeStruct((s, A_WIDTH), BF16),
        compiler_params=_cparams("arbitrary"),
        name="attn_a",
    )(qmin, kmax, proj, wt, kab, aqn, akn, avt, pos_row, pos_col, lut_t, knorm2)


def _memory_attention(q_ref, kv_ref):
    outs = []
    for h in range(C_HEADS):
        cols = slice(h * C_HEAD_DIM, (h + 1) * C_HEAD_DIM)
        s = _dot_t(q_ref[:, cols], kv_ref[:, cols])
        p = jnp.exp(s - jnp.max(s, axis=-1, keepdims=True))
        o = jnp.dot(p.astype(BF16), kv_ref[:, C_WIDTH + h * C_HEAD_DIM:C_WIDTH + (h + 1) * C_HEAD_DIM],
                    preferred_element_type=F32)
        outs.append(o / jnp.sum(p, axis=-1, keepdims=True))
    return jnp.concatenate(outs, axis=1)


def _merge_out_kernel(x_ref, oa_ref, ob_ref, cq_ref, mkv_ref, az_ref, bz_ref, cz_ref, ga_ref, gb_ref, gc_ref,
                      wbr_ref, wout_ref, o_ref):
    branch_out = (lambda: oa_ref[...].astype(F32), lambda: ob_ref[...].astype(F32),
                  lambda: _memory_attention(cq_ref, mkv_ref))
    merged = None
    for n, (out_fn, z_r, g_r) in enumerate(zip(branch_out, (az_ref, bz_ref, cz_ref), (ga_ref, gb_ref, gc_ref))):
        z = z_r[...].astype(F32)
        u = (out_fn() * (z * jax.nn.sigmoid(z))).astype(BF16)
        y = jnp.dot(u, wbr_ref[n], preferred_element_type=F32)
        t = jax.nn.sigmoid(g_r[...].astype(F32)) * y
        merged = t if merged is None else merged + t
    o_ref[...] = x_ref[...] + jnp.dot(merged.astype(BF16), wout_ref[...], preferred_element_type=F32)


def _merge_out(x, o_a, o_b, cqn, mem_kv, proj, w_branch, w_out, *, tm):
    s = x.shape[0]
    row = lambda w: pl.BlockSpec((tm, w), lambda i: (i, 0))
    gate_blk = SEG_START["gates"] * LANES // D_MODEL
    gate = lambda n: pl.BlockSpec((tm, D_MODEL), lambda i: (i, gate_blk + n))
    single = pl.Buffered(1)
    return pl.pallas_call(
        _merge_out_kernel,
        grid=(s // tm,),
        in_specs=[row(D_MODEL), row(BRANCH_WIDTH), row(BRANCH_WIDTH), row(C_WIDTH),
                  pl.BlockSpec(mem_kv.shape, lambda i: (0, 0), pipeline_mode=single),
                  _seg_spec(tm, "az"), _seg_spec(tm, "bz"), _seg_spec(tm, "cz"),
                  gate(0), gate(1), gate(2),
                  pl.BlockSpec(w_branch.shape, lambda i: (0, 0, 0), pipeline_mode=single),
                  pl.BlockSpec(w_out.shape, lambda i: (0, 0), pipeline_mode=single)],
        out_specs=row(D_MODEL),
        out_shape=jax.ShapeDtypeStruct((s, D_MODEL), F32),
        compiler_params=_cparams("parallel"),
        name="merge_out",
    )(x, o_a, o_b, cqn, mem_kv, proj, proj, proj, proj, proj, proj, w_branch, w_out)


def _regroup_tables():
    names = ("aq", "ak", "av", "az", "iq", "ik", "iw", "bcq", "bckv", "bkpe", "bz", "cq", "cz", "gates")
    src_off = dict(zip(names, [0] + IN_OFFSETS))
    src_off["ikw"] = src_off["ik"]
    valid_rows = dict(ikw=IDX_DIM + IDX_HEADS, bkpe=B_ROPE)
    start, valid = [], []
    for name, units in SEG_UNITS.items():
        for u in range(units):
            start.append(src_off[name] + u * LANES)
            valid.append(valid_rows.get(name, LANES))
    return np.asarray(start, np.int32), np.asarray(valid, np.int32)


REGROUP_START, REGROUP_VALID = _regroup_tables()
REGROUP_ALIGN = math.gcd(*(int(v) for v in REGROUP_START if v))
assert REGROUP_ALIGN % SUBLANES == 0


REGROUP_UNITS_PER_STEP = 4
assert PROJ_UNITS % REGROUP_UNITS_PER_STEP == 0


def _regroup_kernel(start_ref, valid_ref, *refs):
    w_refs, o_ref = refs[:-1], refs[-1]
    row = lax.broadcasted_iota(I32, (LANES, o_ref.shape[1]), 0)
    for u, w_ref in enumerate(w_refs):
        valid = valid_ref[pl.program_id(0) * REGROUP_UNITS_PER_STEP + u]
        o_ref[u * LANES:(u + 1) * LANES, :] = jnp.where(row < valid, w_ref[...], 0.0).astype(BF16)


def _regroup_w_in_t(w_in_t):
    d = w_in_t.shape[1]
    ups = REGROUP_UNITS_PER_STEP

    def window(u):
        return pl.BlockSpec((pl.Element(LANES), pl.Element(d)),
                            lambda b, start, valid: (start[b * ups + u] * REGROUP_ALIGN, 0))

    grid_spec = pltpu.PrefetchScalarGridSpec(
        num_scalar_prefetch=2,
        grid=(PROJ_UNITS // ups,),
        in_specs=[window(u) for u in range(ups)],
        out_specs=pl.BlockSpec((ups * LANES, d), lambda b, *_: (b, 0)),
    )
    return pl.pallas_call(
        _regroup_kernel,
        grid_spec=grid_spec,
        out_shape=jax.ShapeDtypeStruct((PROJ_WIDTH, d), BF16),
        compiler_params=_cparams("arbitrary"),
        name="regroup_w_in",
    )(jnp.asarray(REGROUP_START // REGROUP_ALIGN), jnp.asarray(REGROUP_VALID), *([w_in_t] * ups))


def _pad_heads(w, n_heads, width, pad_to):
    r = w.shape[0]
    w = w.reshape(r, n_heads, width)
    w = jnp.pad(w, ((0, 0), (0, 0), (0, pad_to - width)))
    return w.reshape(r, n_heads * pad_to)


class _TilePlan(NamedTuple):
    in_proj_rows: int
    in_proj_cols: int
    attn_a_q: int
    attn_b_q: int
    merge_rows: int


def _tile_plan(seq):
    plan = _TilePlan(in_proj_rows=min(1024, seq), in_proj_cols=2304, attn_a_q=256, attn_b_q=min(512, seq),
                     merge_rows=min(256, seq))
    assert seq % KV_TILE == 0 and PROJ_WIDTH % plan.in_proj_cols == 0
    assert all(seq % t == 0 for t in (plan.in_proj_rows, plan.attn_a_q, plan.attn_b_q, plan.merge_rows))
    return plan


def kernel(x, mem, positions, rel_bias, norm_g, mem_norm_g, w_in, a_q_norm_g, a_k_norm_g, b_q_lat_norm_g,
           b_kv_lat_norm_g, w_b_uq, w_b_ukv, b_q_norm_g, b_k_norm_g, w_mem_kv, c_q_norm_g, c_k_norm_g,
           w_branch, w_out):
    bsz, seq, d = x.shape
    assert d == D_MODEL and norm_g.shape[0] == 1
    topk = min(TOPK_MAX, seq // 4)
    tiles = _tile_plan(seq)
    tq_a, tq_b, tm_in = tiles.attn_a_q, tiles.attn_b_q, tiles.in_proj_rows

    inv_freq = 1.0 / (ROPE_THETA ** (jnp.arange(0, B_ROPE, 2, dtype=F32) / B_ROPE))
    zeros_half = jnp.zeros((B_ROPE,), F32)
    gq_pad = jnp.concatenate([b_q_norm_g[0], zeros_half])[None, :]
    gk_pad = jnp.concatenate([b_k_norm_g[0], zeros_half])[None, :]
    lut_t = ((rel_bias[jnp.asarray(T5_TABLE)] - rel_bias[REL_BUCKETS - 1]).T * LOG2E).astype(F32)

    w_cat = _regroup_w_in_t(jnp.transpose(w_in[0]))
    wuq = _pad_heads(w_b_uq[0], B_HEADS, B_QK_DIM, B_QK_PAD).astype(BF16)
    wukv = w_b_ukv[0].reshape(B_KV_RANK, B_HEADS, B_NOPE + B_V)
    wuk = wukv[:, :, :B_NOPE].reshape(B_KV_RANK, B_HEADS * B_NOPE).astype(BF16)
    wuvt = jnp.transpose(wukv[:, :, B_NOPE:], (1, 2, 0)).astype(BF16)
    wbr = w_branch[0].astype(BF16)
    wout = w_out[0].astype(BF16)

    outs = []
    for b in range(bsz):
        pos = positions[b]
        ang = pos.astype(F32)[:, None] * inv_freq
        cos, sin = jnp.cos(ang), jnp.sin(ang)
        zpad = jnp.zeros((seq, LANES - B_ROPE), F32)
        cos128 = jnp.concatenate([cos, cos, zpad], axis=1)
        sin128 = jnp.concatenate([-sin, sin, zpad], axis=1)

        mem_kv = _mem_kv(mem[b], mem_norm_g[0][None, :], w_mem_kv[0], c_k_norm_g[0][None, :])
        proj = _in_proj(x[b], norm_g[0][None, :], w_cat, tm=tm_in, tn=tiles.in_proj_cols)
        aqn, akn, kab, wt, avt, qb, kb, vbt, cqn, knorm2 = _prep(
            proj, cos128, sin128, a_q_norm_g[0][None, :], a_k_norm_g[0][None, :],
            b_q_lat_norm_g[0][None, :], b_kv_lat_norm_g[0][None, :], gq_pad, gk_pad,
            c_q_norm_g[0][None, :], wuq, wuk, wuvt)
        o_b = _attn_b(qb, kb, vbt, knorm2, tq=tq_b)
        qmin = jnp.min(pos.reshape(seq // tq_a, tq_a), axis=1)
        kmax = jnp.max(pos.reshape(seq // KV_TILE, KV_TILE), axis=1)
        o_a = _attn_a(proj, wt, kab, aqn, akn, avt, pos[None, :], pos[:, None],
                      lut_t, knorm2, qmin, kmax, tq=tq_a, topk=topk)
        outs.append(_merge_out(x[b], o_a, o_b, cqn, mem_kv, proj, wbr, wout, tm=tiles.merge_rows))
    return jnp.stack(outs, axis=0)
```

```python
import functools
import math
from typing import NamedTuple

import numpy as np
import jax
import jax.numpy as jnp
from jax import lax
from jax.experimental import pallas as pl
from jax.experimental.pallas import tpu as pltpu

F32 = jnp.float32
BF16 = jnp.bfloat16
I32 = jnp.int32

D_MODEL = 2048
BRANCH_WIDTH = 1024
N_BRANCH = 3
A_HEADS = 8
A_KV_HEADS = 2
A_HEAD_DIM = 128
A_WIDTH = A_HEADS * A_HEAD_DIM
A_KV_WIDTH = A_KV_HEADS * A_HEAD_DIM
A_REP = A_HEADS // A_KV_HEADS
IDX_HEADS = 16
IDX_DIM = 64
TOPK_MAX = 256
IDX_SCALE = (IDX_DIM ** -0.5) * (IDX_HEADS ** -0.5)
A_SCALE = A_HEAD_DIM ** -0.5
B_HEADS = 8
B_Q_RANK = 512
B_KV_RANK = 256
B_NOPE = 128
B_ROPE = 64
B_QK_DIM = B_NOPE + B_ROPE
B_V = 128
B_WIDTH = B_HEADS * B_V
B_SCALE = B_QK_DIM ** -0.5
ROPE_THETA = 10000.0
C_HEADS = 4
C_HEAD_DIM = 256
C_WIDTH = C_HEADS * C_HEAD_DIM
C_SCALE = C_HEAD_DIM ** -0.5
REL_BUCKETS = 32
REL_MAX_DIST = 128
EPS = 1e-6

IN_SIZES = (A_WIDTH, A_KV_WIDTH, A_KV_WIDTH, A_WIDTH, IDX_HEADS * IDX_DIM, IDX_DIM, IDX_HEADS,
            B_Q_RANK, B_KV_RANK, B_ROPE, B_WIDTH, C_WIDTH, C_WIDTH, N_BRANCH * D_MODEL)
IN_OFFSETS = [int(o) for o in np.cumsum(IN_SIZES)[:-1]]

LANES = 128
SUBLANES = 8
B_QK_PAD = 2 * LANES
VMEM_LIMIT_BYTES = 58 * 1024 * 1024
MASK_VALUE = -1e30
LOG2E = math.log2(math.e)
SHIFT_MARGIN = 1.02
MAX_SHIFT_SPAN = 100.0
KNORM_ROWS = 16
DV = 128
ONES_ROWS = 16
DV_AUG = DV + ONES_ROWS
assert DV == A_HEAD_DIM == B_V
KV_TILE = 512
B_HEADS_PER_STEP = 2

SEG_UNITS = dict(gates=48, aq=8, az=8, iq=8, bz=8, cq=8, cz=8, bcq=4, ak=2, av=2, bckv=2, ikw=1, bkpe=1)
SEG_START = {}
_u = 0
for _name, _w in SEG_UNITS.items():
    SEG_START[_name] = _u
    _u += _w
PROJ_UNITS = _u
PROJ_WIDTH = PROJ_UNITS * LANES


def _t5_bucket_table():
    d = np.arange(LANES)
    max_exact = REL_BUCKETS // 2
    nf = np.maximum(d, 1).astype(np.float64)
    large = max_exact + (np.log(nf / max_exact) / math.log(REL_MAX_DIST / max_exact)
                         * (REL_BUCKETS - max_exact)).astype(np.int64)
    large = np.minimum(large, REL_BUCKETS - 1)
    table = np.where(d < max_exact, d, large).astype(np.int32)
    far = int(np.min(np.nonzero(table == REL_BUCKETS - 1)[0]))
    assert np.all(table[far:] == REL_BUCKETS - 1)
    return table, far


T5_TABLE, T5_FAR = _t5_bucket_table()


def _cparams(*sem):
    return pltpu.CompilerParams(dimension_semantics=sem, vmem_limit_bytes=VMEM_LIMIT_BYTES)


def _dot_t(a, b):
    return lax.dot_general(a, b, (((1,), (1,)), ((), ())), preferred_element_type=F32)


def _rms_scale(ss, n):
    return lax.rsqrt(ss * (1.0 / n) + EPS)


def _mem_kv_kernel(mem_ref, g_ref, w_ref, ckn_ref, o_ref, h_ref):
    j = pl.program_id(0)

    @pl.when(j == 0)
    def _():
        m = mem_ref[...]
        r = _rms_scale(jnp.sum(m * m, axis=-1, keepdims=True), D_MODEL)
        h_ref[...] = (m * r * g_ref[...]).astype(BF16)

    y = jnp.dot(h_ref[...], w_ref[...].astype(BF16), preferred_element_type=F32)

    @pl.when(j < C_HEADS)
    def _():
        r = _rms_scale(jnp.sum(y * y, axis=-1, keepdims=True), C_HEAD_DIM)
        o_ref[...] = (y * r * ckn_ref[...]).astype(BF16)

    @pl.when(j >= C_HEADS)
    def _():
        o_ref[...] = y.astype(BF16)


def _mem_kv(mem, mem_g, w_mem_kv, ck_g):
    n_mem = mem.shape[0]
    return pl.pallas_call(
        _mem_kv_kernel,
        grid=(2 * C_HEADS,),
        in_specs=[
            pl.BlockSpec((n_mem, D_MODEL), lambda j: (0, 0)),
            pl.BlockSpec((1, D_MODEL), lambda j: (0, 0)),
            pl.BlockSpec((D_MODEL, C_HEAD_DIM), lambda j: (0, j)),
            pl.BlockSpec((1, C_HEAD_DIM), lambda j: (0, 0)),
        ],
        out_specs=pl.BlockSpec((n_mem, C_HEAD_DIM), lambda j: (0, j)),
        out_shape=jax.ShapeDtypeStruct((n_mem, 2 * C_WIDTH), BF16),
        scratch_shapes=[pltpu.VMEM((n_mem, D_MODEL), BF16)],
        compiler_params=_cparams("arbitrary"),
        name="mem_kv",
    )(mem, mem_g, w_mem_kv, ck_g)


def _in_proj_kernel(x_ref, g_ref, w_ref, o_ref, h_ref, *, row_chunk):
    @pl.when(pl.program_id(1) == 0)
    def _():
        g = g_ref[...]
        for c in range(x_ref.shape[0] // row_chunk):
            rows = pl.ds(c * row_chunk, row_chunk)
            x = x_ref[rows, :]
            r = _rms_scale(jnp.sum(x * x, axis=-1, keepdims=True), D_MODEL)
            h_ref[rows, :] = (x * r * g).astype(BF16)

    o_ref[...] = _dot_t(h_ref[...], w_ref[...]).astype(o_ref.dtype)


def _in_proj(x, norm_g, w_cat, *, tm, tn):
    s = x.shape[0]
    return pl.pallas_call(
        functools.partial(_in_proj_kernel, row_chunk=min(tm, 256)),
        grid=(s // tm, PROJ_WIDTH // tn),
        in_specs=[
            pl.BlockSpec((tm, D_MODEL), lambda i, j: (i, 0)),
            pl.BlockSpec((1, D_MODEL), lambda i, j: (0, 0)),
            pl.BlockSpec((tn, D_MODEL), lambda i, j: (j, 0)),
        ],
        out_specs=pl.BlockSpec((tm, tn), lambda i, j: (i, j)),
        out_shape=jax.ShapeDtypeStruct((s, PROJ_WIDTH), BF16),
        scratch_shapes=[pltpu.VMEM((tm, D_MODEL), BF16)],
        compiler_params=_cparams("parallel", "arbitrary"),
        name="in_proj",
    )(x, norm_g, w_cat)


def _rope128(x, cos, sin):
    half = B_ROPE // 2
    lane = lax.broadcasted_iota(I32, x.shape, 1)
    partner = jnp.where(lane < half, pltpu.roll(x, LANES - half, 1), pltpu.roll(x, half, 1))
    return x * cos + partner * sin


def _prep_kernel(aq_ref, ak_ref, av_ref, ikw_ref, bcq_ref, bckv_ref, bkpe_ref, cq_ref,
                 cos_ref, sin_ref, aqg_ref, akg_ref, bqlg_ref, bkvlg_ref, bqg_ref, bkg_ref, cqg_ref,
                 wuq_ref, wuk_ref, wuvt_ref,
                 aqn_ref, akn_ref, kab_ref, wt_ref, avt_ref, qb_ref, kb_ref, vbt_ref, cqn_ref, kn_ref):
    cos = cos_ref[...]
    sin = sin_ref[...]

    @pl.when(pl.program_id(0) == 0)
    def _():
        kn_ref[...] = jnp.zeros(kn_ref.shape, F32)

    def note_key_norm(row, sq_norms):
        top = jnp.max(sq_norms, axis=0, keepdims=True)
        kn_ref[row:row + 1, :] = jnp.maximum(kn_ref[row:row + 1, :], top)

    eye = jnp.where(lax.broadcasted_iota(I32, (LANES, LANES), 0) == lax.broadcasted_iota(I32, (LANES, LANES), 1),
                    1.0, 0.0).astype(BF16)

    aqg = aqg_ref[...] * (A_SCALE * LOG2E)
    for h in range(A_HEADS):
        cols = slice(h * A_HEAD_DIM, (h + 1) * A_HEAD_DIM)
        v = aq_ref[:, cols].astype(F32)
        r = _rms_scale(jnp.sum(v * v, axis=-1, keepdims=True), A_HEAD_DIM)
        aqn_ref[:, cols] = (v * r * aqg).astype(BF16)
    akg = akg_ref[...]
    for h in range(A_KV_HEADS):
        cols = slice(h * A_HEAD_DIM, (h + 1) * A_HEAD_DIM)
        v = ak_ref[:, cols].astype(F32)
        r = _rms_scale(jnp.sum(v * v, axis=-1, keepdims=True), A_HEAD_DIM)
        kn = v * r * akg
        akn_ref[:, cols] = kn.astype(BF16)
        note_key_norm(B_HEADS + h, jnp.sum(kn * kn, axis=-1, keepdims=True))
        avt_ref[h, 0, 0:DV] = _dot_t(eye, av_ref[:, cols]).astype(BF16)
        avt_ref[h, 0, DV:DV_AUG] = jnp.ones((ONES_ROWS, avt_ref.shape[-1]), BF16)

    ikw = ikw_ref[...]
    lane = lax.broadcasted_iota(I32, ikw.shape, 1)
    zero = jnp.zeros_like(ikw)
    kab_ref[:, 0:LANES] = jnp.where(lane < IDX_DIM, ikw, zero)
    kab_ref[:, LANES:2 * LANES] = jnp.where(lane >= IDX_DIM, pltpu.roll(ikw.astype(F32), IDX_DIM, 1).astype(BF16), zero)
    wt_ref[...] = _dot_t(eye, ikw) * IDX_SCALE

    cq_lat = bcq_ref[...].astype(F32)
    r = _rms_scale(jnp.sum(cq_lat * cq_lat, axis=-1, keepdims=True), B_Q_RANK)
    qlat = (cq_lat * r * bqlg_ref[...]).astype(BF16)
    bqg = bqg_ref[...] * (B_SCALE * LOG2E)
    for h in range(B_HEADS):
        cols = slice(h * B_QK_PAD, (h + 1) * B_QK_PAD)
        qh = jnp.dot(qlat, wuq_ref[:, cols], preferred_element_type=F32)
        r = _rms_scale(jnp.sum(qh * qh, axis=-1, keepdims=True), B_QK_DIM)
        qn = qh * r * bqg
        qb_ref[:, h * B_QK_PAD:h * B_QK_PAD + LANES] = qn[:, :LANES].astype(BF16)
        qb_ref[:, h * B_QK_PAD + LANES:(h + 1) * B_QK_PAD] = _rope128(qn[:, LANES:], cos, sin).astype(BF16)

    ckv = bckv_ref[...].astype(F32)
    r = _rms_scale(jnp.sum(ckv * ckv, axis=-1, keepdims=True), B_KV_RANK)
    kvlat = (ckv * r * bkvlg_ref[...]).astype(BF16)
    kpe = bkpe_ref[...].astype(F32)
    ss_pe = jnp.sum(kpe * kpe, axis=-1, keepdims=True)
    bkg = bkg_ref[...]
    kpe_rot = _rope128(kpe * bkg[:, LANES:], cos, sin)
    for h in range(B_HEADS):
        kn = jnp.dot(kvlat, wuk_ref[:, h * B_NOPE:(h + 1) * B_NOPE], preferred_element_type=F32)
        r = _rms_scale(jnp.sum(kn * kn, axis=-1, keepdims=True) + ss_pe, B_QK_DIM)
        k_nope = kn * r * bkg[:, :LANES]
        k_pe = kpe_rot * r
        kb_ref[:, h * B_QK_PAD:h * B_QK_PAD + LANES] = k_nope.astype(BF16)
        kb_ref[:, h * B_QK_PAD + LANES:(h + 1) * B_QK_PAD] = k_pe.astype(BF16)
        note_key_norm(h, jnp.sum(k_nope * k_nope, axis=-1, keepdims=True)
                      + jnp.sum(k_pe * k_pe, axis=-1, keepdims=True))
        vbt_ref[h, 0, 0:DV] = _dot_t(wuvt_ref[h], kvlat).astype(BF16)
        vbt_ref[h, 0, DV:DV_AUG] = jnp.ones((ONES_ROWS, vbt_ref.shape[-1]), BF16)

    cqg = cqg_ref[...] * C_SCALE
    for h in range(C_HEADS):
        cols = slice(h * C_HEAD_DIM, (h + 1) * C_HEAD_DIM)
        v = cq_ref[:, cols].astype(F32)
        r = _rms_scale(jnp.sum(v * v, axis=-1, keepdims=True), C_HEAD_DIM)
        cqn_ref[:, cols] = (v * r * cqg).astype(BF16)


def _seg_spec(tm, name):
    units = SEG_UNITS[name]
    blk = SEG_START[name] // units
    return pl.BlockSpec((tm, units * LANES), lambda i: (i, blk))


def _full_spec(shape):
    return pl.BlockSpec(shape, lambda i: (0,) * len(shape))


def _prep(proj, cos128, sin128, aqg, akg, bqlg, bkvlg, bqg, bkg, cqg, wuq, wuk, wuvt):
    s = proj.shape[0]
    tm = KV_TILE
    nt = s // tm
    row = lambda w: pl.BlockSpec((tm, w), lambda i: (i, 0))
    tiles_t = lambda n: pl.BlockSpec((n, 1, DV_AUG, tm), lambda i: (0, i, 0, 0))
    small = [aqg, akg, bqlg, bkvlg, bqg, bkg, cqg, wuq, wuk, wuvt]
    out_specs = [row(A_WIDTH), row(A_KV_WIDTH), row(2 * LANES),
                 pl.BlockSpec((LANES, tm), lambda i: (0, i)), tiles_t(A_KV_HEADS),
                 row(B_HEADS * B_QK_PAD), row(B_HEADS * B_QK_PAD), tiles_t(B_HEADS), row(C_WIDTH),
                 _full_spec((KNORM_ROWS, LANES))]
    out_shape = [
        jax.ShapeDtypeStruct((s, A_WIDTH), BF16),
        jax.ShapeDtypeStruct((s, A_KV_WIDTH), BF16),
        jax.ShapeDtypeStruct((s, 2 * LANES), BF16),
        jax.ShapeDtypeStruct((LANES, s), F32),
        jax.ShapeDtypeStruct((A_KV_HEADS, nt, DV_AUG, tm), BF16),
        jax.ShapeDtypeStruct((s, B_HEADS * B_QK_PAD), BF16),
        jax.ShapeDtypeStruct((s, B_HEADS * B_QK_PAD), BF16),
        jax.ShapeDtypeStruct((B_HEADS, nt, DV_AUG, tm), BF16),
        jax.ShapeDtypeStruct((s, C_WIDTH), BF16),
        jax.ShapeDtypeStruct((KNORM_ROWS, LANES), F32),
    ]
    return pl.pallas_call(
        _prep_kernel,
        grid=(nt,),
        in_specs=[_seg_spec(tm, n) for n in ("aq", "ak", "av", "ikw", "bcq", "bckv", "bkpe", "cq")]
        + [row(LANES), row(LANES)] + [_full_spec(a.shape) for a in small],
        out_specs=out_specs,
        out_shape=out_shape,
        compiler_params=_cparams("arbitrary"),
        name="prep",
    )(*([proj] * 8), cos128, sin128, *small)


def _reduce_keys(x, op):
    tk, tq = x.shape
    n = tk // SUBLANES
    assert n & (n - 1) == 0
    t = x.reshape(n, SUBLANES, tq)
    while n > 1:
        n //= 2
        t = op(t[:n], t[n:2 * n])
    red = jnp.max if op is jnp.maximum else jnp.sum
    return red(t[0], axis=0, keepdims=True)


def _softmax_step_t(st, vt, m_ref, acc_ref):
    m_prev = m_ref[...]
    m_new = jnp.maximum(m_prev, _reduce_keys(st, jnp.maximum))
    alpha = jnp.exp2(m_prev - m_new)
    p = jnp.exp2(st - m_new).astype(BF16)
    acc_ref[...] = alpha * acc_ref[...] + jnp.dot(vt, p, preferred_element_type=F32)
    m_ref[...] = m_new


def _fixed_shift_step_t(st, vt, shift, acc_ref):
    p = jnp.exp2(st - shift).astype(BF16)
    acc_ref[...] += jnp.dot(vt, p, preferred_element_type=F32)


def _normalised_output(acc):
    return (acc[0:DV] / acc[DV:DV + 1]).T


def _logit_bound(q, kmax2):
    qf = q.astype(F32)
    ones = jnp.ones((SUBLANES, q.shape[1]), BF16)
    qn2 = _dot_t(ones, (qf * qf).astype(BF16))[0:1]
    return jnp.sqrt(qn2 * kmax2) * SHIFT_MARGIN


def _causal_ok_t(i, j, tq, tk):
    key = j * tk + lax.broadcasted_iota(I32, (tk, tq), 0)
    qry = i * tq + lax.broadcasted_iota(I32, (tk, tq), 1)
    return key <= qry


def _init_softmax_state(m_ref, acc_ref):
    m_ref[...] = jnp.full(m_ref.shape, MASK_VALUE, F32)
    acc_ref[...] = jnp.zeros(acc_ref.shape, F32)


def _pingpong_tiles(n_full, logits, consume, buf_a, buf_b):
    def pair(p, c):
        j = 2 * p
        logits(j + 1, buf_b)
        consume(j, buf_a, False)
        logits(j + 2, buf_a)
        consume(j + 1, buf_b, False)
        return c

    lax.fori_loop(0, n_full // 2, pair, 0)
    odd = n_full % 2 == 1

    @pl.when(odd)
    def _():
        logits(n_full, buf_b)
        consume(n_full - 1, buf_a, False)
        consume(n_full, buf_b, True)

    @pl.when(jnp.logical_not(odd))
    def _():
        consume(n_full, buf_a, True)


def _attn_b_kernel(q_ref, k_ref, vt_ref, kn_ref, o_ref, sa_ref, sb_ref, m_ref, acc_ref, *, tq, tk):
    hp = pl.program_id(0)
    i = pl.program_id(1)
    _init_softmax_state(m_ref, acc_ref)
    j_diag = (i * tq) // tk
    heads = range(B_HEADS_PER_STEP)

    def logits(j, buf):
        rows = pl.ds(pl.multiple_of(j * tk, tk), tk)
        for hh in heads:
            cols = slice(hh * B_QK_PAD, (hh + 1) * B_QK_PAD)
            buf[hh] = _dot_t(k_ref[rows, cols], q_ref[:, cols])

    logits(0, sa_ref)
    shifts = [_logit_bound(q_ref[:, hh * B_QK_PAD:(hh + 1) * B_QK_PAD],
                           kn_ref[pl.ds(hp * B_HEADS_PER_STEP + hh, 1), 0:1]) for hh in heads]
    fixed_ok = jnp.max(2.0 * functools.reduce(jnp.maximum, shifts)) <= MAX_SHIFT_SPAN

    def consume(j, buf, masked, fixed):
        for hh in heads:
            st = buf[hh]
            if masked:
                st = jnp.where(_causal_ok_t(i, j, tq, tk), st, MASK_VALUE)
            if fixed:
                _fixed_shift_step_t(st, vt_ref[hh, j], shifts[hh], acc_ref.at[hh])
            else:
                _softmax_step_t(st, vt_ref[hh, j], m_ref.at[hh], acc_ref.at[hh])

    @pl.when(fixed_ok)
    def _():
        _pingpong_tiles(j_diag, logits, functools.partial(consume, fixed=True), sa_ref, sb_ref)

    @pl.when(jnp.logical_not(fixed_ok))
    def _():
        _pingpong_tiles(j_diag, logits, functools.partial(consume, fixed=False), sa_ref, sb_ref)

    for hh in heads:
        o_ref[:, hh * B_V:(hh + 1) * B_V] = _normalised_output(acc_ref[hh]).astype(o_ref.dtype)


def _attn_b(qb, kb, vbt, knorm2, *, tq):
    s = qb.shape[0]
    tk = KV_TILE
    hps = B_HEADS_PER_STEP
    assert tk % tq == 0 or tq % tk == 0
    assert tq <= tk
    return pl.pallas_call(
        functools.partial(_attn_b_kernel, tq=tq, tk=tk),
        grid=(B_HEADS // hps, s // tq),
        in_specs=[
            pl.BlockSpec((tq, hps * B_QK_PAD), lambda h, i: (i, h)),
            pl.BlockSpec((s, hps * B_QK_PAD), lambda h, i: (0, h)),
            pl.BlockSpec((hps, s // tk, DV_AUG, tk), lambda h, i: (h, 0, 0, 0)),
            pl.BlockSpec((KNORM_ROWS, LANES), lambda h, i: (0, 0)),
        ],
        out_specs=pl.BlockSpec((tq, hps * B_V), lambda h, i: (i, h)),
        out_shape=jax.ShapeDtypeStruct((s, B_WIDTH), BF16),
        scratch_shapes=[pltpu.VMEM((hps, tk, tq), F32), pltpu.VMEM((hps, tk, tq), F32),
                        pltpu.VMEM((hps, 1, tq), F32), pltpu.VMEM((hps, DV_AUG, tq), F32)],
        compiler_params=_cparams("parallel", "arbitrary"),
        name="attn_b",
    )(qb, kb, vbt, knorm2)


KEY_BITS = 32
COARSE_BITS = 16
BISECT_GROUP = 4
BRACKET_GROUP = 2
PACKED_SUBLANES = 16
KEY_NEG_INF = (0xFF800000 ^ 0x7FFFFFFF) - (1 << 32)
BRACKET_HALF = (1 << (KEY_BITS - COARSE_BITS - 1)) + 1
BRACKET_STEPS = (KEY_BITS - COARSE_BITS + 1 + BRACKET_GROUP) // BRACKET_GROUP * BRACKET_GROUP
assert KEY_BITS % BISECT_GROUP == 0
assert (1 << BRACKET_STEPS) > 2 * BRACKET_HALF + 1


def _key_as_f32(key):
    bits = key ^ (lax.shift_right_arithmetic(key, 31) & 0x7FFFFFFF)
    return lax.bitcast_convert_type(bits, F32)


def _coarse_key_as_bf16(k):
    b = k ^ (lax.shift_right_arithmetic(k, COARSE_BITS - 1) & ((1 << (COARSE_BITS - 1)) - 1))
    return lax.bitcast_convert_type(lax.shift_left(b, COARSE_BITS), F32).astype(BF16)


def _count_hits_packed(hit):
    tk, tq = hit.shape
    n = tk // PACKED_SUBLANES
    assert n <= 256
    t = jnp.where(hit, jnp.ones((), BF16), jnp.zeros((), BF16)).reshape(n, PACKED_SUBLANES, tq)
    while n > 1:
        n //= 2
        t = t[:n] + t[n:2 * n]
    return t[0].astype(F32)


def _count_hits(hit):
    tk, tq = hit.shape
    n = tk // SUBLANES
    t = jnp.where(hit, 1, 0).reshape(n, SUBLANES, tq)
    while n > 1:
        n //= 2
        t = t[:n] + t[n:2 * n]
    return t[0]


def _attn_a_kernel(qmin_ref, kmax_ref,
                   iq_ref, wt_ref, kab_ref, aqn_ref, akn_ref, avt_ref, pq_ref, pk_ref, lut_ref, kn_ref,
                   o_ref,
                   score_ref, coarse_ref, thr_ref, settled_ref, sa_ref, sb_ref, pa_ref, pb_ref, m_ref, acc_ref,
                   *, tq, tk, topk):
    i = pl.program_id(0)
    j_diag = (i * tq) // tk
    n_tiles = j_diag + 1
    int_min = jnp.int32(-2 ** 31)

    def head_pair_dots(j, p):
        rows = pl.ds(pl.multiple_of(j * tk, tk), tk)
        rhs = iq_ref[:, p * LANES:(p + 1) * LANES]
        return _dot_t(kab_ref[rows, 0:LANES], rhs), _dot_t(kab_ref[rows, LANES:2 * LANES], rhs)

    def score_tile(j, masked):
        acc = jnp.zeros((tk, tq), F32)
        n_pairs = IDX_HEADS // 2
        for p in range(n_pairs):
            d_lo, d_hi = (sa_ref[...], sb_ref[...]) if p == 0 else head_pair_dots(j, p)
            if p == n_pairs - 1 and not masked:
                sa_ref[...], sb_ref[...] = head_pair_dots(j + 1, 0)
            w_row = IDX_DIM + 2 * p
            acc = acc + jnp.maximum(d_lo, 0.0) * wt_ref[w_row:w_row + 1, :]
            acc = acc + jnp.maximum(d_hi, 0.0) * wt_ref[w_row + 1:w_row + 2, :]
        if masked:
            acc = jnp.where(_causal_ok_t(i, j, tq, tk), acc, -jnp.inf)
        score_ref[j] = acc
        coarse_ref[j] = acc.astype(BF16)

    def score_body(j, c):
        score_tile(j, False)
        return c

    sa_ref[...], sb_ref[...] = head_pair_dots(0, 0)
    lax.fori_loop(0, j_diag, score_body, 0)
    score_tile(j_diag, True)

    def over_tiles(tile_count, init):
        def single(j, cnt):
            return cnt + tile_count(j)

        def double(p, cnt):
            return cnt + (tile_count(2 * p) + tile_count(2 * p + 1))

        cnt = lax.fori_loop(0, n_tiles // 2, double, init)
        cnt = lax.fori_loop(2 * (n_tiles // 2), n_tiles, single, cnt)
        return jnp.sum(cnt, axis=0, keepdims=True)

    def count_queries(pred_fn):
        return over_tiles(lambda j: _count_hits(pred_fn(score_ref[j], j)), jnp.zeros((SUBLANES, tq), I32))

    def count_at_or_above(cand_key):
        cand = _key_as_f32(cand_key)
        cnt = count_queries(lambda x, j: x >= cand)
        return jnp.where(cand_key <= KEY_NEG_INF, n_tiles * tk, cnt)

    def store_threshold(thr_key, settled):
        thr_ref[...] = jnp.where(thr_key <= KEY_NEG_INF, -jnp.inf, _key_as_f32(thr_key))
        settled_ref[...] = settled

    def bisect_all_bits():
        def cond(c):
            g, _, s = c
            return (g < KEY_BITS // BISECT_GROUP) & (jnp.min(s) == 0)

        def body(c):
            g, t, s = c
            for u in range(BISECT_GROUP):
                bit = KEY_BITS - 1 - (g * BISECT_GROUP + u)
                cand = t + lax.shift_left(jnp.int32(1), bit)
                cnt = count_at_or_above(cand)
                t = jnp.where((s == 0) & (cnt >= topk), cand, t)
                s = jnp.where(cnt == topk, 1, s)
            return g + 1, t, s

        _, t, s = lax.while_loop(cond, body, (jnp.int32(0), jnp.full((1, tq), int_min, I32),
                                              jnp.zeros((1, tq), I32)))
        store_threshold(t, s)

    def bisect_bracket(lo, hi, cnt_lo):
        def cond(c):
            g, lo, hi, _, s = c
            done = (s == 1) | (hi - lo <= 1)
            return (g < BRACKET_STEPS // BRACKET_GROUP) & (jnp.min(jnp.where(done, 1, 0)) == 0)

        def body(c):
            g, lo, hi, t, s = c
            for _ in range(BRACKET_GROUP):
                mid = lo + lax.shift_right_arithmetic(hi - lo, 1)
                cnt = count_at_or_above(mid)
                t = jnp.where((s == 0) & (cnt == topk), mid, t)
                s = jnp.where(cnt == topk, 1, s)
                lo, hi = jnp.where(cnt >= topk, mid, lo), jnp.where(cnt >= topk, hi, mid)
            return g + 1, lo, hi, t, s

        settled0 = jnp.where(cnt_lo == topk, 1, 0)
        _, lo, _, t, s = lax.while_loop(cond, body, (jnp.int32(0), lo, hi, lo, settled0))
        store_threshold(jnp.where(s == 1, t, lo), s)

    def count_coarse(cand):
        return over_tiles(lambda j: _count_hits_packed(coarse_ref[j] >= cand),
                          jnp.zeros((PACKED_SUBLANES, tq), F32))

    def coarse_step(it, t):
        cand = t + lax.shift_left(jnp.int32(1), COARSE_BITS - 1 - it)
        return jnp.where(count_coarse(_coarse_key_as_bf16(cand)) >= topk, cand, t)

    t_coarse = lax.fori_loop(0, COARSE_BITS, coarse_step,
                             jnp.full((1, tq), -(1 << (COARSE_BITS - 1)), I32))

    low_bits = KEY_BITS - COARSE_BITS
    centre = lax.shift_left(t_coarse, low_bits) + jnp.where(t_coarse < 0, (1 << low_bits) - 1, 0)
    lo_key, hi_key = centre - BRACKET_HALF, centre + BRACKET_HALF + 1
    cnt_lo = count_at_or_above(lo_key)
    cnt_hi = count_at_or_above(hi_key)
    bracket_ok = (cnt_lo >= topk) & (cnt_hi < topk) & (lo_key < hi_key)
    coarse_ok = jnp.min(jnp.where(bracket_ok, 1, 0)) == 1

    pl.when(coarse_ok)(lambda: bisect_bracket(lo_key, hi_key, cnt_lo))
    pl.when(jnp.logical_not(coarse_ok))(bisect_all_bits)

    thr = thr_ref[...]
    settled = settled_ref[...]

    def drop_excess_ties():
        cnt_gt = count_queries(lambda x, j: x > thr)
        need = topk - cnt_gt
        kpos = lax.broadcasted_iota(I32, (tk, tq), 0)

        def pos_body(it, cut):
            cand = cut + lax.shift_left(jnp.int32(1), 30 - it)
            cnt = count_queries(lambda x, j: (x == thr) & (j * tk + kpos < cand))
            return jnp.where(cnt < need, cand, cut)

        cut = lax.fori_loop(0, 31, pos_body, jnp.zeros((1, tq), I32))

        def drop_body(j, c):
            x = score_ref[j]
            drop = (x == thr) & (j * tk + kpos > cut)
            score_ref[j] = jnp.where(drop, -jnp.inf, x)
            return c

        lax.fori_loop(0, n_tiles, drop_body, 0)

    @pl.when(jnp.min(settled) == 0)
    def _():
        cnt_ge = count_queries(lambda x, j: x >= thr)
        pl.when(jnp.max(cnt_ge) > topk)(drop_excess_ties)

    _init_softmax_state(m_ref, acc_ref)
    pq = pq_ref[...]
    lut = lut_ref[...]
    bias_hi = jnp.max(lut, axis=1, keepdims=True)
    bias_span = bias_hi - jnp.min(lut, axis=1, keepdims=True)
    bounds = [_logit_bound(aqn_ref[:, h * A_HEAD_DIM:(h + 1) * A_HEAD_DIM],
                           kn_ref[B_HEADS + h // A_REP:B_HEADS + h // A_REP + 1, 0:1]) for h in range(A_HEADS)]
    shifts = [bounds[h] + bias_hi[h:h + 1] for h in range(A_HEADS)]
    spans = [2.0 * bounds[h] + bias_span[h:h + 1] for h in range(A_HEADS)]
    fixed_ok = jnp.max(functools.reduce(jnp.maximum, spans)) <= MAX_SHIFT_SPAN

    def head_buf(h):
        return (sa_ref, sb_ref)[h % 2]

    def logits(j, h, buf):
        rows = pl.ds(pl.multiple_of(j * tk, tk), tk)
        g = h // A_REP
        buf[...] = _dot_t(akn_ref[rows, g * A_HEAD_DIM:(g + 1) * A_HEAD_DIM],
                          aqn_ref[:, h * A_HEAD_DIM:(h + 1) * A_HEAD_DIM])

    def attend(j, masked, near, fixed):
        rows = pl.ds(pl.multiple_of(j * tk, tk), tk)
        sel = score_ref[j] >= thr
        if masked:
            sel = sel & _causal_ok_t(i, j, tq, tk)
        mask_bias = jnp.where(sel, 0.0, MASK_VALUE)
        if near:
            dist = jnp.clip(pq - pk_ref[rows, :], 0, LANES - 1)

        logits(j, 0, head_buf(0))
        for h in range(A_HEADS):
            if h + 1 < A_HEADS:
                logits(j, h + 1, head_buf(h + 1))
            st = head_buf(h)[...] + mask_bias
            if near:
                table = jnp.broadcast_to(lut_ref[h:h + 1, :], (tk, LANES))
                st = st + jnp.concatenate(
                    [jnp.take_along_axis(table, dist[:, c * LANES:(c + 1) * LANES], axis=1,
                                         mode="promise_in_bounds")
                     for c in range(tq // LANES)], axis=1)
            if fixed:
                _fixed_shift_step_t(st, avt_ref[h // A_REP, j], shifts[h], acc_ref.at[h])
            else:
                _softmax_step_t(st, avt_ref[h // A_REP, j], m_ref.at[h], acc_ref.at[h])

    def is_far(j):
        return qmin_ref[i] - kmax_ref[j] >= T5_FAR

    def attend_dyn(j, masked, fixed):
        far = is_far(j)

        @pl.when(far)
        def _():
            attend(j, masked, False, fixed)

        @pl.when(jnp.logical_not(far))
        def _():
            attend(j, masked, True, fixed)

    def attend_far_pair(j):
        rows = pl.ds(pl.multiple_of(j * tk, tk), 2 * tk)
        mask_bias = jnp.concatenate([jnp.where(score_ref[j] >= thr, 0.0, MASK_VALUE),
                                     jnp.where(score_ref[j + 1] >= thr, 0.0, MASK_VALUE)], axis=0)
        pair_bufs = (pa_ref, pb_ref)

        def pair_logits(h):
            g = h // A_REP
            pair_bufs[h % 2][...] = _dot_t(akn_ref[rows, g * A_HEAD_DIM:(g + 1) * A_HEAD_DIM],
                                           aqn_ref[:, h * A_HEAD_DIM:(h + 1) * A_HEAD_DIM])

        pair_logits(0)
        for h in range(A_HEADS):
            if h + 1 < A_HEADS:
                pair_logits(h + 1)
            p = jnp.exp2(pair_bufs[h % 2][...] + mask_bias - shifts[h]).astype(BF16)
            g = h // A_REP
            acc_ref[h] += (jnp.dot(avt_ref[g, j], p[:tk], preferred_element_type=F32)
                           + jnp.dot(avt_ref[g, j + 1], p[tk:], preferred_element_type=F32))

    def attend_all(fixed):
        def attend_body(j, c):
            attend_dyn(j, False, fixed)
            return c

        if fixed:
            def pair_body(p, c):
                j = 2 * p
                both_far = is_far(j) & is_far(j + 1)

                @pl.when(both_far)
                def _():
                    attend_far_pair(j)

                @pl.when(jnp.logical_not(both_far))
                def _():
                    lax.fori_loop(j, j + 2, attend_body, 0)

                return c

            n_pairs = j_diag // 2
            lax.fori_loop(0, n_pairs, pair_body, 0)
            lax.fori_loop(2 * n_pairs, j_diag, attend_body, 0)
        else:
            lax.fori_loop(0, j_diag, attend_body, 0)
        attend_dyn(j_diag, True, fixed)

    pl.when(fixed_ok)(functools.partial(attend_all, True))
    pl.when(jnp.logical_not(fixed_ok))(functools.partial(attend_all, False))

    for h in range(A_HEADS):
        q_cols = slice(h * A_HEAD_DIM, (h + 1) * A_HEAD_DIM)
        o_ref[:, q_cols] = _normalised_output(acc_ref[h]).astype(o_ref.dtype)


def _attn_a(proj, wt, kab, aqn, akn, avt, pos_row, pos_col, lut_t, knorm2, qmin, kmax, *, tq, topk):
    s = proj.shape[0]
    tk = KV_TILE
    assert tk % tq == 0
    iq_blk = SEG_START["iq"] // SEG_UNITS["iq"]
    grid_spec = pltpu.PrefetchScalarGridSpec(
        num_scalar_prefetch=2,
        grid=(s // tq,),
        in_specs=[
            pl.BlockSpec((tq, IDX_HEADS * IDX_DIM), lambda i, *_: (i, iq_blk)),
            pl.BlockSpec((LANES, tq), lambda i, *_: (0, i)),
            pl.BlockSpec((s, 2 * LANES), lambda i, *_: (0, 0), pipeline_mode=pl.Buffered(1)),
            pl.BlockSpec((tq, A_WIDTH), lambda i, *_: (i, 0)),
            pl.BlockSpec((s, A_KV_WIDTH), lambda i, *_: (0, 0), pipeline_mode=pl.Buffered(1)),
            pl.BlockSpec((A_KV_HEADS, s // tk, DV_AUG, tk), lambda i, *_: (0, 0, 0, 0),
                         pipeline_mode=pl.Buffered(1)),
            pl.BlockSpec((1, tq), lambda i, *_: (0, i)),
            pl.BlockSpec((s, 1), lambda i, *_: (0, 0), pipeline_mode=pl.Buffered(1)),
            pl.BlockSpec((A_HEADS, LANES), lambda i, *_: (0, 0)),
            pl.BlockSpec((KNORM_ROWS, LANES), lambda i, *_: (0, 0)),
        ],
        out_specs=pl.BlockSpec((tq, A_WIDTH), lambda i, *_: (i, 0)),
        scratch_shapes=[
            pltpu.VMEM((s // tk, tk, tq), F32),
            pltpu.VMEM((s // tk, tk, tq), BF16),
            pltpu.VMEM((1, tq), F32),
            pltpu.VMEM((1, tq), I32),
            pltpu.VMEM((tk, tq), F32),
            pltpu.VMEM((tk, tq), F32),
            pltpu.VMEM((2 * tk, tq), F32),
            pltpu.VMEM((2 * tk, tq), F32),
            pltpu.VMEM((A_HEADS, 1, tq), F32),
            pltpu.VMEM((A_HEADS, DV_AUG, tq), F32),
        ],
    )
    return pl.pallas_call(
        functools.partial(_attn_a_kernel, tq=tq, tk=tk, topk=topk),
        grid_spec=grid_spec,
        out_shape=jax.ShapeDtypeStruct((s, A_WIDTH), BF16),
        compiler_params=_cparams("arbitrary"),
        name="attn_a",
    )(qmin, kmax, proj, wt, kab, aqn, akn, avt, pos_row, pos_col, lut_t, knorm2)


def _memory_attention(q_ref, kv_ref):
    outs = []
    for h in range(C_HEADS):
        cols = slice(h * C_HEAD_DIM, (h + 1) * C_HEAD_DIM)
        s = _dot_t(q_ref[:, cols], kv_ref[:, cols])
        p = jnp.exp(s - jnp.max(s, axis=-1, keepdims=True))
        o = jnp.dot(p.astype(BF16), kv_ref[:, C_WIDTH + h * C_HEAD_DIM:C_WIDTH + (h + 1) * C_HEAD_DIM],
                    preferred_element_type=F32)
        outs.append(o / jnp.sum(p, axis=-1, keepdims=True))
    return jnp.concatenate(outs, axis=1)


def _merge_out_kernel(x_ref, oa_ref, ob_ref, cq_ref, mkv_ref, az_ref, bz_ref, cz_ref, ga_ref, gb_ref, gc_ref,
                      wbr_ref, wout_ref, o_ref):
    branch_out = (lambda: oa_ref[...].astype(F32), lambda: ob_ref[...].astype(F32),
                  lambda: _memory_attention(cq_ref, mkv_ref))
    merged = None
    for n, (out_fn, z_r, g_r) in enumerate(zip(branch_out, (az_ref, bz_ref, cz_ref), (ga_ref, gb_ref, gc_ref))):
        z = z_r[...].astype(F32)
        u = (out_fn() * (z * jax.nn.sigmoid(z))).astype(BF16)
        y = jnp.dot(u, wbr_ref[n], preferred_element_type=F32)
        t = jax.nn.sigmoid(g_r[...].astype(F32)) * y
        merged = t if merged is None else merged + t
    o_ref[...] = x_ref[...] + jnp.dot(merged.astype(BF16), wout_ref[...], preferred_element_type=F32)


def _merge_out(x, o_a, o_b, cqn, mem_kv, proj, w_branch, w_out, *, tm):
    s = x.shape[0]
    row = lambda w: pl.BlockSpec((tm, w), lambda i: (i, 0))
    gate_blk = SEG_START["gates"] * LANES // D_MODEL
    gate = lambda n: pl.BlockSpec((tm, D_MODEL), lambda i: (i, gate_blk + n))
    single = pl.Buffered(1)
    return pl.pallas_call(
        _merge_out_kernel,
        grid=(s // tm,),
        in_specs=[row(D_MODEL), row(BRANCH_WIDTH), row(BRANCH_WIDTH), row(C_WIDTH),
                  pl.BlockSpec(mem_kv.shape, lambda i: (0, 0), pipeline_mode=single),
                  _seg_spec(tm, "az"), _seg_spec(tm, "bz"), _seg_spec(tm, "cz"),
                  gate(0), gate(1), gate(2),
                  pl.BlockSpec(w_branch.shape, lambda i: (0, 0, 0), pipeline_mode=single),
                  pl.BlockSpec(w_out.shape, lambda i: (0, 0), pipeline_mode=single)],
        out_specs=row(D_MODEL),
        out_shape=jax.ShapeDtypeStruct((s, D_MODEL), F32),
        compiler_params=_cparams("parallel"),
        name="merge_out",
    )(x, o_a, o_b, cqn, mem_kv, proj, proj, proj, proj, proj, proj, w_branch, w_out)


def _regroup_tables():
    names = ("aq", "ak", "av", "az", "iq", "ik", "iw", "bcq", "bckv", "bkpe", "bz", "cq", "cz", "gates")
    src_off = dict(zip(names, [0] + IN_OFFSETS))
    src_off["ikw"] = src_off["ik"]
    valid_rows = dict(ikw=IDX_DIM + IDX_HEADS, bkpe=B_ROPE)
    start, valid = [], []
    for name, units in SEG_UNITS.items():
        for u in range(units):
            start.append(src_off[name] + u * LANES)
            valid.append(valid_rows.get(name, LANES))
    return np.asarray(start, np.int32), np.asarray(valid, np.int32)


REGROUP_START, REGROUP_VALID = _regroup_tables()
REGROUP_ALIGN = math.gcd(*(int(v) for v in REGROUP_START if v))
assert REGROUP_ALIGN % SUBLANES == 0


REGROUP_UNITS_PER_STEP = 4
assert PROJ_UNITS % REGROUP_UNITS_PER_STEP == 0


def _regroup_kernel(start_ref, valid_ref, *refs):
    w_refs, o_ref = refs[:-1], refs[-1]
    row = lax.broadcasted_iota(I32, (LANES, o_ref.shape[1]), 0)
    for u, w_ref in enumerate(w_refs):
        valid = valid_ref[pl.program_id(0) * REGROUP_UNITS_PER_STEP + u]
        o_ref[u * LANES:(u + 1) * LANES, :] = jnp.where(row < valid, w_ref[...], 0.0).astype(BF16)


def _regroup_w_in_t(w_in_t):
    d = w_in_t.shape[1]
    ups = REGROUP_UNITS_PER_STEP

    def window(u):
        return pl.BlockSpec((pl.Element(LANES), pl.Element(d)),
                            lambda b, start, valid: (start[b * ups + u] * REGROUP_ALIGN, 0))

    grid_spec = pltpu.PrefetchScalarGridSpec(
        num_scalar_prefetch=2,
        grid=(PROJ_UNITS // ups,),
        in_specs=[window(u) for u in range(ups)],
        out_specs=pl.BlockSpec((ups * LANES, d), lambda b, *_: (b, 0)),
    )
    return pl.pallas_call(
        _regroup_kernel,
        grid_spec=grid_spec,
        out_shape=jax.ShapeDtypeStruct((PROJ_WIDTH, d), BF16),
        compiler_params=_cparams("arbitrary"),
        name="regroup_w_in",
    )(jnp.asarray(REGROUP_START // REGROUP_ALIGN), jnp.asarray(REGROUP_VALID), *([w_in_t] * ups))


def _pad_heads(w, n_heads, width, pad_to):
    r = w.shape[0]
    w = w.reshape(r, n_heads, width)
    w = jnp.pad(w, ((0, 0), (0, 0), (0, pad_to - width)))
    return w.reshape(r, n_heads * pad_to)


class _TilePlan(NamedTuple):
    in_proj_rows: int
    in_proj_cols: int
    attn_a_q: int
    attn_b_q: int
    merge_rows: int


def _tile_plan(seq):
    plan = _TilePlan(in_proj_rows=min(1024, seq), in_proj_cols=2304, attn_a_q=min(512, seq), attn_b_q=min(512, seq),
                     merge_rows=min(256, seq))
    assert seq % KV_TILE == 0 and PROJ_WIDTH % plan.in_proj_cols == 0
    assert all(seq % t == 0 for t in (plan.in_proj_rows, plan.attn_a_q, plan.attn_b_q, plan.merge_rows))
    return plan


def kernel(x, mem, positions, rel_bias, norm_g, mem_norm_g, w_in, a_q_norm_g, a_k_norm_g, b_q_lat_norm_g,
           b_kv_lat_norm_g, w_b_uq, w_b_ukv, b_q_norm_g, b_k_norm_g, w_mem_kv, c_q_norm_g, c_k_norm_g,
           w_branch, w_out):
    bsz, seq, d = x.shape
    assert d == D_MODEL and norm_g.shape[0] == 1
    topk = min(TOPK_MAX, seq // 4)
    tiles = _tile_plan(seq)
    tq_a, tq_b, tm_in = tiles.attn_a_q, tiles.attn_b_q, tiles.in_proj_rows

    inv_freq = 1.0 / (ROPE_THETA ** (jnp.arange(0, B_ROPE, 2, dtype=F32) / B_ROPE))
    zeros_half = jnp.zeros((B_ROPE,), F32)
    gq_pad = jnp.concatenate([b_q_norm_g[0], zeros_half])[None, :]
    gk_pad = jnp.concatenate([b_k_norm_g[0], zeros_half])[None, :]
    lut_t = ((rel_bias[jnp.asarray(T5_TABLE)] - rel_bias[REL_BUCKETS - 1]).T * LOG2E).astype(F32)

    w_cat = _regroup_w_in_t(jnp.transpose(w_in[0]))
    wuq = _pad_heads(w_b_uq[0], B_HEADS, B_QK_DIM, B_QK_PAD).astype(BF16)
    wukv = w_b_ukv[0].reshape(B_KV_RANK, B_HEADS, B_NOPE + B_V)
    wuk = wukv[:, :, :B_NOPE].reshape(B_KV_RANK, B_HEADS * B_NOPE).astype(BF16)
    wuvt = jnp.transpose(wukv[:, :, B_NOPE:], (1, 2, 0)).astype(BF16)
    wbr = w_branch[0].astype(BF16)
    wout = w_out[0].astype(BF16)

    outs = []
    for b in range(bsz):
        pos = positions[b]
        ang = pos.astype(F32)[:, None] * inv_freq
        cos, sin = jnp.cos(ang), jnp.sin(ang)
        zpad = jnp.zeros((seq, LANES - B_ROPE), F32)
        cos128 = jnp.concatenate([cos, cos, zpad], axis=1)
        sin128 = jnp.concatenate([-sin, sin, zpad], axis=1)

        mem_kv = _mem_kv(mem[b], mem_norm_g[0][None, :], w_mem_kv[0], c_k_norm_g[0][None, :])
        proj = _in_proj(x[b], norm_g[0][None, :], w_cat, tm=tm_in, tn=tiles.in_proj_cols)
        aqn, akn, kab, wt, avt, qb, kb, vbt, cqn, knorm2 = _prep(
            proj, cos128, sin128, a_q_norm_g[0][None, :], a_k_norm_g[0][None, :],
            b_q_lat_norm_g[0][None, :], b_kv_lat_norm_g[0][None, :], gq_pad, gk_pad,
            c_q_norm_g[0][None, :], wuq, wuk, wuvt)
        o_b = _attn_b(qb, kb, vbt, knorm2, tq=tq_b)
        qmin = jnp.min(pos.reshape(seq // tq_a, tq_a), axis=1)
        kmax = jnp.max(pos.reshape(seq // KV_TILE, KV_TILE), axis=1)
        o_a = _attn_a(proj, wt, kab, aqn, akn, avt, pos[None, :], pos[:, None],
                      lut_t, knorm2, qmin, kmax, tq=tq_a, topk=topk)
        outs.append(_merge_out(x[b], o_a, o_b, cqn, mem_kv, proj, wbr, wout, tm=tiles.merge_rows))
    return jnp.stack(outs, axis=0)
```
